```python
import jax
import jax.numpy as jnp
from jax import lax
import numpy as np

D_MODEL = 2048
BATCH = 8
SEQ = 4096
DEPTH = 4

N_MIXERS = 4
N_META = 16
EPS = 1e-6
D_FF = 256 * ((8 * D_MODEL // 3 + 255) // 256)

MLSTM_HEADS = 8
MLSTM_DV = D_MODEL // MLSTM_HEADS
MLSTM_DK = MLSTM_DV // 2
MLSTM_CHUNK = 64
MLSTM_QK_W = MLSTM_HEADS * MLSTM_DK
MLSTM_V_W = MLSTM_HEADS * MLSTM_DV
MLSTM_IN = 2 * MLSTM_QK_W + 2 * MLSTM_V_W + 2 * MLSTM_HEADS

POOL_WINDOWS = (2, 4, 8, 16)
POOL_GROUP = D_MODEL // len(POOL_WINDOWS)

GDN_DK = 128
GDN_DV = 128
GDN_QK_HEADS = D_MODEL // GDN_DK
GDN_V_HEADS = 2 * GDN_QK_HEADS
GDN_CONV = 4
GDN_CHUNK = 64
GDN_QK_W = GDN_QK_HEADS * GDN_DK
GDN_V_W = GDN_V_HEADS * GDN_DV
GDN_CONV_CH = 2 * GDN_QK_W + GDN_V_W
GDN_IN = GDN_CONV_CH + GDN_V_W + 2 * GDN_V_HEADS

SWA_DH = 64
SWA_HQ = D_MODEL // SWA_DH
SWA_GROUP = 8
SWA_HKV = SWA_HQ // SWA_GROUP
SWA_WINDOW = 128
SWA_BLOCK = SWA_WINDOW
SWA_Q_W = SWA_HQ * SWA_DH
SWA_KV_W = SWA_HKV * SWA_DH
SWA_IN = SWA_Q_W + 2 * SWA_KV_W
ROPE_THETA = 10000.0

kernel_name = 'hybrid_interleaved_mlstm_pool_gdn_swa'


def _n_layers_of(mixer):
    return len(range(mixer, DEPTH, N_MIXERS))


def _rms_normalize(x):
    xf = x.astype(jnp.float32)
    return xf * lax.rsqrt(jnp.mean(xf * xf, axis=-1, keepdims=True) + EPS)


def _rmsnorm(x, w):
    return (_rms_normalize(x) * w.astype(jnp.float32)).astype(x.dtype)


def _swiglu(x, w_gate, w_up, w_down):
    return (jax.nn.silu(x @ w_gate) * (x @ w_up)) @ w_down


def _to_chunks(t, c):
    t = t.reshape(t.shape[0], t.shape[1], t.shape[2] // c, c, *t.shape[3:])
    return jnp.moveaxis(t, 2, 0)


def _from_chunks(t):
    t = jnp.moveaxis(t, 0, 2)
    return t.reshape(t.shape[0], t.shape[1], -1, *t.shape[4:])


def _meta_then_chunks(step, state, seqs, chunk):
    meta = tuple(s[:, :, :N_META] for s in seqs)
    real = tuple(_to_chunks(s[:, :, N_META:], chunk) for s in seqs)
    state, out_meta = step(state, meta)
    _, out_real = lax.scan(step, state, real)
    return jnp.concatenate([out_meta, _from_chunks(out_real)], axis=2)


def _mlstm_chunk(state, xs):
    c_st, n_st, m_st = state
    q, k, v, li, lf = xs
    c = q.shape[2]
    causal = jnp.tril(jnp.ones((c, c), bool))
    b = jnp.cumsum(lf, axis=-1)
    log_w = jnp.where(causal, b[..., :, None] - b[..., None, :] + li[..., None, :], -jnp.inf)
    log_init = b + m_st[..., None]
    m_t = jnp.maximum(log_init, jnp.max(log_w, axis=-1))
    w = jnp.exp(log_w - m_t[..., None])
    w_init = jnp.exp(log_init - m_t)
    qk = jnp.einsum('bhtd,bhsd->bhts', q, k) * w
    num = w_init[..., None] * jnp.einsum('bhtd,bhde->bhte', q, c_st) + jnp.einsum('bhts,bhse->bhte', qk, v)
    den = w_init * jnp.einsum('bhtd,bhd->bht', q, n_st) + jnp.sum(qk, axis=-1)
    h = num / jnp.maximum(jnp.abs(den), jnp.exp(-m_t))[..., None]
    log_end_init = b[..., -1] + m_st
    log_end = b[..., -1:] - b + li
    m_new = jnp.maximum(log_end_init, jnp.max(log_end, axis=-1))
    a_init = jnp.exp(log_end_init - m_new)
    a = jnp.exp(log_end - m_new[..., None])
    c_new = a_init[..., None, None] * c_st + jnp.einsum('bhs,bhsd,bhse->bhde', a, k, v)
    n_new = a_init[..., None] * n_st + jnp.einsum('bhs,bhsd->bhd', a, k)
    return (c_new, n_new, m_new), h


def _mlstm(u, w_in, b_if, norm_w, w_out):
    bsz, L, _ = u.shape
    p = u @ w_in
    q, k, v, og, gates = jnp.split(p, [MLSTM_QK_W, 2 * MLSTM_QK_W, 2 * MLSTM_QK_W + MLSTM_V_W,
                                       2 * MLSTM_QK_W + 2 * MLSTM_V_W], axis=-1)
    heads = lambda t, d: t.reshape(bsz, L, MLSTM_HEADS, d).transpose(0, 2, 1, 3).astype(jnp.float32)
    q = heads(q, MLSTM_DK) * (MLSTM_DK ** -0.5)
    k = heads(k, MLSTM_DK)
    v = heads(v, MLSTM_DV)
    gates = (gates.astype(jnp.float32) + b_if.astype(jnp.float32)).transpose(0, 2, 1)
    li = gates[:, :MLSTM_HEADS]
    lf = jax.nn.log_sigmoid(gates[:, MLSTM_HEADS:])
    state0 = (jnp.zeros((bsz, MLSTM_HEADS, MLSTM_DK, MLSTM_DV), jnp.float32),
              jnp.zeros((bsz, MLSTM_HEADS, MLSTM_DK), jnp.float32),
              jnp.zeros((bsz, MLSTM_HEADS), jnp.float32))
    hh = _meta_then_chunks(_mlstm_chunk, state0, (q, k, v, li, lf), MLSTM_CHUNK)
    hh = _rms_normalize(hh.transpose(0, 2, 1, 3)).reshape(bsz, L, MLSTM_V_W)
    hh = hh * norm_w.astype(jnp.float32) * jax.nn.sigmoid(og.astype(jnp.float32))
    return hh.astype(u.dtype) @ w_out


def _pool_mixer(u, w_group, scale):
    bsz, L, _ = u.shape
    uf = u.astype(jnp.float32)
    cs = jnp.cumsum(uf, axis=1)
    count = jnp.arange(L) + 1
    outs = []
    for gi, win in enumerate(POOL_WINDOWS):
        lo, hi = gi * POOL_GROUP, (gi + 1) * POOL_GROUP
        c = cs[..., lo:hi]
        lag = jnp.pad(c[:, :L - win], ((0, 0), (win, 0), (0, 0)))
        mean = (c - lag) / jnp.minimum(count, win).astype(jnp.float32)[None, :, None]
        outs.append(mean - uf[..., lo:hi])
    pooled = jnp.stack(outs, axis=2).astype(u.dtype)
    y = jnp.einsum('blgc,gcd->blgd', pooled, w_group).reshape(bsz, L, D_MODEL)
    return y * scale


def _causal_depthwise_conv(x, w):
    return lax.conv_general_dilated(x, w[:, None, :], window_strides=(1,), padding=[(w.shape[0] - 1, 0)],
                                    dimension_numbers=('NWC', 'WIO', 'NWC'), feature_group_count=x.shape[-1])


def _l2norm(x):
    return x * lax.rsqrt(jnp.sum(x * x, axis=-1, keepdims=True) + EPS)


def _gdn_chunk(s_st, xs):
    q, k, v, g, beta = xs
    c = q.shape[2]
    causal = jnp.tril(jnp.ones((c, c), bool))
    strict = jnp.tril(jnp.ones((c, c), bool), -1)
    gc = jnp.cumsum(g, axis=-1)
    decay = jnp.exp(jnp.where(causal, gc[..., :, None] - gc[..., None, :], -jnp.inf))
    kb = k * beta[..., None]
    lower = jnp.where(strict, jnp.einsum('bhtd,bhsd->bhts', kb, k) * decay, 0.0)
    a_mat = lower + jnp.eye(c, dtype=lower.dtype)
    rhs = jnp.concatenate([v * beta[..., None], kb * jnp.exp(gc)[..., None]], axis=-1)
    sol = lax.linalg.triangular_solve(a_mat, rhs, left_side=True, lower=True, unit_diagonal=True)
    u_vec, w_vec = sol[..., :GDN_DV], sol[..., GDN_DV:]
    v_new = u_vec - jnp.einsum('bhtd,bhde->bhte', w_vec, s_st)
    attn = jnp.einsum('bhtd,bhsd->bhts', q, k) * decay
    o = jnp.einsum('bhtd,bhde->bhte', q * jnp.exp(gc)[..., None], s_st) + jnp.einsum('bhts,bhse->bhte', attn, v_new)
    g_last = gc[..., -1]
    s_new = jnp.exp(g_last)[..., None, None] * s_st + jnp.einsum(
        'bhsd,bhse->bhde', k * jnp.exp(g_last[..., None] - gc)[..., None], v_new)
    return s_new, o


def _gated_deltanet(u, w_in, conv_w, a_log, dt_bias, norm_w, w_out):
    bsz, L, _ = u.shape
    p = u @ w_in
    qkv, z, b_pre, a_pre = jnp.split(p, [GDN_CONV_CH, GDN_CONV_CH + GDN_V_W, GDN_CONV_CH + GDN_V_W + GDN_V_HEADS], axis=-1)
    qkv = jax.nn.silu(_causal_depthwise_conv(qkv, conv_w))
    q, k, v = jnp.split(qkv, [GDN_QK_W, 2 * GDN_QK_W], axis=-1)
    heads = lambda t, n, d: t.reshape(bsz, L, n, d).transpose(0, 2, 1, 3).astype(jnp.float32)
    rep = GDN_V_HEADS // GDN_QK_HEADS
    q = jnp.repeat(_l2norm(heads(q, GDN_QK_HEADS, GDN_DK)) * (GDN_DK ** -0.5), rep, axis=1)
    k = jnp.repeat(_l2norm(heads(k, GDN_QK_HEADS, GDN_DK)), rep, axis=1)
    v = heads(v, GDN_V_HEADS, GDN_DV)
    beta = jax.nn.sigmoid(b_pre.astype(jnp.float32)).transpose(0, 2, 1)
    g = (-jnp.exp(a_log.astype(jnp.float32))
         * jax.nn.softplus(a_pre.astype(jnp.float32) + dt_bias.astype(jnp.float32))).transpose(0, 2, 1)
    s0 = jnp.zeros((bsz, GDN_V_HEADS, GDN_DK, GDN_DV), jnp.float32)
    o = _meta_then_chunks(_gdn_chunk, s0, (q, k, v, g, beta), GDN_CHUNK)
    o = _rms_normalize(o.transpose(0, 2, 1, 3)) * norm_w.astype(jnp.float32)
    o = o * jax.nn.silu(z.astype(jnp.float32).reshape(bsz, L, GDN_V_HEADS, GDN_DV))
    return o.reshape(bsz, L, GDN_V_W).astype(u.dtype) @ w_out


def _rope_tables(L, d):
    inv = ROPE_THETA ** (-jnp.arange(0, d, 2, dtype=jnp.float32) / d)
    ang = jnp.arange(L, dtype=jnp.float32)[:, None] * inv[None, :]
    ang = jnp.concatenate([ang, ang], axis=-1)
    return jnp.cos(ang), jnp.sin(ang)


def _rope(x, cos, sin):
    shape = (1, x.shape[1]) + (1,) * (x.ndim - 3) + (x.shape[-1],)
    cos, sin = cos.reshape(shape), sin.reshape(shape)
    x1, x2 = jnp.split(x, 2, axis=-1)
    return x * cos + jnp.concatenate([-x2, x1], axis=-1) * sin


def _swa_sinks(u, w_qkv, b_qkv, sinks, w_out, b_out):
    bsz, L, _ = u.shape
    p = (u @ w_qkv + b_qkv).astype(jnp.float32)
    q, k, v = jnp.split(p, [SWA_Q_W, SWA_Q_W + SWA_KV_W], axis=-1)
    q = q.reshape(bsz, L, SWA_HKV, SWA_GROUP, SWA_DH)
    k = k.reshape(bsz, L, SWA_HKV, SWA_DH)
    v = v.reshape(bsz, L, SWA_HKV, SWA_DH)
    cos, sin = _rope_tables(L, SWA_DH)
    q, k = _rope(q, cos, sin), _rope(k, cos, sin)
    nb = -(-L // SWA_BLOCK)
    lp = nb * SWA_BLOCK
    pad_end = lambda t: jnp.pad(t, ((0, 0), (0, lp - L)) + ((0, 0),) * (t.ndim - 2))
    q, k, v = pad_end(q), pad_end(k), pad_end(v)
    qb = q.reshape(bsz, nb, SWA_BLOCK, SWA_HKV, SWA_GROUP, SWA_DH)

    def band(t):
        tp = jnp.pad(t, ((0, 0), (SWA_BLOCK, 0), (0, 0), (0, 0)))
        prev = tp[:, :lp].reshape(bsz, nb, SWA_BLOCK, SWA_HKV, SWA_DH)
        cur = tp[:, SWA_BLOCK:].reshape(bsz, nb, SWA_BLOCK, SWA_HKV, SWA_DH)
        return jnp.concatenate([prev, cur], axis=2)

    kb, vb = band(k), band(v)
    s = jnp.einsum('bnqhgd,bnkhd->bnhgqk', qb, kb) * (SWA_DH ** -0.5)
    blk = jnp.arange(nb)[:, None, None] * SWA_BLOCK
    qpos = blk + jnp.arange(SWA_BLOCK)[None, :, None]
    kpos = blk - SWA_BLOCK + jnp.arange(2 * SWA_BLOCK)[None, None, :]
    mask = (kpos <= qpos) & (qpos - kpos < SWA_WINDOW) & (kpos >= 0)
    s = jnp.where(mask[None, :, None, None], s, -jnp.inf)
    sink = jnp.broadcast_to(sinks.astype(jnp.float32).reshape(1, 1, SWA_HKV, SWA_GROUP, 1, 1), s.shape[:-1] + (1,))
    prob = jax.nn.softmax(jnp.concatenate([s, sink], axis=-1), axis=-1)[..., :-1]
    o = jnp.einsum('bnhgqk,bnkhd->bnqhgd', prob, vb).reshape(bsz, lp, SWA_Q_W)[:, :L]
    return o.astype(u.dtype) @ w_out + b_out


def _fwd_setup_inputs(seed: int = 0) -> dict:
    key = jax.random.key(seed)
    ks = iter(jax.random.split(key, 40))
    nrm = lambda shape, s=1.0: s * jax.random.normal(next(ks), shape, jnp.float32)
    dense = lambda shape: nrm(shape, shape[-2] ** -0.5)
    gain = lambda shape: 1.0 + nrm(shape, 0.02)
    na, nb, nc, nd = (_n_layers_of(m) for m in range(N_MIXERS))
    x = nrm((BATCH, SEQ, D_MODEL))
    meta_tokens = nrm((N_META, D_MODEL))
    norm_w = gain((DEPTH, 3, D_MODEL))
    ffn_w_gate = dense((DEPTH, 2, D_MODEL, D_FF))
    ffn_w_up = dense((DEPTH, 2, D_MODEL, D_FF))
    ffn_w_down = dense((DEPTH, 2, D_FF, D_MODEL))
    mlstm_w_in = dense((na, D_MODEL, MLSTM_IN))
    b_i = nrm((na, MLSTM_HEADS), 0.1)
    b_f = jnp.linspace(3.0, 6.0, MLSTM_HEADS, dtype=jnp.float32)[None, :] + nrm((na, MLSTM_HEADS), 0.1)
    mlstm_b_if = jnp.concatenate([b_i, b_f], axis=-1)
    mlstm_norm_w = gain((na, MLSTM_V_W))
    mlstm_w_out = dense((na, MLSTM_V_W, D_MODEL))
    pool_w = dense((nb, len(POOL_WINDOWS), POOL_GROUP, POOL_GROUP))
    pool_scale = gain((nb, D_MODEL))
    gdn_w_in = dense((nc, D_MODEL, GDN_IN))
    gdn_conv_w = nrm((nc, GDN_CONV, GDN_CONV_CH), GDN_CONV ** -0.5)
    gdn_a_log = jnp.log(jax.random.uniform(next(ks), (nc, GDN_V_HEADS), jnp.float32, 1.0, 16.0))
    dt = jnp.exp(jax.random.uniform(next(ks), (nc, GDN_V_HEADS), jnp.float32,
                                    float(np.log(1e-3)), float(np.log(1e-1))))
    gdn_dt_bias = dt + jnp.log(-jnp.expm1(-dt))
    gdn_norm_w = gain((nc, GDN_DV))
    gdn_w_out = dense((nc, GDN_V_W, D_MODEL))
    swa_w_qkv = dense((nd, D_MODEL, SWA_IN))
    swa_b_qkv = nrm((nd, SWA_IN), 0.02)
    swa_sinks = nrm((nd, SWA_HQ), 0.5)
    swa_w_out = dense((nd, SWA_Q_W, D_MODEL))
    swa_b_out = nrm((nd, D_MODEL), 0.02)
    final_norm_w = gain((D_MODEL,))
    return {'x': x, 'meta_tokens': meta_tokens, 'norm_w': norm_w,
            'ffn_w_gate': ffn_w_gate, 'ffn_w_up': ffn_w_up, 'ffn_w_down': ffn_w_down,
            'mlstm_w_in': mlstm_w_in, 'mlstm_b_if': mlstm_b_if, 'mlstm_norm_w': mlstm_norm_w, 'mlstm_w_out': mlstm_w_out,
            'pool_w': pool_w, 'pool_scale': pool_scale,
            'gdn_w_in': gdn_w_in, 'gdn_conv_w': gdn_conv_w, 'gdn_a_log': gdn_a_log, 'gdn_dt_bias': gdn_dt_bias,
            'gdn_norm_w': gdn_norm_w, 'gdn_w_out': gdn_w_out,
            'swa_w_qkv': swa_w_qkv, 'swa_b_qkv': swa_b_qkv, 'swa_sinks': swa_sinks, 'swa_w_out': swa_w_out,
            'swa_b_out': swa_b_out, 'final_norm_w': final_norm_w}


def _fwd_reference(x, meta_tokens, norm_w, ffn_w_gate, ffn_w_up, ffn_w_down,
              mlstm_w_in, mlstm_b_if, mlstm_norm_w, mlstm_w_out,
              pool_w, pool_scale,
              gdn_w_in, gdn_conv_w, gdn_a_log, gdn_dt_bias, gdn_norm_w, gdn_w_out,
              swa_w_qkv, swa_b_qkv, swa_sinks, swa_w_out, swa_b_out, final_norm_w):
    bsz = x.shape[0]
    meta = jnp.broadcast_to(meta_tokens.astype(x.dtype)[None], (bsz, N_META, D_MODEL))
    h = jnp.concatenate([meta, x], axis=1)
    for i in range(DEPTH):
        m, j = i % N_MIXERS, i // N_MIXERS
        h = h + 0.5 * _swiglu(_rmsnorm(h, norm_w[i, 0]), ffn_w_gate[i, 0], ffn_w_up[i, 0], ffn_w_down[i, 0])
        u = _rmsnorm(h, norm_w[i, 1])
        if m == 0:
            y = _mlstm(u, mlstm_w_in[j], mlstm_b_if[j], mlstm_norm_w[j], mlstm_w_out[j])
        elif m == 1:
            y = _pool_mixer(u, pool_w[j], pool_scale[j])
        elif m == 2:
            y = _gated_deltanet(u, gdn_w_in[j], gdn_conv_w[j], gdn_a_log[j], gdn_dt_bias[j], gdn_norm_w[j], gdn_w_out[j])
        else:
            y = _swa_sinks(u, swa_w_qkv[j], swa_b_qkv[j], swa_sinks[j], swa_w_out[j], swa_b_out[j])
        h = h + y
        h = h + 0.5 * _swiglu(_rmsnorm(h, norm_w[i, 2]), ffn_w_gate[i, 1], ffn_w_up[i, 1], ffn_w_down[i, 1])
    return _rmsnorm(h, final_norm_w)[:, N_META:]


import jax as _jax
import jax.numpy as _jnp

TWIN_FORMAT = 'train_step'
FWD_PARAMS = ['x', 'meta_tokens', 'norm_w', 'ffn_w_gate', 'ffn_w_up', 'ffn_w_down', 'mlstm_w_in', 'mlstm_b_if', 'mlstm_norm_w', 'mlstm_w_out', 'pool_w', 'pool_scale', 'gdn_w_in', 'gdn_conv_w', 'gdn_a_log', 'gdn_dt_bias', 'gdn_norm_w', 'gdn_w_out', 'swa_w_qkv', 'swa_b_qkv', 'swa_sinks', 'swa_w_out', 'swa_b_out', 'final_norm_w']
TWIN_WEIGHTS = ['meta_tokens', 'norm_w', 'ffn_w_gate', 'ffn_w_up', 'ffn_w_down', 'mlstm_w_in', 'mlstm_b_if', 'mlstm_norm_w', 'mlstm_w_out', 'pool_w', 'pool_scale', 'gdn_w_in', 'gdn_conv_w', 'gdn_a_log', 'gdn_dt_bias', 'gdn_norm_w', 'gdn_w_out', 'swa_w_qkv', 'swa_b_qkv', 'swa_sinks', 'swa_w_out', 'swa_b_out', 'final_norm_w']
TWIN_DIFF_INPUT = 'x'
TWIN_INPUTS = ['x', 'meta_tokens', 'norm_w', 'ffn_w_gate', 'ffn_w_up', 'ffn_w_down', 'mlstm_w_in', 'mlstm_b_if', 'mlstm_norm_w', 'mlstm_w_out', 'pool_w', 'pool_scale', 'gdn_w_in', 'gdn_conv_w', 'gdn_a_log', 'gdn_dt_bias', 'gdn_norm_w', 'gdn_w_out', 'swa_w_qkv', 'swa_b_qkv', 'swa_sinks', 'swa_w_out', 'swa_b_out', 'final_norm_w', 'loss_target', 'm_meta_tokens', 'm_norm_w', 'm_ffn_w_gate', 'm_ffn_w_up', 'm_ffn_w_down', 'm_mlstm_w_in', 'm_mlstm_b_if', 'm_mlstm_norm_w', 'm_mlstm_w_out', 'm_pool_w', 'm_pool_scale', 'm_gdn_w_in', 'm_gdn_conv_w', 'm_gdn_a_log', 'm_gdn_dt_bias', 'm_gdn_norm_w', 'm_gdn_w_out', 'm_swa_w_qkv', 'm_swa_b_qkv', 'm_swa_sinks', 'm_swa_w_out', 'm_swa_b_out', 'm_final_norm_w', 'v_meta_tokens', 'v_norm_w', 'v_ffn_w_gate', 'v_ffn_w_up', 'v_ffn_w_down', 'v_mlstm_w_in', 'v_mlstm_b_if', 'v_mlstm_norm_w', 'v_mlstm_w_out', 'v_pool_w', 'v_pool_scale', 'v_gdn_w_in', 'v_gdn_conv_w', 'v_gdn_a_log', 'v_gdn_dt_bias', 'v_gdn_norm_w', 'v_gdn_w_out', 'v_swa_w_qkv', 'v_swa_b_qkv', 'v_swa_sinks', 'v_swa_w_out', 'v_swa_b_out', 'v_final_norm_w']
TWIN_OUTPUTS = ['loss', 'grad_x', 'grad_meta_tokens', 'grad_norm_w', 'grad_ffn_w_gate', 'grad_ffn_w_up', 'grad_ffn_w_down', 'grad_mlstm_w_in', 'grad_mlstm_b_if', 'grad_mlstm_norm_w', 'grad_mlstm_w_out', 'grad_pool_w', 'grad_pool_scale', 'grad_gdn_w_in', 'grad_gdn_conv_w', 'grad_gdn_a_log', 'grad_gdn_dt_bias', 'grad_gdn_norm_w', 'grad_gdn_w_out', 'grad_swa_w_qkv', 'grad_swa_b_qkv', 'grad_swa_sinks', 'grad_swa_w_out', 'grad_swa_b_out', 'grad_final_norm_w', 'delta_meta_tokens', 'delta_norm_w', 'delta_ffn_w_gate', 'delta_ffn_w_up', 'delta_ffn_w_down', 'delta_mlstm_w_in', 'delta_mlstm_b_if', 'delta_mlstm_norm_w', 'delta_mlstm_w_out', 'delta_pool_w', 'delta_pool_scale', 'delta_gdn_w_in', 'delta_gdn_conv_w', 'delta_gdn_a_log', 'delta_gdn_dt_bias', 'delta_gdn_norm_w', 'delta_gdn_w_out', 'delta_swa_w_qkv', 'delta_swa_b_qkv', 'delta_swa_sinks', 'delta_swa_w_out', 'delta_swa_b_out', 'delta_final_norm_w', 'new_m_meta_tokens', 'new_m_norm_w', 'new_m_ffn_w_gate', 'new_m_ffn_w_up', 'new_m_ffn_w_down', 'new_m_mlstm_w_in', 'new_m_mlstm_b_if', 'new_m_mlstm_norm_w', 'new_m_mlstm_w_out', 'new_m_pool_w', 'new_m_pool_scale', 'new_m_gdn_w_in', 'new_m_gdn_conv_w', 'new_m_gdn_a_log', 'new_m_gdn_dt_bias', 'new_m_gdn_norm_w', 'new_m_gdn_w_out', 'new_m_swa_w_qkv', 'new_m_swa_b_qkv', 'new_m_swa_sinks', 'new_m_swa_w_out', 'new_m_swa_b_out', 'new_m_final_norm_w', 'new_v_meta_tokens', 'new_v_norm_w', 'new_v_ffn_w_gate', 'new_v_ffn_w_up', 'new_v_ffn_w_down', 'new_v_mlstm_w_in', 'new_v_mlstm_b_if', 'new_v_mlstm_norm_w', 'new_v_mlstm_w_out', 'new_v_pool_w', 'new_v_pool_scale', 'new_v_gdn_w_in', 'new_v_gdn_conv_w', 'new_v_gdn_a_log', 'new_v_gdn_dt_bias', 'new_v_gdn_norm_w', 'new_v_gdn_w_out', 'new_v_swa_w_qkv', 'new_v_swa_b_qkv', 'new_v_swa_sinks', 'new_v_swa_w_out', 'new_v_swa_b_out', 'new_v_final_norm_w']
TWIN_LEAF_KINDS = {'loss': 'loss', 'grad_x': 'grad_x', 'grad_meta_tokens': 'grad_w', 'grad_norm_w': 'grad_w', 'grad_ffn_w_gate': 'grad_w', 'grad_ffn_w_up': 'grad_w', 'grad_ffn_w_down': 'grad_w', 'grad_mlstm_w_in': 'grad_w', 'grad_mlstm_b_if': 'grad_w', 'grad_mlstm_norm_w': 'grad_w', 'grad_mlstm_w_out': 'grad_w', 'grad_pool_w': 'grad_w', 'grad_pool_scale': 'grad_w', 'grad_gdn_w_in': 'grad_w', 'grad_gdn_conv_w': 'grad_w', 'grad_gdn_a_log': 'grad_w', 'grad_gdn_dt_bias': 'grad_w', 'grad_gdn_norm_w': 'grad_w', 'grad_gdn_w_out': 'grad_w', 'grad_swa_w_qkv': 'grad_w', 'grad_swa_b_qkv': 'grad_w', 'grad_swa_sinks': 'grad_w', 'grad_swa_w_out': 'grad_w', 'grad_swa_b_out': 'grad_w', 'grad_final_norm_w': 'grad_w', 'delta_meta_tokens': 'delta_w', 'delta_norm_w': 'delta_w', 'delta_ffn_w_gate': 'delta_w', 'delta_ffn_w_up': 'delta_w', 'delta_ffn_w_down': 'delta_w', 'delta_mlstm_w_in': 'delta_w', 'delta_mlstm_b_if': 'delta_w', 'delta_mlstm_norm_w': 'delta_w', 'delta_mlstm_w_out': 'delta_w', 'delta_pool_w': 'delta_w', 'delta_pool_scale': 'delta_w', 'delta_gdn_w_in': 'delta_w', 'delta_gdn_conv_w': 'delta_w', 'delta_gdn_a_log': 'delta_w', 'delta_gdn_dt_bias': 'delta_w', 'delta_gdn_norm_w': 'delta_w', 'delta_gdn_w_out': 'delta_w', 'delta_swa_w_qkv': 'delta_w', 'delta_swa_b_qkv': 'delta_w', 'delta_swa_sinks': 'delta_w', 'delta_swa_w_out': 'delta_w', 'delta_swa_b_out': 'delta_w', 'delta_final_norm_w': 'delta_w', 'new_m_meta_tokens': 'new_m', 'new_m_norm_w': 'new_m', 'new_m_ffn_w_gate': 'new_m', 'new_m_ffn_w_up': 'new_m', 'new_m_ffn_w_down': 'new_m', 'new_m_mlstm_w_in': 'new_m', 'new_m_mlstm_b_if': 'new_m', 'new_m_mlstm_norm_w': 'new_m', 'new_m_mlstm_w_out': 'new_m', 'new_m_pool_w': 'new_m', 'new_m_pool_scale': 'new_m', 'new_m_gdn_w_in': 'new_m', 'new_m_gdn_conv_w': 'new_m', 'new_m_gdn_a_log': 'new_m', 'new_m_gdn_dt_bias': 'new_m', 'new_m_gdn_norm_w': 'new_m', 'new_m_gdn_w_out': 'new_m', 'new_m_swa_w_qkv': 'new_m', 'new_m_swa_b_qkv': 'new_m', 'new_m_swa_sinks': 'new_m', 'new_m_swa_w_out': 'new_m', 'new_m_swa_b_out': 'new_m', 'new_m_final_norm_w': 'new_m', 'new_v_meta_tokens': 'new_v', 'new_v_norm_w': 'new_v', 'new_v_ffn_w_gate': 'new_v', 'new_v_ffn_w_up': 'new_v', 'new_v_ffn_w_down': 'new_v', 'new_v_mlstm_w_in': 'new_v', 'new_v_mlstm_b_if': 'new_v', 'new_v_mlstm_norm_w': 'new_v', 'new_v_mlstm_w_out': 'new_v', 'new_v_pool_w': 'new_v', 'new_v_pool_scale': 'new_v', 'new_v_gdn_w_in': 'new_v', 'new_v_gdn_conv_w': 'new_v', 'new_v_gdn_a_log': 'new_v', 'new_v_gdn_dt_bias': 'new_v', 'new_v_gdn_norm_w': 'new_v', 'new_v_gdn_w_out': 'new_v', 'new_v_swa_w_qkv': 'new_v', 'new_v_swa_b_qkv': 'new_v', 'new_v_swa_sinks': 'new_v', 'new_v_swa_w_out': 'new_v', 'new_v_swa_b_out': 'new_v', 'new_v_final_norm_w': 'new_v'}


def _forward(args):
    return _fwd_reference(*[args[k] for k in FWD_PARAMS])


def _output_shape():
    def fwd():
        inp = _fwd_setup_inputs(0)
        return _fwd_reference(*[inp[k] for k in FWD_PARAMS])
    out = _jax.eval_shape(fwd)
    return out.shape, out.dtype

N_MICROBATCH = 1
ADAM_LR = 0.001
ADAM_B1 = 0.9
ADAM_B2 = 0.999
ADAM_EPS = 1e-08
ADAM_WD = 0.01
ADAM_STEP = 10
PER_EXAMPLE_BATCH_AXIS = {'x': 0, 'loss_target': 0}
SHARED_INPUTS = []
_WEIGHT_DTYPES = {'meta_tokens': _jnp.float32, 'norm_w': _jnp.float32, 'ffn_w_gate': _jnp.float32, 'ffn_w_up': _jnp.float32, 'ffn_w_down': _jnp.float32, 'mlstm_w_in': _jnp.float32, 'mlstm_b_if': _jnp.float32, 'mlstm_norm_w': _jnp.float32, 'mlstm_w_out': _jnp.float32, 'pool_w': _jnp.float32, 'pool_scale': _jnp.float32, 'gdn_w_in': _jnp.float32, 'gdn_conv_w': _jnp.float32, 'gdn_a_log': _jnp.float32, 'gdn_dt_bias': _jnp.float32, 'gdn_norm_w': _jnp.float32, 'gdn_w_out': _jnp.float32, 'swa_w_qkv': _jnp.float32, 'swa_b_qkv': _jnp.float32, 'swa_sinks': _jnp.float32, 'swa_w_out': _jnp.float32, 'swa_b_out': _jnp.float32, 'final_norm_w': _jnp.float32}
MOMENT_SCALE = {'meta_tokens': 7.386847e-03, 'norm_w': 4.914290e-02, 'ffn_w_gate': 1.511279e-02, 'ffn_w_up': 1.464451e-02, 'ffn_w_down': 2.427727e-02, 'mlstm_w_in': 6.313513e-02, 'mlstm_b_if': 3.223801e-01, 'mlstm_norm_w': 5.345519e-02, 'mlstm_w_out': 5.254457e-02, 'pool_w': 6.187510e-02, 'pool_scale': 1.351433e-01, 'gdn_w_in': 2.219964e-02, 'gdn_conv_w': 2.183103e-02, 'gdn_a_log': 9.087634e-02, 'gdn_dt_bias': 8.736622e-02, 'gdn_norm_w': 1.392504e-01, 'gdn_w_out': 3.245364e-02, 'swa_w_qkv': 1.361152e-02, 'swa_b_qkv': 5.698509e-02, 'swa_sinks': 7.834127e-04, 'swa_w_out': 9.311647e-03, 'swa_b_out': 5.679545e-02, 'final_norm_w': 1.603793e+01}


def _to_microbatches(a, axis):
    t = _jnp.moveaxis(a, axis, 0)
    t = t.reshape((N_MICROBATCH, t.shape[0] // N_MICROBATCH) + t.shape[1:])
    return _jnp.moveaxis(t, 1, axis + 1)


def setup_inputs(seed: int = 0) -> dict:
    inp = _fwd_setup_inputs(seed)
    key = _jax.random.fold_in(_jax.random.key(seed), 7919)
    shape, _ = _output_shape()
    out = dict(inp)
    out["loss_target"] = _jax.random.normal(_jax.random.fold_in(key, 0), shape, _jnp.float32)
    for i, name in enumerate(TWIN_WEIGHTS):
        w = inp[name].astype(_jnp.float32)
        if MOMENT_SCALE is None:
            s = _jnp.sqrt(_jnp.mean(_jnp.square(w)) + 1e-30)
        else:
            s = MOMENT_SCALE[name]
        km, kv = _jax.random.split(_jax.random.fold_in(key, i + 1))
        out[name] = w
        out["m_" + name] = s * _jax.random.normal(km, w.shape, _jnp.float32)
        out["v_" + name] = (s * s) * _jax.random.uniform(kv, w.shape, _jnp.float32, 0.5, 1.5)
    if N_MICROBATCH > 1:
        for name, axis in PER_EXAMPLE_BATCH_AXIS.items():
            out[name] = _to_microbatches(out[name], axis)
    return {'x': out['x'], 'meta_tokens': out['meta_tokens'], 'norm_w': out['norm_w'], 'ffn_w_gate': out['ffn_w_gate'], 'ffn_w_up': out['ffn_w_up'], 'ffn_w_down': out['ffn_w_down'], 'mlstm_w_in': out['mlstm_w_in'], 'mlstm_b_if': out['mlstm_b_if'], 'mlstm_norm_w': out['mlstm_norm_w'], 'mlstm_w_out': out['mlstm_w_out'], 'pool_w': out['pool_w'], 'pool_scale': out['pool_scale'], 'gdn_w_in': out['gdn_w_in'], 'gdn_conv_w': out['gdn_conv_w'], 'gdn_a_log': out['gdn_a_log'], 'gdn_dt_bias': out['gdn_dt_bias'], 'gdn_norm_w': out['gdn_norm_w'], 'gdn_w_out': out['gdn_w_out'], 'swa_w_qkv': out['swa_w_qkv'], 'swa_b_qkv': out['swa_b_qkv'], 'swa_sinks': out['swa_sinks'], 'swa_w_out': out['swa_w_out'], 'swa_b_out': out['swa_b_out'], 'final_norm_w': out['final_norm_w'], 'loss_target': out['loss_target'], 'm_meta_tokens': out['m_meta_tokens'], 'm_norm_w': out['m_norm_w'], 'm_ffn_w_gate': out['m_ffn_w_gate'], 'm_ffn_w_up': out['m_ffn_w_up'], 'm_ffn_w_down': out['m_ffn_w_down'], 'm_mlstm_w_in': out['m_mlstm_w_in'], 'm_mlstm_b_if': out['m_mlstm_b_if'], 'm_mlstm_norm_w': out['m_mlstm_norm_w'], 'm_mlstm_w_out': out['m_mlstm_w_out'], 'm_pool_w': out['m_pool_w'], 'm_pool_scale': out['m_pool_scale'], 'm_gdn_w_in': out['m_gdn_w_in'], 'm_gdn_conv_w': out['m_gdn_conv_w'], 'm_gdn_a_log': out['m_gdn_a_log'], 'm_gdn_dt_bias': out['m_gdn_dt_bias'], 'm_gdn_norm_w': out['m_gdn_norm_w'], 'm_gdn_w_out': out['m_gdn_w_out'], 'm_swa_w_qkv': out['m_swa_w_qkv'], 'm_swa_b_qkv': out['m_swa_b_qkv'], 'm_swa_sinks': out['m_swa_sinks'], 'm_swa_w_out': out['m_swa_w_out'], 'm_swa_b_out': out['m_swa_b_out'], 'm_final_norm_w': out['m_final_norm_w'], 'v_meta_tokens': out['v_meta_tokens'], 'v_norm_w': out['v_norm_w'], 'v_ffn_w_gate': out['v_ffn_w_gate'], 'v_ffn_w_up': out['v_ffn_w_up'], 'v_ffn_w_down': out['v_ffn_w_down'], 'v_mlstm_w_in': out['v_mlstm_w_in'], 'v_mlstm_b_if': out['v_mlstm_b_if'], 'v_mlstm_norm_w': out['v_mlstm_norm_w'], 'v_mlstm_w_out': out['v_mlstm_w_out'], 'v_pool_w': out['v_pool_w'], 'v_pool_scale': out['v_pool_scale'], 'v_gdn_w_in': out['v_gdn_w_in'], 'v_gdn_conv_w': out['v_gdn_conv_w'], 'v_gdn_a_log': out['v_gdn_a_log'], 'v_gdn_dt_bias': out['v_gdn_dt_bias'], 'v_gdn_norm_w': out['v_gdn_norm_w'], 'v_gdn_w_out': out['v_gdn_w_out'], 'v_swa_w_qkv': out['v_swa_w_qkv'], 'v_swa_b_qkv': out['v_swa_b_qkv'], 'v_swa_sinks': out['v_swa_sinks'], 'v_swa_w_out': out['v_swa_w_out'], 'v_swa_b_out': out['v_swa_b_out'], 'v_final_norm_w': out['v_final_norm_w']}


def _loss(weights, diff, rest, loss_target):
    with _jax.named_scope("forward"):
        args = {**rest, TWIN_DIFF_INPUT: diff, **{k: w.astype(_WEIGHT_DTYPES[k]) for k, w in weights.items()}}
        y = _forward(args)
    with _jax.named_scope("loss_head"):
        err = _jnp.square(y.astype(_jnp.float32) - loss_target)
        return 0.5 * _jnp.sum(_jnp.mean(err, axis=-1)) if err.ndim else 0.5 * err


def _adamw(w, g, m, v):
    m = ADAM_B1 * m + (1.0 - ADAM_B1) * g
    v = ADAM_B2 * v + (1.0 - ADAM_B2) * _jnp.square(g)
    m_hat = m / (1.0 - ADAM_B1 ** ADAM_STEP)
    v_hat = v / (1.0 - ADAM_B2 ** ADAM_STEP)
    delta = -ADAM_LR * (m_hat / (_jnp.sqrt(v_hat) + ADAM_EPS) + ADAM_WD * w)
    return delta, m, v


def reference(x, meta_tokens, norm_w, ffn_w_gate, ffn_w_up, ffn_w_down, mlstm_w_in, mlstm_b_if, mlstm_norm_w, mlstm_w_out, pool_w, pool_scale, gdn_w_in, gdn_conv_w, gdn_a_log, gdn_dt_bias, gdn_norm_w, gdn_w_out, swa_w_qkv, swa_b_qkv, swa_sinks, swa_w_out, swa_b_out, final_norm_w, loss_target, m_meta_tokens, m_norm_w, m_ffn_w_gate, m_ffn_w_up, m_ffn_w_down, m_mlstm_w_in, m_mlstm_b_if, m_mlstm_norm_w, m_mlstm_w_out, m_pool_w, m_pool_scale, m_gdn_w_in, m_gdn_conv_w, m_gdn_a_log, m_gdn_dt_bias, m_gdn_norm_w, m_gdn_w_out, m_swa_w_qkv, m_swa_b_qkv, m_swa_sinks, m_swa_w_out, m_swa_b_out, m_final_norm_w, v_meta_tokens, v_norm_w, v_ffn_w_gate, v_ffn_w_up, v_ffn_w_down, v_mlstm_w_in, v_mlstm_b_if, v_mlstm_norm_w, v_mlstm_w_out, v_pool_w, v_pool_scale, v_gdn_w_in, v_gdn_conv_w, v_gdn_a_log, v_gdn_dt_bias, v_gdn_norm_w, v_gdn_w_out, v_swa_w_qkv, v_swa_b_qkv, v_swa_sinks, v_swa_w_out, v_swa_b_out, v_final_norm_w):
    given = dict(x=x, meta_tokens=meta_tokens, norm_w=norm_w, ffn_w_gate=ffn_w_gate, ffn_w_up=ffn_w_up, ffn_w_down=ffn_w_down, mlstm_w_in=mlstm_w_in, mlstm_b_if=mlstm_b_if, mlstm_norm_w=mlstm_norm_w, mlstm_w_out=mlstm_w_out, pool_w=pool_w, pool_scale=pool_scale, gdn_w_in=gdn_w_in, gdn_conv_w=gdn_conv_w, gdn_a_log=gdn_a_log, gdn_dt_bias=gdn_dt_bias, gdn_norm_w=gdn_norm_w, gdn_w_out=gdn_w_out, swa_w_qkv=swa_w_qkv, swa_b_qkv=swa_b_qkv, swa_sinks=swa_sinks, swa_w_out=swa_w_out, swa_b_out=swa_b_out, final_norm_w=final_norm_w, loss_target=loss_target, m_meta_tokens=m_meta_tokens, m_norm_w=m_norm_w, m_ffn_w_gate=m_ffn_w_gate, m_ffn_w_up=m_ffn_w_up, m_ffn_w_down=m_ffn_w_down, m_mlstm_w_in=m_mlstm_w_in, m_mlstm_b_if=m_mlstm_b_if, m_mlstm_norm_w=m_mlstm_norm_w, m_mlstm_w_out=m_mlstm_w_out, m_pool_w=m_pool_w, m_pool_scale=m_pool_scale, m_gdn_w_in=m_gdn_w_in, m_gdn_conv_w=m_gdn_conv_w, m_gdn_a_log=m_gdn_a_log, m_gdn_dt_bias=m_gdn_dt_bias, m_gdn_norm_w=m_gdn_norm_w, m_gdn_w_out=m_gdn_w_out, m_swa_w_qkv=m_swa_w_qkv, m_swa_b_qkv=m_swa_b_qkv, m_swa_sinks=m_swa_sinks, m_swa_w_out=m_swa_w_out, m_swa_b_out=m_swa_b_out, m_final_norm_w=m_final_norm_w, v_meta_tokens=v_meta_tokens, v_norm_w=v_norm_w, v_ffn_w_gate=v_ffn_w_gate, v_ffn_w_up=v_ffn_w_up, v_ffn_w_down=v_ffn_w_down, v_mlstm_w_in=v_mlstm_w_in, v_mlstm_b_if=v_mlstm_b_if, v_mlstm_norm_w=v_mlstm_norm_w, v_mlstm_w_out=v_mlstm_w_out, v_pool_w=v_pool_w, v_pool_scale=v_pool_scale, v_gdn_w_in=v_gdn_w_in, v_gdn_conv_w=v_gdn_conv_w, v_gdn_a_log=v_gdn_a_log, v_gdn_dt_bias=v_gdn_dt_bias, v_gdn_norm_w=v_gdn_norm_w, v_gdn_w_out=v_gdn_w_out, v_swa_w_qkv=v_swa_w_qkv, v_swa_b_qkv=v_swa_b_qkv, v_swa_sinks=v_swa_sinks, v_swa_w_out=v_swa_w_out, v_swa_b_out=v_swa_b_out, v_final_norm_w=v_final_norm_w)
    weights = {n: given[n] for n in TWIN_WEIGHTS}
    shared = {n: given[n] for n in SHARED_INPUTS}
    per_example = {n: given[n] for n in ['x']}
    grad_fn = _jax.value_and_grad(_loss, argnums=(0, 1))

    def one_microbatch(ex, loss_target):
        ex = dict(ex)
        diff = ex.pop(TWIN_DIFF_INPUT)
        return grad_fn(weights, diff, {**shared, **ex}, loss_target)

    if N_MICROBATCH == 1:
        loss, (grad_w, grad_x) = one_microbatch(per_example, given["loss_target"])
    else:
        def body(carry, xs):
            loss_sum, grad_sum = carry
            l_k, (gw_k, gx_k) = one_microbatch(xs[0], xs[1])
            with _jax.named_scope("update"):
                return (loss_sum + l_k, _jax.tree.map(_jnp.add, grad_sum, gw_k)), gx_k

        init = (_jnp.zeros((), _jnp.float32), _jax.tree.map(_jnp.zeros_like, weights))
        (loss, grad_w), grad_x = _jax.lax.scan(body, init, (per_example, given["loss_target"]))
    with _jax.named_scope("update"):
        delta_w, new_m, new_v = {}, {}, {}
        for n in TWIN_WEIGHTS:
            delta_w[n], new_m[n], new_v[n] = _adamw(weights[n], grad_w[n], given["m_" + n], given["v_" + n])
    return (loss, grad_x, *[grad_w[n] for n in TWIN_WEIGHTS], *[delta_w[n] for n in TWIN_WEIGHTS],
            *[new_m[n] for n in TWIN_WEIGHTS], *[new_v[n] for n in TWIN_WEIGHTS])
```

```python
import functools

import jax
import jax.numpy as jnp
from jax import lax
from jax.experimental import pallas as pl
from jax.experimental.pallas import tpu as pltpu

F32 = jnp.float32
BF16 = jnp.bfloat16

N_DEV = 8
N_META = 16
EPS = 1e-6
DEPTH = 4
MLSTM_HEADS = 8
MLSTM_CHUNK = 64
N_POOL = 4
GDN_DK = 128
GDN_CHUNK = 64
GDN_CONV = 4
SWA_DH = 64
SWA_GROUP = 8
SWA_WINDOW = 128
ROPE_THETA = 10000.0
ADAM_LR = 0.001
ADAM_B1 = 0.9
ADAM_B2 = 0.999
ADAM_EPS = 1e-08
ADAM_WD = 0.01
ADAM_STEP = 10

LANES = 128
ROW_TILE = 128
ROW_PAD = 112
IN_PAD = 896
VMEM_LIMIT = 56 * 1024 * 1024
TOKEN_TILES = (1056, 768, 512, 384, 256, 128)
FEATURE_TILES = (512, 896, 768, 640, 384, 256, 128)
ROW_TILES = (512, 352, 256, 176, 160, 128, 112, 64, 48, 32, 16, 8)
HIGHEST = lax.Precision.HIGHEST
HIGH = lax.Precision.HIGH
MESH_ID = pl.DeviceIdType.MESH

WEIGHTS = ('meta_tokens', 'norm_w', 'ffn_w_gate', 'ffn_w_up', 'ffn_w_down', 'mlstm_w_in', 'mlstm_b_if',
           'mlstm_norm_w', 'mlstm_w_out', 'pool_w', 'pool_scale', 'gdn_w_in', 'gdn_conv_w', 'gdn_a_log',
           'gdn_dt_bias', 'gdn_norm_w', 'gdn_w_out', 'swa_w_qkv', 'swa_b_qkv', 'swa_sinks', 'swa_w_out',
           'swa_b_out', 'final_norm_w')
SMALL = {'meta_tokens': 1, 'norm_w': 2, 'mlstm_b_if': None, 'mlstm_norm_w': None, 'pool_scale': 1,
         'gdn_conv_w': 2, 'gdn_a_log': None, 'gdn_dt_bias': None, 'gdn_norm_w': None, 'swa_b_qkv': 1,
         'swa_sinks': None, 'swa_b_out': 1, 'final_norm_w': None}


def _pick(n, cands):
    for c in cands:
        if n % c == 0:
            return c
    return n


def _round_up(n, m):
    return -(-n // m) * m


def _bf(x):
    return x.astype(BF16)


def _dot(a, b, dims, precision=None):
    return lax.dot_general(a, b, (dims, ((), ())), preferred_element_type=F32, precision=precision)


def _mm(a, b):
    return _dot(_bf(a), _bf(b), ((1,), (0,)))


def _mm_nt(a, b):
    return _dot(_bf(a), _bf(b), ((1,), (1,)))


def _mm_tn(a, b):
    return _dot(_bf(a), _bf(b), ((0,), (0,)))


def _mm32(a, b):
    return _dot(a, b, ((1,), (0,)), precision=HIGHEST)


def _mm3(a, b):
    return _dot(a, b, ((1,), (0,)), precision=HIGH)


def _iota(shape, axis):
    return lax.broadcasted_iota(jnp.int32, shape, axis)


def _row2col(row, eye):
    return jnp.sum(eye * row, axis=1, keepdims=True)


def _lane_roll(shift):
    @jax.custom_vjp
    def f(x):
        return pltpu.roll(x, shift, 1)

    def fwd(x):
        return f(x), None

    def bwd(_, g):
        return (pltpu.roll(g, g.shape[1] - shift, 1),)

    f.defvjp(fwd, bwd)
    return f


def mm_call(name, mode, pairs, n_acc, epilogue, out_dtypes, extras=(), tm=None, tn=None, tk=None):
    a0, b0 = pairs[0][0], pairs[0][1]
    if mode == 'nn':
        (M, K), N = a0.shape, b0.shape[1]
    elif mode == 'nt':
        (M, K), N = a0.shape, b0.shape[0]
    else:
        (K, M), N = a0.shape, b0.shape[1]
    if mode == 'tn':
        tm = tm or _pick(M, FEATURE_TILES)
        tn = tn or (N if N <= 2048 else _pick(N, FEATURE_TILES))
        tk = tk or _pick(K, TOKEN_TILES)
        dims = ((0,), (0,))
        a_spec = pl.BlockSpec((tk, tm), lambda i, j, k: (k, i))
        b_spec = pl.BlockSpec((tk, tn), lambda i, j, k: (k, j))
    else:
        tm = tm or _pick(M, TOKEN_TILES)
        tn = tn or _pick(N, FEATURE_TILES)
        tk = tk or (K if K <= 2048 else _pick(K, FEATURE_TILES))
        a_spec = pl.BlockSpec((tm, tk), lambda i, j, k: (i, k))
        if mode == 'nn':
            dims = ((1,), (0,))
            b_spec = pl.BlockSpec((tk, tn), lambda i, j, k: (k, j))
        else:
            dims = ((1,), (1,))
            b_spec = pl.BlockSpec((tn, tk), lambda i, j, k: (j, k))
    n_pairs, n_ex, n_out = len(pairs), len(extras), len(out_dtypes)
    nk = K // tk

    def body(*refs):
        ab = refs[:2 * n_pairs]
        ex = refs[2 * n_pairs:2 * n_pairs + n_ex]
        outs = refs[2 * n_pairs + n_ex:2 * n_pairs + n_ex + n_out]
        accs = refs[2 * n_pairs + n_ex + n_out:]
        k = pl.program_id(2)

        @pl.when(k == 0)
        def _():
            for acc in accs:
                acc[...] = jnp.zeros_like(acc)

        for p, (_, _, ai) in enumerate(pairs):
            accs[ai][...] += _dot(_bf(ab[2 * p][...]), _bf(ab[2 * p + 1][...]), dims)

        @pl.when(k == nk - 1)
        def _():
            res = epilogue([acc[...] for acc in accs], [e[...] for e in ex])
            for o, v in zip(outs, res):
                o[...] = v.astype(o.dtype)

    ex_specs = [pl.BlockSpec((tm, tn), lambda i, j, k: (i, j)) if kind == 'mn'
                else pl.BlockSpec((1, tn), lambda i, j, k: (0, j)) for _, kind in extras]
    outs = pl.pallas_call(
        body, grid=(M // tm, N // tn, nk),
        in_specs=[a_spec, b_spec] * n_pairs + ex_specs,
        out_specs=[pl.BlockSpec((tm, tn), lambda i, j, k: (i, j)) for _ in out_dtypes],
        out_shape=[jax.ShapeDtypeStruct((M, N), dt) for dt in out_dtypes],
        scratch_shapes=[pltpu.VMEM((tm, tn), F32) for _ in range(n_acc)],
        compiler_params=pltpu.CompilerParams(dimension_semantics=("arbitrary",) * 3, vmem_limit_bytes=VMEM_LIMIT),
        name=name)(*[t for a, b, _ in pairs for t in (a, b)], *[e for e, _ in extras])
    return outs


def mm_plain(name, mode, a, b, dtype=F32, scale=None):
    ep = (lambda accs, ex: [accs[0]]) if scale is None else (lambda accs, ex: [accs[0] * scale])
    return mm_call(name, mode, [(a, b, 0)], 1, ep, [dtype])[0]


def scan_fwd(name, step_fn, n_outer, n_steps, params, consts, xs, states, ys):
    n_p, n_c, n_x, n_s, n_y = len(params), len(consts), len(xs), len(states), len(ys)

    def body(*refs):
        p_refs = refs[:n_p]
        c_refs = refs[n_p:n_p + n_c]
        x_refs = refs[n_p + n_c:n_p + n_c + n_x]
        o = n_p + n_c + n_x
        y_refs = refs[o:o + n_y]
        sv_refs = refs[o + n_y:o + n_y + n_s]
        st_refs = refs[o + n_y + n_s:]
        s = pl.program_id(1)

        @pl.when(s == 0)
        def _():
            for r in st_refs:
                r[...] = jnp.zeros_like(r)

        st = tuple(r[...] for r in st_refs)
        for sv, v in zip(sv_refs, st):
            sv[...] = v
        new_st, y = step_fn(tuple(r[...] for r in p_refs), st, tuple(r[...] for r in x_refs),
                            tuple(r[...] for r in c_refs))
        for r, v in zip(y_refs, y):
            r[...] = v.astype(r.dtype)
        for r, v in zip(st_refs, new_st):
            r[...] = v

    in_specs = ([pl.BlockSpec(p[1], (lambda o, s, f=p[2]: f(o))) for p in params]
                + [pl.BlockSpec(c[1], c[2]) for c in consts]
                + [pl.BlockSpec(x[1], x[2]) for x in xs])
    out_specs = ([pl.BlockSpec(b, f) for _, _, b, f in ys]
                 + [pl.BlockSpec((None, None) + tuple(sh), (lambda o, s, n=len(sh): (o, s) + (0,) * n)) for sh, _ in states])
    out_shape = ([jax.ShapeDtypeStruct(sh, dt) for sh, dt, _, _ in ys]
                 + [jax.ShapeDtypeStruct((n_outer, n_steps) + tuple(sh), dt) for sh, dt in states])
    outs = pl.pallas_call(
        body, grid=(n_outer, n_steps), in_specs=in_specs, out_specs=out_specs, out_shape=out_shape,
        scratch_shapes=[pltpu.VMEM(tuple(sh), dt) for sh, dt in states],
        compiler_params=pltpu.CompilerParams(dimension_semantics=("arbitrary", "arbitrary"), vmem_limit_bytes=VMEM_LIMIT),
        name=name)(*[p[0] for p in params], *[c[0] for c in consts], *[x[0] for x in xs])
    return tuple(outs[:n_y]), tuple(outs[n_y:])


def scan_bwd(name, step_fn, n_outer, n_steps, params, consts, xs, states, saved, dys, glob):
    n_p, n_c, n_x, n_s, n_y = len(params), len(consts), len(xs), len(states), len(dys)
    rev = lambda f: (lambda o, s: f(o, n_steps - 1 - s))

    def body(*refs):
        p_refs = refs[:n_p]
        c_refs = refs[n_p:n_p + n_c]
        x_refs = refs[n_p + n_c:n_p + n_c + n_x]
        o = n_p + n_c + n_x
        sv_refs = refs[o:o + n_s]
        dy_refs = refs[o + n_s:o + n_s + n_y]
        o = o + n_s + n_y
        dx_refs = refs[o:o + n_x]
        dp_refs = refs[o + n_x:o + n_x + n_p]
        dst_refs = refs[o + n_x + n_p:]
        oi, s = pl.program_id(0), pl.program_id(1)

        @pl.when(s == 0)
        def _():
            for r in dst_refs:
                r[...] = jnp.zeros_like(r)

        for r, g in zip(dp_refs, glob):
            @pl.when(((s == 0) & (oi == 0)) if g else (s == 0))
            def _(r=r):
                r[...] = jnp.zeros_like(r)

        c_vals = tuple(r[...] for r in c_refs)
        f = lambda p, st, x: step_fn(p, st, x, c_vals)
        _, vjp = jax.vjp(f, tuple(r[...] for r in p_refs), tuple(r[...] for r in sv_refs), tuple(r[...] for r in x_refs))
        dp, dst, dx = vjp((tuple(r[...] for r in dst_refs), tuple(r[...] for r in dy_refs)))
        for r, v in zip(dx_refs, dx):
            r[...] = v.astype(r.dtype)
        for r, v in zip(dst_refs, dst):
            r[...] = v
        for r, v in zip(dp_refs, dp):
            r[...] += v

    gshape = lambda t: t[3] if len(t) > 3 else t[0].shape
    gidx = lambda t: t[4] if len(t) > 3 else t[2]
    in_specs = ([pl.BlockSpec(p[1], (lambda o, s, f=p[2]: f(o))) for p in params]
                + [pl.BlockSpec(c[1], rev(c[2])) for c in consts]
                + [pl.BlockSpec(x[1], rev(x[2])) for x in xs]
                + [pl.BlockSpec((None, None) + tuple(sh), (lambda o, s, n=len(sh): (o, n_steps - 1 - s) + (0,) * n)) for sh, _ in states]
                + [pl.BlockSpec(b, rev(f)) for _, b, f in dys])
    out_specs = ([pl.BlockSpec(x[1], rev(gidx(x))) for x in xs]
                 + [pl.BlockSpec(p[1], (lambda o, s, f=gidx(p): f(o))) for p in params])
    out_shape = ([jax.ShapeDtypeStruct(gshape(x), F32) for x in xs]
                 + [jax.ShapeDtypeStruct(gshape(p), F32) for p in params])
    outs = pl.pallas_call(
        body, grid=(n_outer, n_steps), in_specs=in_specs, out_specs=out_specs, out_shape=out_shape,
        scratch_shapes=[pltpu.VMEM(tuple(sh), dt) for sh, dt in states],
        compiler_params=pltpu.CompilerParams(dimension_semantics=("arbitrary", "arbitrary"), vmem_limit_bytes=VMEM_LIMIT),
        name=name)(*[p[0] for p in params], *[c[0] for c in consts], *[x[0] for x in xs], *saved,
                   *[d[0] for d in dys])
    return tuple(outs[:n_x]), tuple(outs[n_x:])


def _rows(a, rt):
    return (a, (rt, a.shape[1]), lambda o, s: (s, 0))


def _whole(a):
    return (a, a.shape, lambda o: (0,) * a.ndim)


def _rms(h, w):
    return h * lax.rsqrt(jnp.mean(h * h, axis=1, keepdims=True) + EPS) * w


def rms_fwd(name, h, w, dtype):
    TP, D = h.shape
    rt = _pick(TP, (384, 128))
    step = lambda p, st, x, c: ((), (_rms(x[0], p[0]),))
    (y,), _ = scan_fwd(name, step, 1, TP // rt, [_whole(w)], [], [_rows(h, rt)], [],
                       [((TP, D), dtype, (rt, D), lambda o, s: (s, 0))])
    return y


def rms_bwd(name, h, w, dxn, dres):
    TP, D = h.shape
    rt = _pick(TP, (384, 128))
    step = lambda p, st, x, c: ((), (_rms(x[0], p[0]), x[0]))
    (dh,), (dw,) = scan_bwd(name, step, 1, TP // rt, [_whole(w)], [], [_rows(h, rt)], [], [],
                            [_rows(dxn, rt), _rows(dres, rt)], [True])
    return dh, dw


def loss_call(h, w, tgt, mask):
    TP, D = h.shape
    rt = _pick(TP, (384, 128))

    def body(h_ref, w_ref, t_ref, m_ref, loss_ref, dh_ref, dw_ref):
        s = pl.program_id(0)

        @pl.when(s == 0)
        def _():
            loss_ref[...] = jnp.zeros_like(loss_ref)
            dw_ref[...] = jnp.zeros_like(dw_ref)

        tg, mk = t_ref[...], m_ref[...]

        def f(wv, hv):
            err = jnp.square(_rms(hv, wv) - tg) * mk
            return 0.5 * jnp.sum(jnp.sum(err, axis=1, keepdims=True), axis=0, keepdims=True) / D

        l, vjp = jax.vjp(f, w_ref[...], h_ref[...])
        dw, dh = vjp(jnp.ones((1, 1), F32))
        loss_ref[...] += l
        dw_ref[...] += dw
        dh_ref[...] = dh

    row = lambda wd: pl.BlockSpec((rt, wd), lambda s: (s, 0))
    const = lambda shape: pl.BlockSpec(shape, lambda s: (0, 0))
    return pl.pallas_call(
        body, grid=(TP // rt,), in_specs=[row(D), const((1, D)), row(D), row(1)],
        out_specs=[const((1, 1)), row(D), const((1, D))],
        out_shape=[jax.ShapeDtypeStruct((1, 1), F32), jax.ShapeDtypeStruct((TP, D), F32), jax.ShapeDtypeStruct((1, D), F32)],
        compiler_params=pltpu.CompilerParams(dimension_semantics=("arbitrary",), vmem_limit_bytes=VMEM_LIMIT),
        name="loss_head")(h, w, tgt, mask)


def colsum(name, a):
    TP, N = a.shape
    rt = _pick(TP, (384, 128))

    def body(a_ref, o_ref):
        @pl.when(pl.program_id(0) == 0)
        def _():
            o_ref[...] = jnp.zeros_like(o_ref)
        o_ref[...] += jnp.sum(a_ref[...], axis=0, keepdims=True)

    return pl.pallas_call(
        body, grid=(TP // rt,), in_specs=[pl.BlockSpec((rt, N), lambda s: (s, 0))],
        out_specs=pl.BlockSpec((1, N), lambda s: (0, 0)), out_shape=jax.ShapeDtypeStruct((1, N), F32),
        compiler_params=pltpu.CompilerParams(dimension_semantics=("arbitrary",)), name=name)(a)


def ffn_fwd(tag, h, nw, wg_t, wu_t, wd):
    xn = rms_fwd(tag + "_norm", h, nw, BF16)

    def ep_up(accs, ex):
        g, u = accs
        return [g, u, jax.nn.silu(g) * u]

    g, u, a = mm_call(tag + "_up", 'nt', [(xn, wg_t, 0), (xn, wu_t, 1)], 2, ep_up, [BF16, BF16, BF16])
    h2 = mm_call(tag + "_down", 'nn', [(a, wd, 0)], 1, lambda accs, ex: [ex[0] + 0.5 * accs[0]], [F32],
                 extras=[(h, 'mn')])[0]
    return h2, (h, xn, g, u, a)


def ffn_bwd(tag, dh2, res, nw, wg_t, wu_t, wd):
    h, xn, g, u, a = res

    def ep_act(accs, ex):
        da = 0.5 * accs[0]
        gv, uv = ex[0].astype(F32), ex[1].astype(F32)
        sg = jax.nn.sigmoid(gv)
        return [da * uv * (sg * (1.0 + gv * (1.0 - sg))), da * (gv * sg)]

    dg, du = mm_call(tag + "_dact", 'nt', [(dh2, wd, 0)], 1, ep_act, [BF16, BF16], extras=[(g, 'mn'), (u, 'mn')])
    dwd = mm_plain(tag + "_dwd", 'tn', a, dh2, BF16, scale=0.5)
    dxn = mm_call(tag + "_dxn", 'nn', [(dg, wg_t, 0), (du, wu_t, 0)], 1, lambda accs, ex: [accs[0]], [F32])[0]
    dwg_t = mm_plain(tag + "_dwg", 'tn', dg, xn, BF16)
    dwu_t = mm_plain(tag + "_dwu", 'tn', du, xn, BF16)
    dh, dnw = rms_bwd(tag + "_dnorm", h, nw, dxn, dh2)
    return dh, dnw, dwg_t, dwu_t, dwd


def _mlstm_step(params, state, xs, consts, *, dk):
    bif, nw = params
    c_st, n_st, m_st = state
    q, k, v, og, gr = xs
    C = MLSTM_CHUNK
    R = q.shape[0]
    ri, ci = _iota((C, C), 0), _iota((C, C), 1)
    eye = (ri == ci).astype(F32)
    causal = ci <= ri
    upper = (ri <= ci).astype(F32)
    outs = []
    for j in range(R // C):
        sl = slice(j * C, (j + 1) * C)
        qj, kj, vj = q[sl] * (dk ** -0.5), k[sl], v[sl]
        li = gr[0:1, sl] + bif[0:1, 0:1]
        lf = jax.nn.log_sigmoid(gr[1:2, sl] + bif[1:2, 0:1])
        b_row = _mm32(lf, upper)
        b_col = _row2col(b_row, eye)
        log_w = jnp.where(causal, b_col - b_row + li, -jnp.inf)
        log_init = b_col + m_st
        m_t = lax.stop_gradient(jnp.maximum(log_init, jnp.max(log_w, axis=1, keepdims=True)))
        w = jnp.exp(log_w - m_t)
        w_init = jnp.exp(log_init - m_t)
        qk = _mm_nt(qj, kj) * w
        num = w_init * _mm(qj, c_st) + _mm(qk, vj)
        den = w_init * jnp.sum(qj * n_st, axis=1, keepdims=True) + jnp.sum(qk, axis=1, keepdims=True)
        h = num / jnp.maximum(jnp.abs(den), jnp.exp(-m_t))
        b_last = b_row[:, C - 1:C]
        log_end_init = b_last + m_st
        log_end = b_last - b_row + li
        m_new = lax.stop_gradient(jnp.maximum(log_end_init, jnp.max(log_end, axis=1, keepdims=True)))
        a_init = jnp.exp(log_end_init - m_new)
        ka = kj * _row2col(jnp.exp(log_end - m_new), eye)
        c_st = a_init * c_st + _mm_tn(ka, vj)
        n_st = a_init * n_st + jnp.sum(ka, axis=0, keepdims=True)
        m_st = m_new
        hn = h * lax.rsqrt(jnp.mean(h * h, axis=1, keepdims=True) + EPS)
        outs.append(hn * nw * jax.nn.sigmoid(og[sl]))
    return (c_st, n_st, m_st), (jnp.concatenate(outs, axis=0),)


def _mlstm_ops(p, gr, bif, nw):
    H = MLSTM_HEADS
    TP = p.shape[0]
    dv = nw.shape[1] // H
    dk = dv // 2
    R = ROW_TILE
    step = functools.partial(_mlstm_step, dk=dk)
    params = [(bif, (None, 2, LANES), lambda o: (o, 0, 0)), (nw, (1, dv), lambda o: (0, o))]
    col = lambda w, off: (p, (R, w), (lambda o, s: (s, off + o)), (TP, H * w), (lambda o, s: (s, o)))
    xs = [col(dk, 0), col(dk, H), col(dv, H), col(dv, 2 * H), (gr, (None, 2, R), lambda o, s: (o, 0, s))]
    states = [((dk, dv), F32), ((1, dk), F32), ((1, 1), F32)]
    ys = [((TP, H * dv), F32, (R, dv), lambda o, s: (s, o))]
    return step, H, TP // R, params, xs, states, ys


def mlstm_fwd(h, nw1, w_in_t, w_out, b_if, norm_w):
    H = MLSTM_HEADS
    D = h.shape[1]
    u = rms_fwd("mlstm_norm", h, nw1, BF16)
    p = mm_plain("mlstm_in", 'nt', u, w_in_t)
    gr = p[:, 3 * D:3 * D + 2 * H].T.reshape(2, H, -1).transpose(1, 0, 2)
    bif = jnp.broadcast_to(b_if.reshape(2, H).T[:, :, None], (H, 2, LANES))
    step, _, n, params, xs, states, ys = _mlstm_ops(p, gr, bif, norm_w)
    (act,), saved = scan_fwd("mlstm_core", step, H, n, params, [], xs, states, ys)
    h2 = mm_call("mlstm_out", 'nn', [(act, w_out, 0)], 1, lambda accs, ex: [ex[0] + accs[0]], [F32], extras=[(h, 'mn')])[0]
    return h2, (h, u, p, gr, bif, saved, act)


def mlstm_bwd(dh2, res, nw1, w_in_t, w_out, norm_w):
    h, u, p, gr, bif, saved, act = res
    H = MLSTM_HEADS
    TP = h.shape[0]
    dact = mm_plain("mlstm_dact", 'nt', dh2, w_out)
    dw_out = mm_plain("mlstm_dwout", 'tn', act, dh2, BF16)
    step, _, n, params, xs, states, ys = _mlstm_ops(p, gr, bif, norm_w)
    (dq, dk, dv, dog, dgr), (dbif, dnorm) = scan_bwd("mlstm_core_bwd", step, H, n, params, [], xs, states, saved,
                                                     [(dact, ys[0][2], ys[0][3])], [False, False])
    dgates = dgr.transpose(1, 0, 2).reshape(2 * H, TP).T
    pad = p.shape[1] - (dq.shape[1] + dk.shape[1] + dv.shape[1] + dog.shape[1] + 2 * H)
    dp = jnp.concatenate([dq, dk, dv, dog, dgates, jnp.zeros((TP, pad), F32)], axis=1)
    du = mm_plain("mlstm_du", 'nn', dp, w_in_t)
    dw_in_t = mm_plain("mlstm_dwin", 'tn', dp, u, BF16)
    dh, dnw1 = rms_bwd("mlstm_dnorm", h, nw1, du, dh2)
    db_if = dbif[:, :, 0].T.reshape(1, 2 * H)
    return dh, dict(nw=dnw1, w_in_t=dw_in_t, w_out=dw_out, b_if=db_if, norm_w=dnorm)


def _pool_step(params, state, xs, consts):
    w, scale = params
    (prev,) = state
    u, h = xs
    pos, win = consts
    R = u.shape[0]
    wn = win[0:1, 0:1]
    ext = jnp.concatenate([prev, u], axis=0)
    lag = _iota((R, 2 * R), 0) + R - _iota((R, 2 * R), 1)
    band = ((lag >= 0) & (lag < wn)).astype(F32)
    wsum = _mm32(band, ext)
    cnt = jnp.minimum(pos + 1, wn).astype(F32)
    pooled = wsum / cnt - u
    return (u,), (h + _mm(pooled, w) * scale,)


def _pool_ops(u, h, w, scale, pos):
    TP, D = u.shape
    G = D // N_POOL
    R = ROW_TILE
    win = jnp.broadcast_to(jnp.array([2 << g for g in range(N_POOL)], jnp.int32)[:, None, None], (N_POOL, 1, LANES))
    params = [(w, (None, G, G), lambda o: (o, 0, 0)), (scale, (1, G), lambda o: (0, o))]
    consts = [(pos, (R, 1), lambda o, s: (s, 0)), (win, (None, 1, LANES), lambda o, s: (o, 0, 0))]
    grp = lambda a: (a, (R, G), lambda o, s: (s, o))
    return N_POOL, TP // R, params, consts, [grp(u), grp(h)], [((R, G), F32)], [((TP, D), F32, (R, G), lambda o, s: (s, o))]


def pool_fwd(h, nw1, w, scale, pos):
    u = rms_fwd("pool_norm", h, nw1, F32)
    no, n, params, consts, xs, states, ys = _pool_ops(u, h, w, scale, pos)
    (h2,), saved = scan_fwd("pool_core", _pool_step, no, n, params, consts, xs, states, ys)
    return h2, (h, u, saved)


def pool_bwd(dh2, res, nw1, w, scale, pos):
    h, u, saved = res
    no, n, params, consts, xs, states, ys = _pool_ops(u, h, w, scale, pos)
    (du, dres), (dw, dscale) = scan_bwd("pool_core_bwd", _pool_step, no, n, params, consts, xs, states, saved,
                                        [(dh2, ys[0][2], ys[0][3])], [False, False])
    dh, dnw1 = rms_bwd("pool_dnorm", h, nw1, du, dres)
    return dh, dict(nw=dnw1, w=dw, scale=dscale)


def _unit_lower_inverse(low):
    C = low.shape[0]
    ri, ci = _iota((C, C), 0), _iota((C, C), 1)
    inv = (ri == ci).astype(F32)
    b = 1
    while b < C:
        blk = 2 * b
        sh = blk.bit_length() - 1
        off = jnp.where(((ri >> sh) == (ci >> sh)) & ((ri & (blk - 1)) >= b) & ((ci & (blk - 1)) < b), low, 0.0)
        inv = inv - (off if b == 1 else _mm3(_mm3(inv, off), inv))
        b = blk
    return inv


def _conv_silu(prev, x, w):
    R = x.shape[0]
    ext = jnp.concatenate([prev, x], axis=0)
    y = sum(w[j:j + 1, :] * ext[8 - (GDN_CONV - 1) + j:8 - (GDN_CONV - 1) + j + R] for j in range(GDN_CONV))
    return jax.nn.silu(y)


def _gdn_step(params, state, xs, consts):
    cwq, cwk, cwv, ad, gnw = params
    s_st, pq, pk, pv = state
    q, k, v, z, gb = xs
    C = GDN_CHUNK
    R, dk = q.shape
    dv = v.shape[1] // 2
    ri, ci = _iota((C, C), 0), _iota((C, C), 1)
    eye = (ri == ci).astype(F32)
    causal = ci <= ri
    strict = ci < ri
    upper = (ri <= ci).astype(F32)
    qc, kc, vc = _conv_silu(pq, q, cwq), _conv_silu(pk, k, cwk), _conv_silu(pv, v, cwv)
    qn = qc * lax.rsqrt(jnp.sum(qc * qc, axis=1, keepdims=True) + EPS) * (dk ** -0.5)
    kn = kc * lax.rsqrt(jnp.sum(kc * kc, axis=1, keepdims=True) + EPS)
    new_s, outs = [], []
    for e in range(2):
        s_e = s_st[e]
        beta_row = jax.nn.sigmoid(gb[e:e + 1, :])
        g_row = -jnp.exp(ad[e:e + 1, 0:1]) * jax.nn.softplus(gb[2 + e:3 + e, :] + ad[2 + e:3 + e, 0:1])
        ve, ze = vc[:, e * dv:(e + 1) * dv], z[:, e * dv:(e + 1) * dv]
        o_e = []
        for j in range(R // C):
            sl = slice(j * C, (j + 1) * C)
            qj, kj = qn[sl], kn[sl]
            gc_row = _mm32(g_row[:, sl], upper)
            gc_col = _row2col(gc_row, eye)
            decay = jnp.exp(jnp.where(causal, gc_col - gc_row, -jnp.inf))
            beta_col = _row2col(beta_row[:, sl], eye)
            kb = kj * beta_col
            low = jnp.where(strict, _mm_nt(kb, kj) * decay, 0.0)
            rhs = jnp.concatenate([ve[sl] * beta_col, kb * jnp.exp(gc_col)], axis=1)
            sol = _mm3(_unit_lower_inverse(low), rhs)
            v_new = sol[:, :dv] - _mm(sol[:, dv:], s_e)
            attn = _mm_nt(qj, kj) * decay
            o = _mm(qj * jnp.exp(gc_col), s_e) + _mm(attn, v_new)
            g_last = gc_row[:, C - 1:C]
            s_e = jnp.exp(g_last) * s_e + _mm_tn(kj * jnp.exp(g_last - gc_col), v_new)
            on = o * lax.rsqrt(jnp.mean(o * o, axis=1, keepdims=True) + EPS) * gnw
            o_e.append(on * jax.nn.silu(ze[sl]))
        new_s.append(s_e)
        outs.append(jnp.concatenate(o_e, axis=0))
    return (jnp.stack(new_s), q[R - 8:], k[R - 8:], v[R - 8:]), (jnp.concatenate(outs, axis=1),)


def _gdn_ops(p, gb, conv_w, ad, gnw):
    TP = p.shape[0]
    dk = GDN_DK
    nqk = gb.shape[0]
    R = ROW_TILE
    cw = lambda w, off: (conv_w, (GDN_CONV, w), (lambda o: (0, off + o)), (GDN_CONV, nqk * w), (lambda o: (0, o)))
    params = [cw(dk, 0), cw(dk, nqk), cw(2 * dk, nqk), (ad, (None, 4, LANES), lambda o: (o, 0, 0)), _whole(gnw)]
    col = lambda w, off: (p, (R, w), (lambda o, s: (s, off + o)), (TP, nqk * w), (lambda o, s: (s, o)))
    xs = [col(dk, 0), col(dk, nqk), col(2 * dk, nqk), col(2 * dk, 2 * nqk), (gb, (None, 4, R), lambda o, s: (o, 0, s))]
    states = [((2, dk, dk), F32), ((8, dk), F32), ((8, dk), F32), ((8, 2 * dk), F32)]
    ys = [((TP, 2 * nqk * dk), F32, (R, 2 * dk), lambda o, s: (s, o))]
    return nqk, TP // R, params, xs, states, ys


def gdn_fwd(h, nw1, w_in_t, w_out, conv_w, a_log, dt_bias, gnw):
    D = h.shape[1]
    nqk = D // GDN_DK
    u = rms_fwd("gdn_norm", h, nw1, BF16)
    p = mm_plain("gdn_in", 'nt', u, w_in_t)
    gb = p[:, 6 * D:6 * D + 4 * nqk].T.reshape(2, nqk, 2, -1).transpose(1, 0, 2, 3).reshape(nqk, 4, -1)
    ad = jnp.concatenate([a_log.reshape(nqk, 2), dt_bias.reshape(nqk, 2)], axis=1)
    ad = jnp.broadcast_to(ad[:, :, None], (nqk, 4, LANES))
    no, n, params, xs, states, ys = _gdn_ops(p, gb, conv_w, ad, gnw)
    (act,), saved = scan_fwd("gdn_core", _gdn_step, no, n, params, [], xs, states, ys)
    h2 = mm_call("gdn_out", 'nn', [(act, w_out, 0)], 1, lambda accs, ex: [ex[0] + accs[0]], [F32], extras=[(h, 'mn')])[0]
    return h2, (h, u, p, gb, ad, saved, act)


def gdn_bwd(dh2, res, nw1, w_in_t, w_out, conv_w, gnw):
    h, u, p, gb, ad, saved, act = res
    TP, D = h.shape
    nqk = D // GDN_DK
    dact = mm_plain("gdn_dact", 'nt', dh2, w_out)
    dw_out = mm_plain("gdn_dwout", 'tn', act, dh2, BF16)
    no, n, params, xs, states, ys = _gdn_ops(p, gb, conv_w, ad, gnw)
    (dq, dk, dv, dz, dgb), (dcq, dck, dcv, dad, dgnw) = scan_bwd(
        "gdn_core_bwd", _gdn_step, no, n, params, [], xs, states, saved, [(dact, ys[0][2], ys[0][3])],
        [False, False, False, False, True])
    dgates = dgb.reshape(nqk, 2, 2, TP).transpose(1, 0, 2, 3).reshape(4 * nqk, TP).T
    pad = p.shape[1] - (6 * D + 4 * nqk)
    dp = jnp.concatenate([dq, dk, dv, dz, dgates, jnp.zeros((TP, pad), F32)], axis=1)
    du = mm_plain("gdn_du", 'nn', dp, w_in_t)
    dw_in_t = mm_plain("gdn_dwin", 'tn', dp, u, BF16)
    dh, dnw1 = rms_bwd("gdn_dnorm", h, nw1, du, dh2)
    dad = dad[:, :, 0]
    return dh, dict(nw=dnw1, w_in_t=dw_in_t, w_out=dw_out, conv_w=jnp.concatenate([dcq, dck, dcv], axis=1),
                    a_log=dad[:, :2].reshape(1, 2 * nqk), dt_bias=dad[:, 2:].reshape(1, 2 * nqk), norm_w=dgnw)


def _rope(x, cos, sin):
    W = x.shape[1]
    first_half = (_iota(x.shape, 1) & (SWA_DH - 1)) < SWA_DH // 2
    rot = jnp.where(first_half, -_lane_roll(W - SWA_DH // 2)(x), _lane_roll(SWA_DH // 2)(x))
    return x * jnp.tile(cos, (1, W // LANES)) + rot * jnp.tile(sin, (1, W // LANES))


def _swa_step(params, state, xs, consts):
    (sinks,) = params
    kprev, vprev = state
    q, k, v = xs
    cos, sin, pos = consts
    R = q.shape[0]
    hkv = k.shape[1] // SWA_DH
    G = SWA_GROUP
    qr, kr = _rope(q, cos, sin), _rope(k, cos, sin)
    k2, v2 = jnp.concatenate([kprev, kr], axis=0), jnp.concatenate([vprev, v], axis=0)
    lane = _iota((R, LANES), 1)
    qpos = jnp.concatenate([pos] * G, axis=0)
    kpos = pos[0:1, 0:1] - R + _iota((1, 2 * R), 1)
    mask = (kpos <= qpos) & (qpos - kpos < SWA_WINDOW) & (kpos >= 0)
    sel_r, sel_c = _iota((hkv * SWA_DH, LANES), 0), _iota((hkv * SWA_DH, LANES), 1)
    out = []
    for hh in range(hkv):
        sel = _bf((sel_r == hh * SWA_DH + (sel_c & (SWA_DH - 1))).astype(F32))
        kd, vd = _mm(k2, sel), _mm(v2, sel)
        q8 = []
        for i in range(G // 2):
            qb = qr[:, (hh * G // 2 + i) * LANES:(hh * G // 2 + i + 1) * LANES]
            q8 += [jnp.where(lane < SWA_DH, qb, 0.0), jnp.where(lane >= SWA_DH, qb, 0.0)]
        s = _mm_nt(jnp.concatenate(q8, axis=0), kd) * (SWA_DH ** -0.5)
        s = jnp.where(mask, s, -jnp.inf)
        sink = jnp.concatenate([jnp.broadcast_to(sinks[0:1, hh * G + g:hh * G + g + 1], (R, 1)) for g in range(G)], axis=0)
        m = lax.stop_gradient(jnp.maximum(jnp.max(s, axis=1, keepdims=True), sink))
        e = jnp.exp(s - m)
        prob = e / (jnp.sum(e, axis=1, keepdims=True) + jnp.exp(sink - m))
        o8 = _mm(prob, vd)
        for i in range(G // 2):
            out.append(jnp.where(lane < SWA_DH, o8[2 * i * R:(2 * i + 1) * R], o8[(2 * i + 1) * R:(2 * i + 2) * R]))
    return (kr, v), (jnp.concatenate(out, axis=1),)


def _swa_ops(p, sinks, cos, sin, pos):
    TP = p.shape[0]
    R = ROW_TILE
    hq = sinks.shape[1]
    wq, wkv = hq * SWA_DH, hq // SWA_GROUP * SWA_DH
    nb = wq // wkv
    xs = [(p, (R, wq), (lambda o, s: (s, 0)), (TP, wq), (lambda o, s: (s, 0))),
          (p, (R, wkv), (lambda o, s: (s, nb)), (TP, wkv), (lambda o, s: (s, 0))),
          (p, (R, wkv), (lambda o, s: (s, nb + 1)), (TP, wkv), (lambda o, s: (s, 0)))]
    consts = [_rows(cos, R), _rows(sin, R), _rows(pos, R)]
    states = [((R, wkv), F32), ((R, wkv), F32)]
    ys = [((TP, wq), F32, (R, wq), lambda o, s: (s, 0))]
    return TP // R, [_whole(sinks)], consts, xs, states, ys


def swa_fwd(h, nw1, w_qkv_t, b_qkv, w_out, b_out, sinks, cos, sin, pos):
    u = rms_fwd("swa_norm", h, nw1, BF16)
    p = mm_call("swa_in", 'nt', [(u, w_qkv_t, 0)], 1, lambda accs, ex: [accs[0] + ex[0]], [F32], extras=[(b_qkv, 'n')])[0]
    n, params, consts, xs, states, ys = _swa_ops(p, sinks, cos, sin, pos)
    (act,), saved = scan_fwd("swa_core", _swa_step, 1, n, params, consts, xs, states, ys)
    h2 = mm_call("swa_out", 'nn', [(act, w_out, 0)], 1, lambda accs, ex: [ex[0] + accs[0] + ex[1]], [F32],
                 extras=[(h, 'mn'), (b_out, 'n')])[0]
    return h2, (h, u, p, saved, act)


def swa_bwd(dh2, res, nw1, w_qkv_t, w_out, sinks, cos, sin, pos):
    h, u, p, saved, act = res
    dact = mm_plain("swa_dact", 'nt', dh2, w_out)
    dw_out = mm_plain("swa_dwout", 'tn', act, dh2, BF16)
    db_out = colsum("swa_dbout", dh2)
    n, params, consts, xs, states, ys = _swa_ops(p, sinks, cos, sin, pos)
    (dq, dk, dv), (dsinks,) = scan_bwd("swa_core_bwd", _swa_step, 1, n, params, consts, xs, states, saved,
                                       [(dact, ys[0][2], ys[0][3])], [True])
    dp = jnp.concatenate([dq, dk, dv], axis=1)
    db_qkv = colsum("swa_dbqkv", dp)
    du = mm_plain("swa_du", 'nn', dp, w_qkv_t)
    dw_qkv_t = mm_plain("swa_dwqkv", 'tn', dp, u, BF16)
    dh, dnw1 = rms_bwd("swa_dnorm", h, nw1, du, dh2)
    return dh, dict(nw=dnw1, w_qkv_t=dw_qkv_t, w_out=dw_out, b_qkv=db_qkv, b_out=db_out, sinks=dsinks)


def _dev_index(dev):
    return 4 * dev[0] + 2 * dev[1] + dev[2]


def all_gather(name, shards):
    n = len(shards)

    def body(*refs):
        x_refs, out_refs = refs[:n], refs[n:2 * n]
        send_sems, recv_sems, local_sem = refs[2 * n:]
        x, y, c = lax.axis_index("x"), lax.axis_index("y"), lax.axis_index("c")
        me, sibling = (x, y, c), (x, y, 1 - c)
        chips = [(1 - x, y), (x, 1 - y), (1 - x, 1 - y)]

        def copy(a, k, block, to, src=None):
            dst = out_refs[a].at[_dev_index(block)]
            return pltpu.make_async_remote_copy(src_ref=dst if src is None else src, dst_ref=dst,
                                                send_sem=send_sems.at[a, k], recv_sem=recv_sems.at[a, k],
                                                device_id=to, device_id_type=MESH_ID)

        mine = [pltpu.make_async_copy(x_refs[a], out_refs[a].at[_dev_index(me)], local_sem.at[a]) for a in range(n)]
        first, passed = [], []
        for a in range(n):
            mine[a].start()
            first += [copy(a, 0, me, sibling, src=x_refs[a])]
            first += [copy(a, 1 + j, me, (*chip, c), src=x_refs[a]) for j, chip in enumerate(chips)]
        for cp in first:
            cp.start()
        for a in range(n):
            for j, chip in enumerate(chips):
                copy(a, 1 + j, (*chip, c), me).wait_recv()
                fwd = copy(a, 4 + j, (*chip, c), sibling)
                fwd.start()
                passed.append(fwd)
        for a in range(n):
            copy(a, 0, sibling, me).wait_recv()
            for j, chip in enumerate(chips):
                copy(a, 4 + j, (*chip, 1 - c), me).wait_recv()
        for cp in first + passed:
            cp.wait_send()
        for cp in mine:
            cp.wait()

    any_spec = pl.BlockSpec(memory_space=pl.ANY)
    return pl.pallas_call(
        body, in_specs=[any_spec] * n, out_specs=[any_spec] * n,
        out_shape=[jax.ShapeDtypeStruct((N_DEV,) + s.shape, s.dtype) for s in shards],
        scratch_shapes=[pltpu.SemaphoreType.DMA((n, 7)), pltpu.SemaphoreType.DMA((n, 7)), pltpu.SemaphoreType.DMA((n,))],
        name=name)(*shards)


def exchange_blocks(name, fulls):
    n = len(fulls)

    def body(*refs):
        g_refs, out_refs = refs[:n], refs[n:2 * n]
        send_sems, recv_sems, local_sem = refs[2 * n:]
        x, y, c = lax.axis_index("x"), lax.axis_index("y"), lax.axis_index("c")
        me = (x, y, c)
        peers = [(1 - x if r & 4 else x, 1 - y if r & 2 else y, 1 - c if r & 1 else c) for r in range(1, N_DEV)]

        def copy(a, k, peer):
            return pltpu.make_async_remote_copy(src_ref=g_refs[a].at[_dev_index(peer)], dst_ref=out_refs[a].at[_dev_index(me)],
                                                send_sem=send_sems.at[a, k], recv_sem=recv_sems.at[a, k],
                                                device_id=peer, device_id_type=MESH_ID)

        mine = [pltpu.make_async_copy(g_refs[a].at[_dev_index(me)], out_refs[a].at[_dev_index(me)], local_sem.at[a])
                for a in range(n)]
        sends = [copy(a, k, peer) for a in range(n) for k, peer in enumerate(peers)]
        for cp in mine + sends:
            cp.start()
        for a in range(n):
            for k, peer in enumerate(peers):
                pltpu.make_async_remote_copy(src_ref=g_refs[a].at[_dev_index(peer)], dst_ref=out_refs[a].at[_dev_index(peer)],
                                             send_sem=send_sems.at[a, k], recv_sem=recv_sems.at[a, k],
                                             device_id=peer, device_id_type=MESH_ID).wait_recv()
        for cp in sends:
            cp.wait_send()
        for cp in mine:
            cp.wait()

    any_spec = pl.BlockSpec(memory_space=pl.ANY)
    return pl.pallas_call(
        body, in_specs=[any_spec] * n, out_specs=[any_spec] * n,
        out_shape=[jax.ShapeDtypeStruct(g.shape, g.dtype) for g in fulls],
        scratch_shapes=[pltpu.SemaphoreType.DMA((n, 7)), pltpu.SemaphoreType.DMA((n, 7)), pltpu.SemaphoreType.DMA((n,))],
        name=name)(*fulls)


def sum_blocks(name, parts):
    _, r, c = parts.shape
    tr = _pick(r, ROW_TILES)

    def body(p_ref, o_ref):
        acc = p_ref[0].astype(F32)
        for b in range(1, N_DEV):
            acc = acc + p_ref[b].astype(F32)
        o_ref[...] = acc

    return pl.pallas_call(
        body, grid=(r // tr,), in_specs=[pl.BlockSpec((N_DEV, tr, c), lambda i: (0, i, 0))],
        out_specs=pl.BlockSpec((tr, c), lambda i: (i, 0)), out_shape=jax.ShapeDtypeStruct((r, c), F32),
        compiler_params=pltpu.CompilerParams(dimension_semantics=("arbitrary",), vmem_limit_bytes=VMEM_LIMIT),
        name=name)(parts)


def adamw(name, w, g, m, v):
    r, c = w.shape
    tr = _pick(r, [t for t in ROW_TILES if t * c * 4 * 7 * 2 <= VMEM_LIMIT // 2])

    def body(w_ref, g_ref, m_ref, v_ref, d_ref, nm_ref, nv_ref):
        gv = g_ref[...]
        nm = ADAM_B1 * m_ref[...] + (1.0 - ADAM_B1) * gv
        nv = ADAM_B2 * v_ref[...] + (1.0 - ADAM_B2) * jnp.square(gv)
        m_hat = nm / (1.0 - ADAM_B1 ** ADAM_STEP)
        v_hat = nv / (1.0 - ADAM_B2 ** ADAM_STEP)
        d_ref[...] = -ADAM_LR * (m_hat / (jnp.sqrt(v_hat) + ADAM_EPS) + ADAM_WD * w_ref[...])
        nm_ref[...] = nm
        nv_ref[...] = nv

    spec = pl.BlockSpec((tr, c), lambda i: (i, 0))
    return pl.pallas_call(
        body, grid=(r // tr,), in_specs=[spec] * 4, out_specs=[spec] * 3,
        out_shape=[jax.ShapeDtypeStruct((r, c), F32)] * 3,
        compiler_params=pltpu.CompilerParams(dimension_semantics=("arbitrary",), vmem_limit_bytes=VMEM_LIMIT),
        name=name)(w, g, m, v)


def _pack(arrays):
    flat = jnp.concatenate([a.reshape(-1) for a in arrays])
    n = _round_up(flat.shape[0], 8 * LANES)
    return jnp.pad(flat, (0, n - flat.shape[0])).reshape(-1, LANES)


def _unpack(packed, shapes):
    flat = packed.reshape(-1)
    out, o = [], 0
    for sh in shapes:
        sz = 1
        for d in sh:
            sz *= d
        out.append(flat[o:o + sz].reshape(sh))
        o += sz
    return out


def _gather_axis(g, ax):
    g = jnp.moveaxis(g, 0, ax)
    return g.reshape(g.shape[:ax] + (g.shape[ax] * g.shape[ax + 1],) + g.shape[ax + 2:])


def _comm_rows(a):
    r = a.shape[0]
    rp = r if r % 16 == 0 else _round_up(r, ROW_PAD)
    return jnp.pad(_bf(a), ((0, rp - r), (0, 0)))


def _natural(g, r, pad_to=None):
    full = g[:, :r].reshape(N_DEV * r, g.shape[2])
    if pad_to is not None and full.shape[0] % pad_to:
        full = jnp.pad(full, ((0, _round_up(full.shape[0], pad_to) - full.shape[0]), (0, 0)))
    return full


def _blocked(full, r):
    blocks = full[:N_DEV * r].reshape(N_DEV, r, full.shape[1])
    rp = r if r % 16 == 0 else _round_up(r, ROW_PAD)
    return jnp.pad(_bf(blocks), ((0, 0), (0, rp - r), (0, 0)))


def kernel(x, *rest):
    nw_ = len(WEIGHTS)
    W = dict(zip(WEIGHTS, rest[:nw_]))
    loss_target = rest[nw_]
    M = dict(zip(WEIGHTS, rest[nw_ + 1:2 * nw_ + 1]))
    V = dict(zip(WEIGHTS, rest[2 * nw_ + 1:3 * nw_ + 1]))

    T0, D = x.shape[1], x.shape[2]
    T = T0 + N_META
    TP = _round_up(T, ROW_TILE)
    me = 4 * lax.axis_index("x") + 2 * lax.axis_index("y") + lax.axis_index("c")

    small_sharded = [k for k in WEIGHTS if k in SMALL and SMALL[k] is not None]
    (sg,) = all_gather("ag_small", [_pack([W[k] for k in small_sharded])])
    per_dev = [_unpack(sg[b], [W[k].shape for k in small_sharded]) for b in range(N_DEV)]
    full = {k: _gather_axis(jnp.stack([per_dev[b][i] for b in range(N_DEV)]), SMALL[k]) for i, k in enumerate(small_sharded)}
    for k in SMALL:
        if SMALL[k] is None:
            full[k] = W[k]

    ffn = {}
    for i in range(DEPTH):
        for s in range(2):
            r = W['ffn_w_gate'].shape[3]
            g, u, d = all_gather(f"ag_ffn_{i}_{s}", [_comm_rows(W['ffn_w_gate'][i, s].T), _comm_rows(W['ffn_w_up'][i, s].T),
                                                    _comm_rows(W['ffn_w_down'][i, s])])
            ffn[i, s] = (_natural(g, r), _natural(u, r), _natural(d, r))
    r_mi, r_mo = W['mlstm_w_in'].shape[2], W['mlstm_w_out'].shape[1]
    g_in, g_out = all_gather("ag_mlstm", [_comm_rows(W['mlstm_w_in'][0].T), _comm_rows(W['mlstm_w_out'][0])])
    mlstm_w = (_natural(g_in, r_mi, IN_PAD), _natural(g_out, r_mo))
    pw = W['pool_w'][0]
    (g_pool,) = all_gather("ag_pool", [_bf(pw.reshape(-1, pw.shape[2]))])
    pool_w = g_pool.reshape(N_DEV, N_POOL, pw.shape[1], pw.shape[2]).transpose(1, 0, 2, 3).reshape(N_POOL, pw.shape[2], pw.shape[2]).astype(F32)
    r_gi, r_go = W['gdn_w_in'].shape[2], W['gdn_w_out'].shape[1]
    g_in, g_out = all_gather("ag_gdn", [_comm_rows(W['gdn_w_in'][0].T), _comm_rows(W['gdn_w_out'][0])])
    gdn_w = (_natural(g_in, r_gi, IN_PAD), _natural(g_out, r_go))
    r_si, r_so = W['swa_w_qkv'].shape[2], W['swa_w_out'].shape[1]
    g_in, g_out = all_gather("ag_swa", [_comm_rows(W['swa_w_qkv'][0].T), _comm_rows(W['swa_w_out'][0])])
    swa_w = (_natural(g_in, r_si), _natural(g_out, r_so))

    pos = jnp.arange(TP, dtype=jnp.int32)[:, None]
    inv = ROPE_THETA ** (-jnp.arange(0, SWA_DH, 2, dtype=F32) / SWA_DH)
    ang = jnp.arange(TP, dtype=F32)[:, None] * inv[None, :]
    ang = jnp.concatenate([ang, ang, ang, ang], axis=1)
    cos, sin = jnp.cos(ang), jnp.sin(ang)
    row_mask = ((pos >= N_META) & (pos < T)).astype(F32)
    tgt = jnp.pad(loss_target[0], ((N_META, TP - T), (0, 0)))
    nrm = lambda i, j: full['norm_w'][i, j][None, :]

    h = jnp.concatenate([full['meta_tokens'], x[0], jnp.zeros((TP - T, D), F32)], axis=0)
    res = []
    for i in range(DEPTH):
        h, r1 = ffn_fwd(f"ffn_{i}_0", h, nrm(i, 0), *ffn[i, 0])
        if i % 4 == 0:
            h, rm = mlstm_fwd(h, nrm(i, 1), mlstm_w[0], mlstm_w[1], full['mlstm_b_if'], full['mlstm_norm_w'])
        elif i % 4 == 1:
            h, rm = pool_fwd(h, nrm(i, 1), pool_w, full['pool_scale'], pos)
        elif i % 4 == 2:
            h, rm = gdn_fwd(h, nrm(i, 1), gdn_w[0], gdn_w[1], full['gdn_conv_w'][0], full['gdn_a_log'],
                            full['gdn_dt_bias'], full['gdn_norm_w'])
        else:
            h, rm = swa_fwd(h, nrm(i, 1), swa_w[0], full['swa_b_qkv'], swa_w[1], full['swa_b_out'], full['swa_sinks'],
                            cos, sin, pos)
        h, r2 = ffn_fwd(f"ffn_{i}_1", h, nrm(i, 2), *ffn[i, 1])
        res.append((r1, rm, r2))

    loss_local, dh, d_final = loss_call(h, full['final_norm_w'][None, :], tgt, row_mask)
    loss = lax.psum(loss_local[0, 0], ("x", "y", "c"))

    gs = {'final_norm_w': d_final[0]}
    d_norm = [[None] * 3 for _ in range(DEPTH)]
    big = {}
    for i in reversed(range(DEPTH)):
        r1, rm, r2 = res[i]
        r = W['ffn_w_gate'].shape[3]
        dh, d_norm[i][2], dg, du, dd = ffn_bwd(f"ffn_{i}_1", dh, r2, nrm(i, 2), *ffn[i, 1])
        big['ffn', i, 1] = [_blocked(dg, r), _blocked(du, r), _blocked(dd, r)]
        if i % 4 == 0:
            dh, gm = mlstm_bwd(dh, rm, nrm(i, 1), mlstm_w[0], mlstm_w[1], full['mlstm_norm_w'])
            big['mlstm'] = [_blocked(gm['w_in_t'], r_mi), _blocked(gm['w_out'], r_mo)]
            gs.update(mlstm_b_if=gm['b_if'], mlstm_norm_w=gm['norm_w'])
        elif i % 4 == 1:
            dh, gm = pool_bwd(dh, rm, nrm(i, 1), pool_w, full['pool_scale'], pos)
            dw = gm['w'].reshape(N_POOL, N_DEV, pw.shape[1], pw.shape[2]).transpose(1, 0, 2, 3)
            big['pool'] = [_bf(dw.reshape(N_DEV, N_POOL * pw.shape[1], pw.shape[2]))]
            gs.update(pool_scale=gm['scale'])
        elif i % 4 == 2:
            dh, gm = gdn_bwd(dh, rm, nrm(i, 1), gdn_w[0], gdn_w[1], full['gdn_conv_w'][0], full['gdn_norm_w'])
            big['gdn'] = [_blocked(gm['w_in_t'], r_gi), _blocked(gm['w_out'], r_go)]
            gs.update(gdn_conv_w=gm['conv_w'][None], gdn_a_log=gm['a_log'], gdn_dt_bias=gm['dt_bias'], gdn_norm_w=gm['norm_w'])
        else:
            dh, gm = swa_bwd(dh, rm, nrm(i, 1), swa_w[0], swa_w[1], full['swa_sinks'], cos, sin, pos)
            big['swa'] = [_blocked(gm['w_qkv_t'], r_si), _blocked(gm['w_out'], r_so)]
            gs.update(swa_b_qkv=gm['b_qkv'], swa_b_out=gm['b_out'], swa_sinks=gm['sinks'])
        d_norm[i][1] = gm['nw']
        dh, d_norm[i][0], dg, du, dd = ffn_bwd(f"ffn_{i}_0", dh, r1, nrm(i, 0), *ffn[i, 0])
        big['ffn', i, 0] = [_blocked(dg, r), _blocked(du, r), _blocked(dd, r)]
    gs['norm_w'] = jnp.stack([jnp.concatenate(d_norm[i], axis=0) for i in range(DEPTH)])
    gs['meta_tokens'] = dh[:N_META]
    grad_x = dh[N_META:T][None]

    grads = {}
    small_names = [k for k in WEIGHTS if k in SMALL]
    (parts,) = all_gather("ag_small_grads", [_pack([gs[k].reshape(full[k].shape) for k in small_names])])
    tot = _unpack(sum_blocks("sum_small_grads", parts), [full[k].shape for k in small_names])
    for k, g in zip(small_names, tot):
        ax = SMALL[k]
        grads[k] = g if ax is None else lax.dynamic_slice_in_dim(g, me * W[k].shape[ax], W[k].shape[ax], axis=ax)

    def reduce(tag, blocks):
        return [sum_blocks(f"sum_{tag}_{j}", p) for j, p in enumerate(exchange_blocks("rs_" + tag, blocks))]

    gg, gu, gd = [], [], []
    for i in range(DEPTH):
        for s in range(2):
            r = W['ffn_w_gate'].shape[3]
            a, b, c_ = reduce(f"ffn_{i}_{s}", big['ffn', i, s])
            gg.append(a[:r].T)
            gu.append(b[:r].T)
            gd.append(c_[:r])
    shape4 = lambda lst, ref: jnp.stack(lst).reshape(ref.shape)
    grads['ffn_w_gate'] = shape4(gg, W['ffn_w_gate'])
    grads['ffn_w_up'] = shape4(gu, W['ffn_w_up'])
    grads['ffn_w_down'] = shape4(gd, W['ffn_w_down'])
    a, b = reduce("mlstm", big['mlstm'])
    grads['mlstm_w_in'], grads['mlstm_w_out'] = a[:r_mi].T[None], b[:r_mo][None]
    (a,) = reduce("pool", big['pool'])
    grads['pool_w'] = a.reshape(W['pool_w'].shape)
    a, b = reduce("gdn", big['gdn'])
    grads['gdn_w_in'], grads['gdn_w_out'] = a[:r_gi].T[None], b[:r_go][None]
    a, b = reduce("swa", big['swa'])
    grads['swa_w_qkv'], grads['swa_w_out'] = a[:r_si].T[None], b[:r_so][None]

    delta, new_m, new_v = {}, {}, {}
    shapes = [W[k].shape for k in small_names]
    d, nm, nv = adamw("adamw_small", _pack([W[k] for k in small_names]), _pack([grads[k] for k in small_names]),
                      _pack([M[k] for k in small_names]), _pack([V[k] for k in small_names]))
    for k, a, b, c_ in zip(small_names, _unpack(d, shapes), _unpack(nm, shapes), _unpack(nv, shapes)):
        delta[k], new_m[k], new_v[k] = a, b, c_
    for k in WEIGHTS:
        if k not in SMALL:
            two = lambda a: a.reshape(-1, a.shape[-1])
            d, nm, nv = adamw("adamw_" + k, two(W[k]), two(grads[k]), two(M[k]), two(V[k]))
            delta[k], new_m[k], new_v[k] = d.reshape(W[k].shape), nm.reshape(W[k].shape), nv.reshape(W[k].shape)

    return (loss, grad_x, *[grads[k].reshape(W[k].shape) for k in WEIGHTS], *[delta[k] for k in WEIGHTS],
            *[new_m[k] for k in WEIGHTS], *[new_v[k] for k in WEIGHTS])
```

```python
import functools

import jax
import jax.numpy as jnp
from jax import lax
from jax.experimental import pallas as pl
from jax.experimental.pallas import tpu as pltpu

F32 = jnp.float32
BF16 = jnp.bfloat16

N_DEV = 8
N_META = 16
EPS = 1e-6
DEPTH = 4
MLSTM_HEADS = 8
MLSTM_CHUNK = 64
N_POOL = 4
GDN_DK = 128
GDN_CHUNK = 64
GDN_CONV = 4
SWA_DH = 64
SWA_GROUP = 8
SWA_WINDOW = 128
ROPE_THETA = 10000.0
ADAM_LR = 0.001
ADAM_B1 = 0.9
ADAM_B2 = 0.999
ADAM_EPS = 1e-08
ADAM_WD = 0.01
ADAM_STEP = 10

LANES = 128
ROW_TILE = 128
ROW_PAD = 112
IN_PAD = 896
VMEM_LIMIT = 56 * 1024 * 1024
TOKEN_TILES = (1056, 768, 512, 384, 256, 128)
FEATURE_TILES = (512, 896, 768, 640, 384, 256, 128)
ROW_TILES = (512, 352, 256, 176, 160, 128, 112, 64, 48, 32, 16, 8)
HIGHEST = lax.Precision.HIGHEST
HIGH = lax.Precision.HIGH
MESH_ID = pl.DeviceIdType.MESH

WEIGHTS = ('meta_tokens', 'norm_w', 'ffn_w_gate', 'ffn_w_up', 'ffn_w_down', 'mlstm_w_in', 'mlstm_b_if',
           'mlstm_norm_w', 'mlstm_w_out', 'pool_w', 'pool_scale', 'gdn_w_in', 'gdn_conv_w', 'gdn_a_log',
           'gdn_dt_bias', 'gdn_norm_w', 'gdn_w_out', 'swa_w_qkv', 'swa_b_qkv', 'swa_sinks', 'swa_w_out',
           'swa_b_out', 'final_norm_w')
SMALL = {'meta_tokens': 1, 'norm_w': 2, 'mlstm_b_if': None, 'mlstm_norm_w': None, 'pool_scale': 1,
         'gdn_conv_w': 2, 'gdn_a_log': None, 'gdn_dt_bias': None, 'gdn_norm_w': None, 'swa_b_qkv': 1,
         'swa_sinks': None, 'swa_b_out': 1, 'final_norm_w': None}


def _pick(n, cands):
    for c in cands:
        if n % c == 0:
            return c
    return n


def _round_up(n, m):
    return -(-n // m) * m


def _bf(x):
    return x.astype(BF16)


def _dot(a, b, dims, precision=None):
    return lax.dot_general(a, b, (dims, ((), ())), preferred_element_type=F32, precision=precision)


def _mm(a, b):
    return _dot(_bf(a), _bf(b), ((1,), (0,)))


def _mm_nt(a, b):
    return _dot(_bf(a), _bf(b), ((1,), (1,)))


def _mm_tn(a, b):
    return _dot(_bf(a), _bf(b), ((0,), (0,)))


def _mm32(a, b):
    return _dot(a, b, ((1,), (0,)), precision=HIGHEST)


def _mm3(a, b):
    return _dot(a, b, ((1,), (0,)), precision=HIGH)


def _iota(shape, axis):
    return lax.broadcasted_iota(jnp.int32, shape, axis)


def _row2col(row, eye):
    return jnp.sum(eye * row, axis=1, keepdims=True)


def _lane_roll(shift):
    @jax.custom_vjp
    def f(x):
        return pltpu.roll(x, shift, 1)

    def fwd(x):
        return f(x), None

    def bwd(_, g):
        return (pltpu.roll(g, g.shape[1] - shift, 1),)

    f.defvjp(fwd, bwd)
    return f


def mm_call(name, mode, pairs, n_acc, epilogue, out_dtypes, extras=(), tm=None, tn=None, tk=None):
    a0, b0 = pairs[0][0], pairs[0][1]
    if mode == 'nn':
        (M, K), N = a0.shape, b0.shape[1]
    elif mode == 'nt':
        (M, K), N = a0.shape, b0.shape[0]
    else:
        (K, M), N = a0.shape, b0.shape[1]
    if mode == 'tn':
        tm = tm or _pick(M, FEATURE_TILES)
        tn = tn or (N if N <= 2048 else _pick(N, FEATURE_TILES))
        tk = tk or _pick(K, [t for t in (2112,) + TOKEN_TILES
                             if 2 * t * (tm * a0.dtype.itemsize + tn * b0.dtype.itemsize) <= VMEM_LIMIT // 2])
        dims = ((0,), (0,))
        a_spec = pl.BlockSpec((tk, tm), lambda i, j, k: (k, i))
        b_spec = pl.BlockSpec((tk, tn), lambda i, j, k: (k, j))
    else:
        tm = tm or _pick(M, TOKEN_TILES)
        tn = tn or _pick(N, FEATURE_TILES)
        tk = tk or (K if K <= 2048 else _pick(K, (2816, 2048) + FEATURE_TILES))
        a_spec = pl.BlockSpec((tm, tk), lambda i, j, k: (i, k))
        if mode == 'nn':
            dims = ((1,), (0,))
            b_spec = pl.BlockSpec((tk, tn), lambda i, j, k: (k, j))
        else:
            dims = ((1,), (1,))
            b_spec = pl.BlockSpec((tn, tk), lambda i, j, k: (j, k))
    n_pairs, n_ex, n_out = len(pairs), len(extras), len(out_dtypes)
    nk = K // tk

    def body(*refs):
        ab = refs[:2 * n_pairs]
        ex = refs[2 * n_pairs:2 * n_pairs + n_ex]
        outs = refs[2 * n_pairs + n_ex:2 * n_pairs + n_ex + n_out]
        accs = refs[2 * n_pairs + n_ex + n_out:]
        k = pl.program_id(2)

        @pl.when(k == 0)
        def _():
            for acc in accs:
                acc[...] = jnp.zeros_like(acc)

        for p, (_, _, ai) in enumerate(pairs):
            accs[ai][...] += _dot(_bf(ab[2 * p][...]), _bf(ab[2 * p + 1][...]), dims)

        @pl.when(k == nk - 1)
        def _():
            res = epilogue([acc[...] for acc in accs], [e[...] for e in ex])
            for o, v in zip(outs, res):
                o[...] = v.astype(o.dtype)

    ex_specs = [pl.BlockSpec((tm, tn), lambda i, j, k: (i, j)) if kind == 'mn'
                else pl.BlockSpec((1, tn), lambda i, j, k: (0, j)) if kind == 'n'
                else pl.BlockSpec(e.shape, lambda i, j, k: (0, 0)) for e, kind in extras]
    outs = pl.pallas_call(
        body, grid=(M // tm, N // tn, nk),
        in_specs=[a_spec, b_spec] * n_pairs + ex_specs,
        out_specs=[pl.BlockSpec((tm, tn), lambda i, j, k: (i, j)) for _ in out_dtypes],
        out_shape=[jax.ShapeDtypeStruct((M, N), dt) for dt in out_dtypes],
        scratch_shapes=[pltpu.VMEM((tm, tn), F32) for _ in range(n_acc)],
        compiler_params=pltpu.CompilerParams(dimension_semantics=("arbitrary",) * 3, vmem_limit_bytes=VMEM_LIMIT),
        name=name)(*[t for a, b, _ in pairs for t in (a, b)], *[e for e, _ in extras])
    return outs


def mm_plain(name, mode, a, b, dtype=F32, scale=None, dep=None):
    ep = (lambda accs, ex: [accs[0]]) if scale is None else (lambda accs, ex: [accs[0] * scale])
    return mm_call(name, mode, [(a, b, 0)], 1, ep, [dtype], extras=[] if dep is None else [(dep, 'dep')])[0]


def scan_fwd(name, step_fn, n_outer, n_steps, params, consts, xs, states, ys):
    n_p, n_c, n_x, n_s, n_y = len(params), len(consts), len(xs), len(states), len(ys)

    def body(*refs):
        p_refs = refs[:n_p]
        c_refs = refs[n_p:n_p + n_c]
        x_refs = refs[n_p + n_c:n_p + n_c + n_x]
        o = n_p + n_c + n_x
        y_refs = refs[o:o + n_y]
        sv_refs = refs[o + n_y:o + n_y + n_s]
        st_refs = refs[o + n_y + n_s:]
        s = pl.program_id(1)

        @pl.when(s == 0)
        def _():
            for r in st_refs:
                r[...] = jnp.zeros_like(r)

        st = tuple(r[...] for r in st_refs)
        for sv, v in zip(sv_refs, st):
            sv[...] = v
        new_st, y = step_fn(tuple(r[...] for r in p_refs), st, tuple(r[...] for r in x_refs),
                            tuple(r[...] for r in c_refs))
        for r, v in zip(y_refs, y):
            r[...] = v.astype(r.dtype)
        for r, v in zip(st_refs, new_st):
            r[...] = v

    in_specs = ([pl.BlockSpec(p[1], (lambda o, s, f=p[2]: f(o))) for p in params]
                + [pl.BlockSpec(c[1], c[2]) for c in consts]
                + [pl.BlockSpec(x[1], x[2]) for x in xs])
    out_specs = ([pl.BlockSpec(b, f) for _, _, b, f in ys]
                 + [pl.BlockSpec((None, None) + tuple(sh), (lambda o, s, n=len(sh): (o, s) + (0,) * n)) for sh, _ in states])
    out_shape = ([jax.ShapeDtypeStruct(sh, dt) for sh, dt, _, _ in ys]
                 + [jax.ShapeDtypeStruct((n_outer, n_steps) + tuple(sh), dt) for sh, dt in states])
    outs = pl.pallas_call(
        body, grid=(n_outer, n_steps), in_specs=in_specs, out_specs=out_specs, out_shape=out_shape,
        scratch_shapes=[pltpu.VMEM(tuple(sh), dt) for sh, dt in states],
        compiler_params=pltpu.CompilerParams(dimension_semantics=("arbitrary", "arbitrary"), vmem_limit_bytes=VMEM_LIMIT),
        name=name)(*[p[0] for p in params], *[c[0] for c in consts], *[x[0] for x in xs])
    return tuple(outs[:n_y]), tuple(outs[n_y:])


def scan_bwd(name, step_fn, n_outer, n_steps, params, consts, xs, states, saved, dys, glob):
    n_p, n_c, n_x, n_s, n_y = len(params), len(consts), len(xs), len(states), len(dys)
    rev = lambda f: (lambda o, s: f(o, n_steps - 1 - s))

    def body(*refs):
        p_refs = refs[:n_p]
        c_refs = refs[n_p:n_p + n_c]
        x_refs = refs[n_p + n_c:n_p + n_c + n_x]
        o = n_p + n_c + n_x
        sv_refs = refs[o:o + n_s]
        dy_refs = refs[o + n_s:o + n_s + n_y]
        o = o + n_s + n_y
        dx_refs = refs[o:o + n_x]
        dp_refs = refs[o + n_x:o + n_x + n_p]
        dst_refs = refs[o + n_x + n_p:]
        oi, s = pl.program_id(0), pl.program_id(1)

        @pl.when(s == 0)
        def _():
            for r in dst_refs:
                r[...] = jnp.zeros_like(r)

        for r, g in zip(dp_refs, glob):
            @pl.when(((s == 0) & (oi == 0)) if g else (s == 0))
            def _(r=r):
                r[...] = jnp.zeros_like(r)

        c_vals = tuple(r[...] for r in c_refs)
        f = lambda p, st, x: step_fn(p, st, x, c_vals)
        _, vjp = jax.vjp(f, tuple(r[...] for r in p_refs), tuple(r[...] for r in sv_refs), tuple(r[...] for r in x_refs))
        dp, dst, dx = vjp((tuple(r[...] for r in dst_refs), tuple(r[...] for r in dy_refs)))
        for r, v in zip(dx_refs, dx):
            r[...] = v.astype(r.dtype)
        for r, v in zip(dst_refs, dst):
            r[...] = v
        for r, v in zip(dp_refs, dp):
            r[...] += v

    gshape = lambda t: t[3] if len(t) > 3 else t[0].shape
    gidx = lambda t: t[4] if len(t) > 3 else t[2]
    in_specs = ([pl.BlockSpec(p[1], (lambda o, s, f=p[2]: f(o))) for p in params]
                + [pl.BlockSpec(c[1], rev(c[2])) for c in consts]
                + [pl.BlockSpec(x[1], rev(x[2])) for x in xs]
                + [pl.BlockSpec((None, None) + tuple(sh), (lambda o, s, n=len(sh): (o, n_steps - 1 - s) + (0,) * n)) for sh, _ in states]
                + [pl.BlockSpec(b, rev(f)) for _, b, f in dys])
    out_specs = ([pl.BlockSpec(x[1], rev(gidx(x))) for x in xs]
                 + [pl.BlockSpec(p[1], (lambda o, s, f=gidx(p): f(o))) for p in params])
    out_shape = ([jax.ShapeDtypeStruct(gshape(x), F32) for x in xs]
                 + [jax.ShapeDtypeStruct(gshape(p), F32) for p in params])
    outs = pl.pallas_call(
        body, grid=(n_outer, n_steps), in_specs=in_specs, out_specs=out_specs, out_shape=out_shape,
        scratch_shapes=[pltpu.VMEM(tuple(sh), dt) for sh, dt in states],
        compiler_params=pltpu.CompilerParams(dimension_semantics=("arbitrary", "arbitrary"), vmem_limit_bytes=VMEM_LIMIT),
        name=name)(*[p[0] for p in params], *[c[0] for c in consts], *[x[0] for x in xs], *saved,
                   *[d[0] for d in dys])
    return tuple(outs[:n_x]), tuple(outs[n_x:])


def _rows(a, rt):
    return (a, (rt, a.shape[1]), lambda o, s: (s, 0))


def _whole(a):
    return (a, a.shape, lambda o: (0,) * a.ndim)


def _rms(h, w):
    return h * lax.rsqrt(jnp.mean(h * h, axis=1, keepdims=True) + EPS) * w


def rms_fwd(name, h, w, dtype):
    TP, D = h.shape
    rt = _pick(TP, (384, 128))
    step = lambda p, st, x, c: ((), (_rms(x[0], p[0]),))
    (y,), _ = scan_fwd(name, step, 1, TP // rt, [_whole(w)], [], [_rows(h, rt)], [],
                       [((TP, D), dtype, (rt, D), lambda o, s: (s, 0))])
    return y


def rms_bwd(name, h, w, dxn, dres):
    TP, D = h.shape
    rt = _pick(TP, (384, 128))
    step = lambda p, st, x, c: ((), (_rms(x[0], p[0]), x[0]))
    (dh,), (dw,) = scan_bwd(name, step, 1, TP // rt, [_whole(w)], [], [_rows(h, rt)], [], [],
                            [_rows(dxn, rt), _rows(dres, rt)], [True])
    return dh, dw


def loss_call(h, w, tgt, mask):
    TP, D = h.shape
    rt = _pick(TP, (384, 128))

    def body(h_ref, w_ref, t_ref, m_ref, loss_ref, dh_ref, dw_ref):
        s = pl.program_id(0)

        @pl.when(s == 0)
        def _():
            loss_ref[...] = jnp.zeros_like(loss_ref)
            dw_ref[...] = jnp.zeros_like(dw_ref)

        tg, mk = t_ref[...], m_ref[...]

        def f(wv, hv):
            err = jnp.square(_rms(hv, wv) - tg) * mk
            return 0.5 * jnp.sum(jnp.sum(err, axis=1, keepdims=True), axis=0, keepdims=True) / D

        l, vjp = jax.vjp(f, w_ref[...], h_ref[...])
        dw, dh = vjp(jnp.ones((1, 1), F32))
        loss_ref[...] += l
        dw_ref[...] += dw
        dh_ref[...] = dh

    row = lambda wd: pl.BlockSpec((rt, wd), lambda s: (s, 0))
    const = lambda shape: pl.BlockSpec(shape, lambda s: (0, 0))
    return pl.pallas_call(
        body, grid=(TP // rt,), in_specs=[row(D), const((1, D)), row(D), row(1)],
        out_specs=[const((1, 1)), row(D), const((1, D))],
        out_shape=[jax.ShapeDtypeStruct((1, 1), F32), jax.ShapeDtypeStruct((TP, D), F32), jax.ShapeDtypeStruct((1, D), F32)],
        compiler_params=pltpu.CompilerParams(dimension_semantics=("arbitrary",), vmem_limit_bytes=VMEM_LIMIT),
        name="loss_head")(h, w, tgt, mask)


def colsum(name, a):
    TP, N = a.shape
    rt = _pick(TP, (384, 128))

    def body(a_ref, o_ref):
        @pl.when(pl.program_id(0) == 0)
        def _():
            o_ref[...] = jnp.zeros_like(o_ref)
        o_ref[...] += jnp.sum(a_ref[...], axis=0, keepdims=True)

    return pl.pallas_call(
        body, grid=(TP // rt,), in_specs=[pl.BlockSpec((rt, N), lambda s: (s, 0))],
        out_specs=pl.BlockSpec((1, N), lambda s: (0, 0)), out_shape=jax.ShapeDtypeStruct((1, N), F32),
        compiler_params=pltpu.CompilerParams(dimension_semantics=("arbitrary",)), name=name)(a)


def ffn_fwd(tag, h, nw, wg_t, wu_t, wd):
    xn = rms_fwd(tag + "_norm", h, nw, BF16)

    def ep_up(accs, ex):
        g, u = accs
        return [g, u, jax.nn.silu(g) * u]

    g, u, a = mm_call(tag + "_up", 'nt', [(xn, wg_t, 0), (xn, wu_t, 1)], 2, ep_up, [BF16, BF16, BF16])
    h2 = mm_call(tag + "_down", 'nn', [(a, wd, 0)], 1, lambda accs, ex: [ex[0] + 0.5 * accs[0]], [F32],
                 extras=[(h, 'mn')])[0]
    return h2, (h, xn, g, u, a)


def ffn_bwd(tag, dh2, res, nw, wg_t, wu_t, wd, dep):
    h, xn, g, u, a = res

    def ep_act(accs, ex):
        da = 0.5 * accs[0]
        gv, uv = ex[0].astype(F32), ex[1].astype(F32)
        sg = jax.nn.sigmoid(gv)
        return [da * uv * (sg * (1.0 + gv * (1.0 - sg))), da * (gv * sg)]

    dg, du = mm_call(tag + "_dact", 'nt', [(dh2, wd, 0)], 1, ep_act, [BF16, BF16],
                     extras=[(g, 'mn'), (u, 'mn'), (dep, 'dep')])
    dwd = mm_plain(tag + "_dwd", 'tn', a, dh2, BF16, scale=0.5)
    dxn = mm_call(tag + "_dxn", 'nn', [(dg, wg_t, 0), (du, wu_t, 0)], 1, lambda accs, ex: [accs[0]], [F32])[0]
    dwg_t = mm_plain(tag + "_dwg", 'tn', dg, xn, BF16)
    dwu_t = mm_plain(tag + "_dwu", 'tn', du, xn, BF16)
    dh, dnw = rms_bwd(tag + "_dnorm", h, nw, dxn, dh2)
    return dh, dnw, dwg_t, dwu_t, dwd


def _mlstm_step(params, state, xs, consts, *, dk):
    bif, nw = params
    c_st, n_st, m_st = state
    q, k, v, og, gr = xs
    C = MLSTM_CHUNK
    R = q.shape[0]
    ri, ci = _iota((C, C), 0), _iota((C, C), 1)
    eye = (ri == ci).astype(F32)
    causal = ci <= ri
    upper = (ri <= ci).astype(F32)
    outs = []
    for j in range(R // C):
        sl = slice(j * C, (j + 1) * C)
        qj, kj, vj = q[sl] * (dk ** -0.5), k[sl], v[sl]
        li = gr[0:1, sl] + bif[0:1, 0:1]
        lf = jax.nn.log_sigmoid(gr[1:2, sl] + bif[1:2, 0:1])
        b_row = _mm32(lf, upper)
        b_col = _row2col(b_row, eye)
        log_w = jnp.where(causal, b_col - b_row + li, -jnp.inf)
        log_init = b_col + m_st
        m_t = lax.stop_gradient(jnp.maximum(log_init, jnp.max(log_w, axis=1, keepdims=True)))
        w = jnp.exp(log_w - m_t)
        w_init = jnp.exp(log_init - m_t)
        qk = _mm_nt(qj, kj) * w
        num = w_init * _mm(qj, c_st) + _mm(qk, vj)
        den = w_init * jnp.sum(qj * n_st, axis=1, keepdims=True) + jnp.sum(qk, axis=1, keepdims=True)
        h = num / jnp.maximum(jnp.abs(den), jnp.exp(-m_t))
        b_last = b_row[:, C - 1:C]
        log_end_init = b_last + m_st
        log_end = b_last - b_row + li
        m_new = lax.stop_gradient(jnp.maximum(log_end_init, jnp.max(log_end, axis=1, keepdims=True)))
        a_init = jnp.exp(log_end_init - m_new)
        ka = kj * _row2col(jnp.exp(log_end - m_new), eye)
        c_st = a_init * c_st + _mm_tn(ka, vj)
        n_st = a_init * n_st + jnp.sum(ka, axis=0, keepdims=True)
        m_st = m_new
        hn = h * lax.rsqrt(jnp.mean(h * h, axis=1, keepdims=True) + EPS)
        outs.append(hn * nw * jax.nn.sigmoid(og[sl]))
    return (c_st, n_st, m_st), (jnp.concatenate(outs, axis=0),)


def _mlstm_ops(p, gr, bif, nw):
    H = MLSTM_HEADS
    TP = p.shape[0]
    dv = nw.shape[1] // H
    dk = dv // 2
    R = ROW_TILE
    step = functools.partial(_mlstm_step, dk=dk)
    params = [(bif, (None, 2, LANES), lambda o: (o, 0, 0)), (nw, (1, dv), lambda o: (0, o))]
    col = lambda w, off: (p, (R, w), (lambda o, s: (s, off + o)), (TP, H * w), (lambda o, s: (s, o)))
    xs = [col(dk, 0), col(dk, H), col(dv, H), col(dv, 2 * H), (gr, (None, 2, R), lambda o, s: (o, 0, s))]
    states = [((dk, dv), F32), ((1, dk), F32), ((1, 1), F32)]
    ys = [((TP, H * dv), F32, (R, dv), lambda o, s: (s, o))]
    return step, H, TP // R, params, xs, states, ys


def mlstm_fwd(h, nw1, w_in_t, w_out, b_if, norm_w):
    H = MLSTM_HEADS
    D = h.shape[1]
    u = rms_fwd("mlstm_norm", h, nw1, BF16)
    p = mm_plain("mlstm_in", 'nt', u, w_in_t)
    gr = p[:, 3 * D:3 * D + 2 * H].T.reshape(2, H, -1).transpose(1, 0, 2)
    bif = jnp.broadcast_to(b_if.reshape(2, H).T[:, :, None], (H, 2, LANES))
    step, _, n, params, xs, states, ys = _mlstm_ops(p, gr, bif, norm_w)
    (act,), saved = scan_fwd("mlstm_core", step, H, n, params, [], xs, states, ys)
    h2 = mm_call("mlstm_out", 'nn', [(act, w_out, 0)], 1, lambda accs, ex: [ex[0] + accs[0]], [F32], extras=[(h, 'mn')])[0]
    return h2, (h, u, p, gr, bif, saved, act)


def mlstm_bwd(dh2, res, nw1, w_in_t, w_out, norm_w, dep):
    h, u, p, gr, bif, saved, act = res
    H = MLSTM_HEADS
    TP = h.shape[0]
    dact = mm_plain("mlstm_dact", 'nt', dh2, w_out, dep=dep)
    dw_out = mm_plain("mlstm_dwout", 'tn', act, dh2, BF16)
    step, _, n, params, xs, states, ys = _mlstm_ops(p, gr, bif, norm_w)
    (dq, dk, dv, dog, dgr), (dbif, dnorm) = scan_bwd("mlstm_core_bwd", step, H, n, params, [], xs, states, saved,
                                                     [(dact, ys[0][2], ys[0][3])], [False, False])
    dgates = dgr.transpose(1, 0, 2).reshape(2 * H, TP).T
    pad = p.shape[1] - (dq.shape[1] + dk.shape[1] + dv.shape[1] + dog.shape[1] + 2 * H)
    dp = jnp.concatenate([dq, dk, dv, dog, dgates, jnp.zeros((TP, pad), F32)], axis=1)
    du = mm_plain("mlstm_du", 'nn', dp, w_in_t)
    dw_in_t = mm_plain("mlstm_dwin", 'tn', dp, u, BF16)
    dh, dnw1 = rms_bwd("mlstm_dnorm", h, nw1, du, dh2)
    db_if = dbif[:, :, 0].T.reshape(1, 2 * H)
    return dh, dict(nw=dnw1, w_in_t=dw_in_t, w_out=dw_out, b_if=db_if, norm_w=dnorm)


def _pool_step(params, state, xs, consts):
    w, scale = params
    (prev,) = state
    u, h = xs
    pos, win = consts
    R = u.shape[0]
    wn = win[0:1, 0:1]
    ext = jnp.concatenate([prev, u], axis=0)
    lag = _iota((R, 2 * R), 0) + R - _iota((R, 2 * R), 1)
    band = ((lag >= 0) & (lag < wn)).astype(F32)
    wsum = _mm32(band, ext)
    cnt = jnp.minimum(pos + 1, wn).astype(F32)
    pooled = wsum / cnt - u
    return (u,), (h + _mm(pooled, w) * scale,)


def _pool_ops(u, h, w, scale, pos):
    TP, D = u.shape
    G = D // N_POOL
    R = ROW_TILE
    win = jnp.broadcast_to(jnp.array([2 << g for g in range(N_POOL)], jnp.int32)[:, None, None], (N_POOL, 1, LANES))
    params = [(w, (None, G, G), lambda o: (o, 0, 0)), (scale, (1, G), lambda o: (0, o))]
    consts = [(pos, (R, 1), lambda o, s: (s, 0)), (win, (None, 1, LANES), lambda o, s: (o, 0, 0))]
    grp = lambda a: (a, (R, G), lambda o, s: (s, o))
    return N_POOL, TP // R, params, consts, [grp(u), grp(h)], [((R, G), F32)], [((TP, D), F32, (R, G), lambda o, s: (s, o))]


def pool_fwd(h, nw1, w, scale, pos):
    u = rms_fwd("pool_norm", h, nw1, F32)
    no, n, params, consts, xs, states, ys = _pool_ops(u, h, w, scale, pos)
    (h2,), saved = scan_fwd("pool_core", _pool_step, no, n, params, consts, xs, states, ys)
    return h2, (h, u, saved)


def pool_bwd(dh2, res, nw1, w, scale, pos):
    h, u, saved = res
    no, n, params, consts, xs, states, ys = _pool_ops(u, h, w, scale, pos)
    (du, dres), (dw, dscale) = scan_bwd("pool_core_bwd", _pool_step, no, n, params, consts, xs, states, saved,
                                        [(dh2, ys[0][2], ys[0][3])], [False, False])
    dh, dnw1 = rms_bwd("pool_dnorm", h, nw1, du, dres)
    return dh, dict(nw=dnw1, w=dw, scale=dscale)


def _unit_lower_inverse(low):
    C = low.shape[0]
    ri, ci = _iota((C, C), 0), _iota((C, C), 1)
    inv = (ri == ci).astype(F32)
    b = 1
    while b < C:
        blk = 2 * b
        sh = blk.bit_length() - 1
        off = jnp.where(((ri >> sh) == (ci >> sh)) & ((ri & (blk - 1)) >= b) & ((ci & (blk - 1)) < b), low, 0.0)
        inv = inv - (off if b == 1 else _mm3(_mm3(inv, off), inv))
        b = blk
    return inv


def _conv_silu(prev, x, w):
    R = x.shape[0]
    ext = jnp.concatenate([prev, x], axis=0)
    y = sum(w[j:j + 1, :] * ext[8 - (GDN_CONV - 1) + j:8 - (GDN_CONV - 1) + j + R] for j in range(GDN_CONV))
    return jax.nn.silu(y)


def _gdn_step(params, state, xs, consts):
    cwq, cwk, cwv, ad, gnw = params
    s_st, pq, pk, pv = state
    q, k, v, z, gb = xs
    C = GDN_CHUNK
    R, dk = q.shape
    dv = v.shape[1] // 2
    ri, ci = _iota((C, C), 0), _iota((C, C), 1)
    eye = (ri == ci).astype(F32)
    causal = ci <= ri
    strict = ci < ri
    upper = (ri <= ci).astype(F32)
    qc, kc, vc = _conv_silu(pq, q, cwq), _conv_silu(pk, k, cwk), _conv_silu(pv, v, cwv)
    qn = qc * lax.rsqrt(jnp.sum(qc * qc, axis=1, keepdims=True) + EPS) * (dk ** -0.5)
    kn = kc * lax.rsqrt(jnp.sum(kc * kc, axis=1, keepdims=True) + EPS)
    new_s, outs = [], []
    for e in range(2):
        s_e = s_st[e]
        beta_row = jax.nn.sigmoid(gb[e:e + 1, :])
        g_row = -jnp.exp(ad[e:e + 1, 0:1]) * jax.nn.softplus(gb[2 + e:3 + e, :] + ad[2 + e:3 + e, 0:1])
        ve, ze = vc[:, e * dv:(e + 1) * dv], z[:, e * dv:(e + 1) * dv]
        o_e = []
        for j in range(R // C):
            sl = slice(j * C, (j + 1) * C)
            qj, kj = qn[sl], kn[sl]
            gc_row = _mm32(g_row[:, sl], upper)
            gc_col = _row2col(gc_row, eye)
            decay = jnp.exp(jnp.where(causal, gc_col - gc_row, -jnp.inf))
            beta_col = _row2col(beta_row[:, sl], eye)
            kb = kj * beta_col
            low = jnp.where(strict, _mm_nt(kb, kj) * decay, 0.0)
            rhs = jnp.concatenate([ve[sl] * beta_col, kb * jnp.exp(gc_col)], axis=1)
            sol = _mm3(_unit_lower_inverse(low), rhs)
            v_new = sol[:, :dv] - _mm(sol[:, dv:], s_e)
            attn = _mm_nt(qj, kj) * decay
            o = _mm(qj * jnp.exp(gc_col), s_e) + _mm(attn, v_new)
            g_last = gc_row[:, C - 1:C]
            s_e = jnp.exp(g_last) * s_e + _mm_tn(kj * jnp.exp(g_last - gc_col), v_new)
            on = o * lax.rsqrt(jnp.mean(o * o, axis=1, keepdims=True) + EPS) * gnw
            o_e.append(on * jax.nn.silu(ze[sl]))
        new_s.append(s_e)
        outs.append(jnp.concatenate(o_e, axis=0))
    return (jnp.stack(new_s), q[R - 8:], k[R - 8:], v[R - 8:]), (jnp.concatenate(outs, axis=1),)


def _gdn_ops(p, gb, conv_w, ad, gnw):
    TP = p.shape[0]
    dk = GDN_DK
    nqk = gb.shape[0]
    R = ROW_TILE
    cw = lambda w, off: (conv_w, (GDN_CONV, w), (lambda o: (0, off + o)), (GDN_CONV, nqk * w), (lambda o: (0, o)))
    params = [cw(dk, 0), cw(dk, nqk), cw(2 * dk, nqk), (ad, (None, 4, LANES), lambda o: (o, 0, 0)), _whole(gnw)]
    col = lambda w, off: (p, (R, w), (lambda o, s: (s, off + o)), (TP, nqk * w), (lambda o, s: (s, o)))
    xs = [col(dk, 0), col(dk, nqk), col(2 * dk, nqk), col(2 * dk, 2 * nqk), (gb, (None, 4, R), lambda o, s: (o, 0, s))]
    states = [((2, dk, dk), F32), ((8, dk), F32), ((8, dk), F32), ((8, 2 * dk), F32)]
    ys = [((TP, 2 * nqk * dk), F32, (R, 2 * dk), lambda o, s: (s, o))]
    return nqk, TP // R, params, xs, states, ys


def gdn_fwd(h, nw1, w_in_t, w_out, conv_w, a_log, dt_bias, gnw):
    D = h.shape[1]
    nqk = D // GDN_DK
    u = rms_fwd("gdn_norm", h, nw1, BF16)
    p = mm_plain("gdn_in", 'nt', u, w_in_t)
    gb = p[:, 6 * D:6 * D + 4 * nqk].T.reshape(2, nqk, 2, -1).transpose(1, 0, 2, 3).reshape(nqk, 4, -1)
    ad = jnp.concatenate([a_log.reshape(nqk, 2), dt_bias.reshape(nqk, 2)], axis=1)
    ad = jnp.broadcast_to(ad[:, :, None], (nqk, 4, LANES))
    no, n, params, xs, states, ys = _gdn_ops(p, gb, conv_w, ad, gnw)
    (act,), saved = scan_fwd("gdn_core", _gdn_step, no, n, params, [], xs, states, ys)
    h2 = mm_call("gdn_out", 'nn', [(act, w_out, 0)], 1, lambda accs, ex: [ex[0] + accs[0]], [F32], extras=[(h, 'mn')])[0]
    return h2, (h, u, p, gb, ad, saved, act)


def gdn_bwd(dh2, res, nw1, w_in_t, w_out, conv_w, gnw, dep):
    h, u, p, gb, ad, saved, act = res
    TP, D = h.shape
    nqk = D // GDN_DK
    dact = mm_plain("gdn_dact", 'nt', dh2, w_out, dep=dep)
    dw_out = mm_plain("gdn_dwout", 'tn', act, dh2, BF16)
    no, n, params, xs, states, ys = _gdn_ops(p, gb, conv_w, ad, gnw)
    (dq, dk, dv, dz, dgb), (dcq, dck, dcv, dad, dgnw) = scan_bwd(
        "gdn_core_bwd", _gdn_step, no, n, params, [], xs, states, saved, [(dact, ys[0][2], ys[0][3])],
        [False, False, False, False, True])
    dgates = dgb.reshape(nqk, 2, 2, TP).transpose(1, 0, 2, 3).reshape(4 * nqk, TP).T
    pad = p.shape[1] - (6 * D + 4 * nqk)
    dp = jnp.concatenate([dq, dk, dv, dz, dgates, jnp.zeros((TP, pad), F32)], axis=1)
    du = mm_plain("gdn_du", 'nn', dp, w_in_t)
    dw_in_t = mm_plain("gdn_dwin", 'tn', dp, u, BF16)
    dh, dnw1 = rms_bwd("gdn_dnorm", h, nw1, du, dh2)
    dad = dad[:, :, 0]
    return dh, dict(nw=dnw1, w_in_t=dw_in_t, w_out=dw_out, conv_w=jnp.concatenate([dcq, dck, dcv], axis=1),
                    a_log=dad[:, :2].reshape(1, 2 * nqk), dt_bias=dad[:, 2:].reshape(1, 2 * nqk), norm_w=dgnw)


def _rope(x, cos, sin):
    W = x.shape[1]
    first_half = (_iota(x.shape, 1) & (SWA_DH - 1)) < SWA_DH // 2
    rot = jnp.where(first_half, -_lane_roll(W - SWA_DH // 2)(x), _lane_roll(SWA_DH // 2)(x))
    return x * jnp.tile(cos, (1, W // LANES)) + rot * jnp.tile(sin, (1, W // LANES))


def _swa_step(params, state, xs, consts):
    (sinks,) = params
    kprev, vprev = state
    q, k, v = xs
    cos, sin, pos = consts
    R = q.shape[0]
    hkv = k.shape[1] // SWA_DH
    G = SWA_GROUP
    qr, kr = _rope(q, cos, sin), _rope(k, cos, sin)
    k2, v2 = jnp.concatenate([kprev, kr], axis=0), jnp.concatenate([vprev, v], axis=0)
    lane = _iota((R, LANES), 1)
    qpos = jnp.concatenate([pos] * G, axis=0)
    kpos = pos[0:1, 0:1] - R + _iota((1, 2 * R), 1)
    mask = (kpos <= qpos) & (qpos - kpos < SWA_WINDOW) & (kpos >= 0)
    sel_r, sel_c = _iota((hkv * SWA_DH, LANES), 0), _iota((hkv * SWA_DH, LANES), 1)
    out = []
    for hh in range(hkv):
        sel = _bf((sel_r == hh * SWA_DH + (sel_c & (SWA_DH - 1))).astype(F32))
        kd, vd = _mm(k2, sel), _mm(v2, sel)
        q8 = []
        for i in range(G // 2):
            qb = qr[:, (hh * G // 2 + i) * LANES:(hh * G // 2 + i + 1) * LANES]
            q8 += [jnp.where(lane < SWA_DH, qb, 0.0), jnp.where(lane >= SWA_DH, qb, 0.0)]
        s = _mm_nt(jnp.concatenate(q8, axis=0), kd) * (SWA_DH ** -0.5)
        s = jnp.where(mask, s, -jnp.inf)
        sink = jnp.concatenate([jnp.broadcast_to(sinks[0:1, hh * G + g:hh * G + g + 1], (R, 1)) for g in range(G)], axis=0)
        m = lax.stop_gradient(jnp.maximum(jnp.max(s, axis=1, keepdims=True), sink))
        e = jnp.exp(s - m)
        prob = e / (jnp.sum(e, axis=1, keepdims=True) + jnp.exp(sink - m))
        o8 = _mm(prob, vd)
        for i in range(G // 2):
            out.append(jnp.where(lane < SWA_DH, o8[2 * i * R:(2 * i + 1) * R], o8[(2 * i + 1) * R:(2 * i + 2) * R]))
    return (kr, v), (jnp.concatenate(out, axis=1),)


def _swa_ops(p, sinks, cos, sin, pos):
    TP = p.shape[0]
    R = ROW_TILE
    hq = sinks.shape[1]
    wq, wkv = hq * SWA_DH, hq // SWA_GROUP * SWA_DH
    nb = wq // wkv
    xs = [(p, (R, wq), (lambda o, s: (s, 0)), (TP, wq), (lambda o, s: (s, 0))),
          (p, (R, wkv), (lambda o, s: (s, nb)), (TP, wkv), (lambda o, s: (s, 0))),
          (p, (R, wkv), (lambda o, s: (s, nb + 1)), (TP, wkv), (lambda o, s: (s, 0)))]
    consts = [_rows(cos, R), _rows(sin, R), _rows(pos, R)]
    states = [((R, wkv), F32), ((R, wkv), F32)]
    ys = [((TP, wq), F32, (R, wq), lambda o, s: (s, 0))]
    return TP // R, [_whole(sinks)], consts, xs, states, ys


def swa_fwd(h, nw1, w_qkv_t, b_qkv, w_out, b_out, sinks, cos, sin, pos):
    u = rms_fwd("swa_norm", h, nw1, BF16)
    p = mm_call("swa_in", 'nt', [(u, w_qkv_t, 0)], 1, lambda accs, ex: [accs[0] + ex[0]], [F32], extras=[(b_qkv, 'n')])[0]
    n, params, consts, xs, states, ys = _swa_ops(p, sinks, cos, sin, pos)
    (act,), saved = scan_fwd("swa_core", _swa_step, 1, n, params, consts, xs, states, ys)
    h2 = mm_call("swa_out", 'nn', [(act, w_out, 0)], 1, lambda accs, ex: [ex[0] + accs[0] + ex[1]], [F32],
                 extras=[(h, 'mn'), (b_out, 'n')])[0]
    return h2, (h, u, p, saved, act)


def swa_bwd(dh2, res, nw1, w_qkv_t, w_out, sinks, cos, sin, pos, dep):
    h, u, p, saved, act = res
    dact = mm_plain("swa_dact", 'nt', dh2, w_out, dep=dep)
    dw_out = mm_plain("swa_dwout", 'tn', act, dh2, BF16)
    db_out = colsum("swa_dbout", dh2)
    n, params, consts, xs, states, ys = _swa_ops(p, sinks, cos, sin, pos)
    (dq, dk, dv), (dsinks,) = scan_bwd("swa_core_bwd", _swa_step, 1, n, params, consts, xs, states, saved,
                                       [(dact, ys[0][2], ys[0][3])], [True])
    dp = jnp.concatenate([dq, dk, dv], axis=1)
    db_qkv = colsum("swa_dbqkv", dp)
    du = mm_plain("swa_du", 'nn', dp, w_qkv_t)
    dw_qkv_t = mm_plain("swa_dwqkv", 'tn', dp, u, BF16)
    dh, dnw1 = rms_bwd("swa_dnorm", h, nw1, du, dh2)
    return dh, dict(nw=dnw1, w_qkv_t=dw_qkv_t, w_out=dw_out, b_qkv=db_qkv, b_out=db_out, sinks=dsinks)


def _dev_index(dev):
    return 4 * dev[0] + 2 * dev[1] + dev[2]


def all_gather(name, shards):
    n = len(shards)

    def body(*refs):
        x_refs, out_refs = refs[:n], refs[n:2 * n]
        send_sems, recv_sems, local_sem = refs[2 * n:]
        x, y, c = lax.axis_index("x"), lax.axis_index("y"), lax.axis_index("c")
        me, sibling = (x, y, c), (x, y, 1 - c)
        chips = [(1 - x, y), (x, 1 - y), (1 - x, 1 - y)]

        def copy(a, k, block, to, src=None):
            dst = out_refs[a].at[_dev_index(block)]
            return pltpu.make_async_remote_copy(src_ref=dst if src is None else src, dst_ref=dst,
                                                send_sem=send_sems.at[a, k], recv_sem=recv_sems.at[a, k],
                                                device_id=to, device_id_type=MESH_ID)

        mine = [pltpu.make_async_copy(x_refs[a], out_refs[a].at[_dev_index(me)], local_sem.at[a]) for a in range(n)]
        first, passed = [], []
        for a in range(n):
            mine[a].start()
            first += [copy(a, 0, me, sibling, src=x_refs[a])]
            first += [copy(a, 1 + j, me, (*chip, c), src=x_refs[a]) for j, chip in enumerate(chips)]
        for cp in first:
            cp.start()
        for a in range(n):
            for j, chip in enumerate(chips):
                copy(a, 1 + j, (*chip, c), me).wait_recv()
                fwd = copy(a, 4 + j, (*chip, c), sibling)
                fwd.start()
                passed.append(fwd)
        for a in range(n):
            copy(a, 0, sibling, me).wait_recv()
            for j, chip in enumerate(chips):
                copy(a, 4 + j, (*chip, 1 - c), me).wait_recv()
        for cp in first + passed:
            cp.wait_send()
        for cp in mine:
            cp.wait()

    any_spec = pl.BlockSpec(memory_space=pl.ANY)
    return pl.pallas_call(
        body, in_specs=[any_spec] * n, out_specs=[any_spec] * n,
        out_shape=[jax.ShapeDtypeStruct((N_DEV,) + s.shape, s.dtype) for s in shards],
        scratch_shapes=[pltpu.SemaphoreType.DMA((n, 7)), pltpu.SemaphoreType.DMA((n, 7)), pltpu.SemaphoreType.DMA((n,))],
        name=name)(*shards)


def exchange_blocks(name, fulls):
    n = len(fulls)

    def body(*refs):
        g_refs, out_refs = refs[:n], refs[n:2 * n]
        send_sems, recv_sems, local_sem = refs[2 * n:]
        x, y, c = lax.axis_index("x"), lax.axis_index("y"), lax.axis_index("c")
        me = (x, y, c)
        peers = [(1 - x if r & 4 else x, 1 - y if r & 2 else y, 1 - c if r & 1 else c) for r in range(1, N_DEV)]

        def copy(a, k, peer):
            return pltpu.make_async_remote_copy(src_ref=g_refs[a].at[_dev_index(peer)], dst_ref=out_refs[a].at[_dev_index(me)],
                                                send_sem=send_sems.at[a, k], recv_sem=recv_sems.at[a, k],
                                                device_id=peer, device_id_type=MESH_ID)

        mine = [pltpu.make_async_copy(g_refs[a].at[_dev_index(me)], out_refs[a].at[_dev_index(me)], local_sem.at[a])
                for a in range(n)]
        sends = [copy(a, k, peer) for a in range(n) for k, peer in enumerate(peers)]
        for cp in mine + sends:
            cp.start()
        for a in range(n):
            for k, peer in enumerate(peers):
                pltpu.make_async_remote_copy(src_ref=g_refs[a].at[_dev_index(peer)], dst_ref=out_refs[a].at[_dev_index(peer)],
                                             send_sem=send_sems.at[a, k], recv_sem=recv_sems.at[a, k],
                                             device_id=peer, device_id_type=MESH_ID).wait_recv()
        for cp in sends:
            cp.wait_send()
        for cp in mine:
            cp.wait()

    any_spec = pl.BlockSpec(memory_space=pl.ANY)
    return pl.pallas_call(
        body, in_specs=[any_spec] * n, out_specs=[any_spec] * n,
        out_shape=[jax.ShapeDtypeStruct(g.shape, g.dtype) for g in fulls],
        scratch_shapes=[pltpu.SemaphoreType.DMA((n, 7)), pltpu.SemaphoreType.DMA((n, 7)), pltpu.SemaphoreType.DMA((n,))],
        name=name)(*fulls)


def _peers_of(x, y, c):
    return [(1 - x if r & 4 else x, 1 - y if r & 2 else y, 1 - c if r & 1 else c) for r in range(1, N_DEV)]


def push_start(name, srcs, lands, after, gather):
    n = len(srcs)

    def body(*refs):
        src_refs, land_refs = refs[:n], refs[n:2 * n]
        send_sems, recv_sems = refs[2 * n + 1], refs[2 * n + 2]
        token = refs[-1]
        x, y, c = lax.axis_index("x"), lax.axis_index("y"), lax.axis_index("c")
        me = (x, y, c)
        for a in range(n):
            for k, peer in enumerate(_peers_of(x, y, c)):
                pltpu.make_async_remote_copy(
                    src_ref=src_refs[a] if gather else src_refs[a].at[_dev_index(peer)],
                    dst_ref=land_refs[a].at[_dev_index(me)], send_sem=send_sems.at[a * (N_DEV - 1) + k],
                    recv_sem=recv_sems.at[a * (N_DEV - 1) + k],
                    device_id=peer, device_id_type=MESH_ID).start()
        token[...] = jnp.zeros_like(token)

    hbm = pl.BlockSpec(memory_space=pltpu.HBM)
    sem = pl.BlockSpec(memory_space=pltpu.SEMAPHORE)
    outs = pl.pallas_call(
        body, name=name,
        out_shape=(pltpu.SemaphoreType.DMA((n * (N_DEV - 1),)), pltpu.SemaphoreType.DMA((n * (N_DEV - 1),)),
                   *[pltpu.HBM(s.shape, s.dtype) for s in srcs], *[pltpu.HBM(l.shape, l.dtype) for l in lands],
                   jax.ShapeDtypeStruct((8, LANES), F32)),
        in_specs=[hbm] * (2 * n) + [pl.BlockSpec(memory_space=pl.ANY)],
        out_specs=(sem, sem, *[hbm] * (2 * n), pl.BlockSpec(memory_space=pltpu.VMEM)),
        input_output_aliases={i: 2 + i for i in range(2 * n)},
        compiler_params=pltpu.CompilerParams(has_side_effects=pltpu.SideEffectType.DATAFLOW_SIDE_EFFECTING),
    )(*[pltpu.with_memory_space_constraint(s, pltpu.HBM) for s in srcs],
      *[pltpu.with_memory_space_constraint(l, pltpu.HBM) for l in lands], after)
    return (outs[0], outs[1], outs[2:2 + n], outs[2 + n:2 + 2 * n], gather), outs[-1]


def push_wait(name, handle, after):
    send_sems, recv_sems, srcs, lands, gather = handle
    n = len(srcs)

    def body(*refs):
        src_refs, land_refs = refs[:n], refs[n:2 * n]
        send_sem_ref, recv_sem_ref = refs[2 * n], refs[2 * n + 1]
        x, y, c = lax.axis_index("x"), lax.axis_index("y"), lax.axis_index("c")
        for a in range(n):
            for k, peer in enumerate(_peers_of(x, y, c)):
                cp = pltpu.make_async_remote_copy(
                    src_ref=src_refs[a] if gather else src_refs[a].at[_dev_index(peer)],
                    dst_ref=land_refs[a].at[_dev_index(peer)], send_sem=send_sem_ref.at[a * (N_DEV - 1) + k],
                    recv_sem=recv_sem_ref.at[a * (N_DEV - 1) + k],
                    device_id=peer, device_id_type=MESH_ID)
                cp.wait_send()
                cp.wait_recv()

    hbm = pl.BlockSpec(memory_space=pltpu.HBM)
    sem = pl.BlockSpec(memory_space=pltpu.SEMAPHORE)
    outs = pl.pallas_call(
        body, name=name,
        out_shape=(*[pltpu.HBM(s.shape, s.dtype) for s in srcs], *[pltpu.HBM(l.shape, l.dtype) for l in lands]),
        in_specs=[hbm] * (2 * n) + [sem, sem, pl.BlockSpec(memory_space=pl.ANY)],
        out_specs=tuple([hbm] * (2 * n)),
        input_output_aliases={i: i for i in range(2 * n)},
        compiler_params=pltpu.CompilerParams(has_side_effects=pltpu.SideEffectType.DATAFLOW_SIDE_EFFECTING),
    )(*srcs, *lands, send_sems, recv_sems, after)
    return outs[n:]


def _own_block(block, me):
    land = lax.empty((N_DEV,) + block.shape, block.dtype)
    return lax.dynamic_update_slice(land, block[None], (me,) + (0,) * block.ndim)


def sum_blocks(name, parts):
    _, r, c = parts.shape
    tr = _pick(r, ROW_TILES)

    def body(p_ref, o_ref):
        acc = p_ref[0].astype(F32)
        for b in range(1, N_DEV):
            acc = acc + p_ref[b].astype(F32)
        o_ref[...] = acc

    return pl.pallas_call(
        body, grid=(r // tr,), in_specs=[pl.BlockSpec((N_DEV, tr, c), lambda i: (0, i, 0))],
        out_specs=pl.BlockSpec((tr, c), lambda i: (i, 0)), out_shape=jax.ShapeDtypeStruct((r, c), F32),
        compiler_params=pltpu.CompilerParams(dimension_semantics=("arbitrary",), vmem_limit_bytes=VMEM_LIMIT),
        name=name)(parts)


def adamw(name, w, g, m, v):
    r, c = w.shape
    tr = _pick(r, [t for t in ROW_TILES if t * c * 4 * 7 * 2 <= VMEM_LIMIT // 2])

    def body(w_ref, g_ref, m_ref, v_ref, d_ref, nm_ref, nv_ref):
        gv = g_ref[...]
        nm = ADAM_B1 * m_ref[...] + (1.0 - ADAM_B1) * gv
        nv = ADAM_B2 * v_ref[...] + (1.0 - ADAM_B2) * jnp.square(gv)
        m_hat = nm / (1.0 - ADAM_B1 ** ADAM_STEP)
        v_hat = nv / (1.0 - ADAM_B2 ** ADAM_STEP)
        d_ref[...] = -ADAM_LR * (m_hat / (jnp.sqrt(v_hat) + ADAM_EPS) + ADAM_WD * w_ref[...])
        nm_ref[...] = nm
        nv_ref[...] = nv

    spec = pl.BlockSpec((tr, c), lambda i: (i, 0))
    return pl.pallas_call(
        body, grid=(r // tr,), in_specs=[spec] * 4, out_specs=[spec] * 3,
        out_shape=[jax.ShapeDtypeStruct((r, c), F32)] * 3,
        compiler_params=pltpu.CompilerParams(dimension_semantics=("arbitrary",), vmem_limit_bytes=VMEM_LIMIT),
        name=name)(w, g, m, v)


def _pack(arrays):
    flat = jnp.concatenate([a.reshape(-1) for a in arrays])
    n = _round_up(flat.shape[0], 8 * LANES)
    return jnp.pad(flat, (0, n - flat.shape[0])).reshape(-1, LANES)


def _unpack(packed, shapes):
    flat = packed.reshape(-1)
    out, o = [], 0
    for sh in shapes:
        sz = 1
        for d in sh:
            sz *= d
        out.append(flat[o:o + sz].reshape(sh))
        o += sz
    return out


def _gather_axis(g, ax):
    g = jnp.moveaxis(g, 0, ax)
    return g.reshape(g.shape[:ax] + (g.shape[ax] * g.shape[ax + 1],) + g.shape[ax + 2:])


def _comm_rows(a):
    r = a.shape[0]
    rp = r if r % 16 == 0 else _round_up(r, ROW_PAD)
    return jnp.pad(_bf(a), ((0, rp - r), (0, 0)))


def _natural(g, r, pad_to=None):
    full = g[:, :r].reshape(N_DEV * r, g.shape[2])
    if pad_to is not None and full.shape[0] % pad_to:
        full = jnp.pad(full, ((0, _round_up(full.shape[0], pad_to) - full.shape[0]), (0, 0)))
    return full


def _blocked(full, r):
    blocks = full[:N_DEV * r].reshape(N_DEV, r, full.shape[1])
    rp = r if r % 16 == 0 else _round_up(r, ROW_PAD)
    return jnp.pad(_bf(blocks), ((0, 0), (0, rp - r), (0, 0)))


def kernel(x, *rest):
    nw_ = len(WEIGHTS)
    W = dict(zip(WEIGHTS, rest[:nw_]))
    loss_target = rest[nw_]
    M = dict(zip(WEIGHTS, rest[nw_ + 1:2 * nw_ + 1]))
    V = dict(zip(WEIGHTS, rest[2 * nw_ + 1:3 * nw_ + 1]))

    T0, D = x.shape[1], x.shape[2]
    T = T0 + N_META
    TP = _round_up(T, ROW_TILE)
    me = 4 * lax.axis_index("x") + 2 * lax.axis_index("y") + lax.axis_index("c")

    small_sharded = [k for k in WEIGHTS if k in SMALL and SMALL[k] is not None]
    (sg,) = all_gather("ag_small", [_pack([W[k] for k in small_sharded])])
    per_dev = [_unpack(sg[b], [W[k].shape for k in small_sharded]) for b in range(N_DEV)]
    full = {k: _gather_axis(jnp.stack([per_dev[b][i] for b in range(N_DEV)]), SMALL[k]) for i, k in enumerate(small_sharded)}
    for k in SMALL:
        if SMALL[k] is None:
            full[k] = W[k]

    r_ff = W['ffn_w_gate'].shape[3]
    r_mi, r_mo = W['mlstm_w_in'].shape[2], W['mlstm_w_out'].shape[1]
    r_gi, r_go = W['gdn_w_in'].shape[2], W['gdn_w_out'].shape[1]
    r_si, r_so = W['swa_w_qkv'].shape[2], W['swa_w_out'].shape[1]
    pw = W['pool_w'][0]
    stages = [(i, j) for i in range(DEPTH) for j in range(3)]

    def stage_shards(i, j):
        if j != 1:
            s = j // 2
            return [_comm_rows(W['ffn_w_gate'][i, s].T), _comm_rows(W['ffn_w_up'][i, s].T), _comm_rows(W['ffn_w_down'][i, s])]
        if i % 4 == 0:
            return [_comm_rows(W['mlstm_w_in'][0].T), _comm_rows(W['mlstm_w_out'][0])]
        if i % 4 == 1:
            return [_bf(pw.reshape(-1, pw.shape[2]))]
        if i % 4 == 2:
            return [_comm_rows(W['gdn_w_in'][0].T), _comm_rows(W['gdn_w_out'][0])]
        return [_comm_rows(W['swa_w_qkv'][0].T), _comm_rows(W['swa_w_out'][0])]

    def stage_weights(i, j, g):
        if j != 1:
            return tuple(_natural(a, r_ff) for a in g)
        if i % 4 == 0:
            return (_natural(g[0], r_mi, IN_PAD), _natural(g[1], r_mo))
        if i % 4 == 1:
            return (g[0].reshape(N_DEV, N_POOL, pw.shape[1], pw.shape[2]).transpose(1, 0, 2, 3)
                    .reshape(N_POOL, pw.shape[2], pw.shape[2]).astype(F32),)
        if i % 4 == 2:
            return (_natural(g[0], r_gi, IN_PAD), _natural(g[1], r_go))
        return (_natural(g[0], r_si), _natural(g[1], r_so))

    def gather_start(k, after):
        sh = stage_shards(*stages[k])
        return push_start(f"ag_start_{k}", sh, [_own_block(s, me) for s in sh], after, True)

    first = all_gather("ag_stage_0", stage_shards(*stages[0]))
    wts = {0: stage_weights(*stages[0], first)}
    pending, zero = {}, jnp.zeros((), F32)
    pending[1], tok1 = gather_start(1, first[0])
    pending[2], tok2 = gather_start(2, tok1)
    tok = tok1[0, 0] + tok2[0, 0]

    pos = jnp.arange(TP, dtype=jnp.int32)[:, None]
    inv = ROPE_THETA ** (-jnp.arange(0, SWA_DH, 2, dtype=F32) / SWA_DH)
    ang = jnp.arange(TP, dtype=F32)[:, None] * inv[None, :]
    ang = jnp.concatenate([ang, ang, ang, ang], axis=1)
    cos, sin = jnp.cos(ang), jnp.sin(ang)
    row_mask = ((pos >= N_META) & (pos < T)).astype(F32)
    tgt = jnp.pad(loss_target[0], ((N_META, TP - T), (0, 0)))
    nrm = lambda i, j: full['norm_w'][i, j][None, :]

    h = jnp.concatenate([full['meta_tokens'], x[0], jnp.zeros((TP - T, D), F32)], axis=0)
    res = {}
    for k, (i, j) in enumerate(stages):
        if k >= 1:
            wts[k] = stage_weights(i, j, push_wait(f"ag_wait_{k}", pending.pop(k), h))
            tok = zero
            if k + 2 < len(stages):
                pending[k + 2], t = gather_start(k + 2, h)
                tok = t[0, 0]
        nw = nrm(i, j) + tok
        w = wts[k]
        if j != 1:
            h, res[k] = ffn_fwd(f"ffn_{i}_{j // 2}", h, nw, *w)
        elif i % 4 == 0:
            h, res[k] = mlstm_fwd(h, nw, w[0], w[1], full['mlstm_b_if'], full['mlstm_norm_w'])
        elif i % 4 == 1:
            h, res[k] = pool_fwd(h, nw, w[0], full['pool_scale'], pos)
        elif i % 4 == 2:
            h, res[k] = gdn_fwd(h, nw, w[0], w[1], full['gdn_conv_w'][0], full['gdn_a_log'], full['gdn_dt_bias'],
                                full['gdn_norm_w'])
        else:
            h, res[k] = swa_fwd(h, nw, w[0], full['swa_b_qkv'], w[1], full['swa_b_out'], full['swa_sinks'], cos, sin, pos)

    loss_local, dh, d_final = loss_call(h, full['final_norm_w'][None, :], tgt, row_mask)
    loss = lax.psum(loss_local[0, 0], ("x", "y", "c"))

    gs = {'final_norm_w': d_final[0]}
    d_norm = [[None] * 3 for _ in range(DEPTH)]
    sent = {}
    dep = jnp.zeros((8, LANES), F32)
    for k in reversed(range(len(stages))):
        i, j = stages[k]
        nw, w = nrm(i, j), wts[k]
        if j != 1:
            dh, d_norm[i][j], dg, du, dd = ffn_bwd(f"ffn_{i}_{j // 2}", dh, res[k], nw, *w, dep)
            blocks = [_blocked(dg, r_ff), _blocked(du, r_ff), _blocked(dd, r_ff)]
        else:
            if i % 4 == 0:
                dh, gm = mlstm_bwd(dh, res[k], nw, w[0], w[1], full['mlstm_norm_w'], dep)
                blocks = [_blocked(gm['w_in_t'], r_mi), _blocked(gm['w_out'], r_mo)]
                gs.update(mlstm_b_if=gm['b_if'], mlstm_norm_w=gm['norm_w'])
            elif i % 4 == 1:
                dh, gm = pool_bwd(dh, res[k], nw + dep[0, 0], w[0], full['pool_scale'], pos)
                dw = gm['w'].reshape(N_POOL, N_DEV, pw.shape[1], pw.shape[2]).transpose(1, 0, 2, 3)
                blocks = [_bf(dw.reshape(N_DEV, N_POOL * pw.shape[1], pw.shape[2]))]
                gs.update(pool_scale=gm['scale'])
            elif i % 4 == 2:
                dh, gm = gdn_bwd(dh, res[k], nw, w[0], w[1], full['gdn_conv_w'][0], full['gdn_norm_w'], dep)
                blocks = [_blocked(gm['w_in_t'], r_gi), _blocked(gm['w_out'], r_go)]
                gs.update(gdn_conv_w=gm['conv_w'][None], gdn_a_log=gm['a_log'], gdn_dt_bias=gm['dt_bias'],
                          gdn_norm_w=gm['norm_w'])
            else:
                dh, gm = swa_bwd(dh, res[k], nw, w[0], w[1], full['swa_sinks'], cos, sin, pos, dep)
                blocks = [_blocked(gm['w_qkv_t'], r_si), _blocked(gm['w_out'], r_so)]
                gs.update(swa_b_qkv=gm['b_qkv'], swa_b_out=gm['b_out'], swa_sinks=gm['sinks'])
            d_norm[i][1] = gm['nw']
        lands = [_own_block(lax.dynamic_index_in_dim(b, me, 0, keepdims=False), me) for b in blocks]
        sent[k], dep = push_start(f"rs_start_{k}", blocks, lands, dh, False)
    gs['norm_w'] = jnp.stack([jnp.concatenate(d_norm[i], axis=0) for i in range(DEPTH)])
    gs['meta_tokens'] = dh[:N_META]
    grad_x = dh[N_META:T][None]

    grads = {}
    small_names = [k for k in WEIGHTS if k in SMALL]
    (parts,) = all_gather("ag_small_grads", [_pack([gs[k].reshape(full[k].shape) for k in small_names])])
    tot = _unpack(sum_blocks("sum_small_grads", parts), [full[k].shape for k in small_names])
    for k, g in zip(small_names, tot):
        ax = SMALL[k]
        grads[k] = g if ax is None else lax.dynamic_slice_in_dim(g, me * W[k].shape[ax], W[k].shape[ax], axis=ax)

    summed, after = {}, dh
    for k in reversed(range(len(stages))):
        parts = push_wait(f"rs_wait_{k}", sent.pop(k), after)
        summed[k] = [sum_blocks(f"sum_{k}_{n}", p) for n, p in enumerate(parts)]
        after = summed[k][0]

    gg, gu, gd = [], [], []
    for i in range(DEPTH):
        for j in (0, 2):
            a, b, c_ = summed[3 * i + j]
            gg.append(a[:r_ff].T)
            gu.append(b[:r_ff].T)
            gd.append(c_[:r_ff])
    shape4 = lambda lst, ref: jnp.stack(lst).reshape(ref.shape)
    grads['ffn_w_gate'] = shape4(gg, W['ffn_w_gate'])
    grads['ffn_w_up'] = shape4(gu, W['ffn_w_up'])
    grads['ffn_w_down'] = shape4(gd, W['ffn_w_down'])
    for i in range(DEPTH):
        g = summed[3 * i + 1]
        if i % 4 == 0:
            grads['mlstm_w_in'], grads['mlstm_w_out'] = g[0][:r_mi].T[None], g[1][:r_mo][None]
        elif i % 4 == 1:
            grads['pool_w'] = g[0].reshape(W['pool_w'].shape)
        elif i % 4 == 2:
            grads['gdn_w_in'], grads['gdn_w_out'] = g[0][:r_gi].T[None], g[1][:r_go][None]
        else:
            grads['swa_w_qkv'], grads['swa_w_out'] = g[0][:r_si].T[None], g[1][:r_so][None]

    delta, new_m, new_v = {}, {}, {}
    shapes = [W[k].shape for k in small_names]
    d, nm, nv = adamw("adamw_small", _pack([W[k] for k in small_names]), _pack([grads[k] for k in small_names]),
                      _pack([M[k] for k in small_names]), _pack([V[k] for k in small_names]))
    for k, a, b, c_ in zip(small_names, _unpack(d, shapes), _unpack(nm, shapes), _unpack(nv, shapes)):
        delta[k], new_m[k], new_v[k] = a, b, c_
    for k in WEIGHTS:
        if k not in SMALL:
            two = lambda a: a.reshape(-1, a.shape[-1])
            d, nm, nv = adamw("adamw_" + k, two(W[k]), two(grads[k]), two(M[k]), two(V[k]))
            delta[k], new_m[k], new_v[k] = d.reshape(W[k].shape), nm.reshape(W[k].shape), nv.reshape(W[k].shape)

    return (loss, grad_x, *[grads[k].reshape(W[k].shape) for k in WEIGHTS], *[delta[k] for k in WEIGHTS],
            *[new_m[k] for k in WEIGHTS], *[new_v[k] for k in WEIGHTS])
```

```python
import functools

import jax
import jax.numpy as jnp
from jax import lax
from jax.experimental import pallas as pl
from jax.experimental.pallas import tpu as pltpu

F32 = jnp.float32
BF16 = jnp.bfloat16

N_DEV = 8
N_META = 16
EPS = 1e-6
DEPTH = 4
MLSTM_HEADS = 8
MLSTM_CHUNK = 64
N_POOL = 4
GDN_DK = 128
GDN_CHUNK = 64
GDN_CONV = 4
SWA_DH = 64
SWA_GROUP = 8
SWA_WINDOW = 128
ROPE_THETA = 10000.0
ADAM_LR = 0.001
ADAM_B1 = 0.9
ADAM_B2 = 0.999
ADAM_EPS = 1e-08
ADAM_WD = 0.01
ADAM_STEP = 10

LANES = 128
ROW_TILE = 128
ROW_PAD = 112
IN_PAD = 896
VMEM_LIMIT = 56 * 1024 * 1024
TOKEN_TILES = (1056, 768, 512, 384, 256, 128)
FEATURE_TILES = (512, 896, 768, 640, 384, 256, 128)
ROW_TILES = (512, 352, 256, 176, 160, 128, 112, 64, 48, 32, 16, 8)
HIGHEST = lax.Precision.HIGHEST
HIGH = lax.Precision.HIGH
MESH_ID = pl.DeviceIdType.MESH

WEIGHTS = ('meta_tokens', 'norm_w', 'ffn_w_gate', 'ffn_w_up', 'ffn_w_down', 'mlstm_w_in', 'mlstm_b_if',
           'mlstm_norm_w', 'mlstm_w_out', 'pool_w', 'pool_scale', 'gdn_w_in', 'gdn_conv_w', 'gdn_a_log',
           'gdn_dt_bias', 'gdn_norm_w', 'gdn_w_out', 'swa_w_qkv', 'swa_b_qkv', 'swa_sinks', 'swa_w_out',
           'swa_b_out', 'final_norm_w')
SMALL = {'meta_tokens': 1, 'norm_w': 2, 'mlstm_b_if': None, 'mlstm_norm_w': None, 'pool_scale': 1,
         'gdn_conv_w': 2, 'gdn_a_log': None, 'gdn_dt_bias': None, 'gdn_norm_w': None, 'swa_b_qkv': 1,
         'swa_sinks': None, 'swa_b_out': 1, 'final_norm_w': None}


def _pick(n, cands):
    for c in cands:
        if n % c == 0:
            return c
    return n


def _round_up(n, m):
    return -(-n // m) * m


def _bf(x):
    return x.astype(BF16)


def _dot(a, b, dims, precision=None):
    return lax.dot_general(a, b, (dims, ((), ())), preferred_element_type=F32, precision=precision)


def _mm(a, b):
    return _dot(_bf(a), _bf(b), ((1,), (0,)))


def _mm_nt(a, b):
    return _dot(_bf(a), _bf(b), ((1,), (1,)))


def _mm_tn(a, b):
    return _dot(_bf(a), _bf(b), ((0,), (0,)))


def _mm32(a, b):
    return _dot(a, b, ((1,), (0,)), precision=HIGHEST)


def _mm3(a, b):
    return _dot(a, b, ((1,), (0,)), precision=HIGH)


def _iota(shape, axis):
    return lax.broadcasted_iota(jnp.int32, shape, axis)


def _row2col(row, eye):
    return jnp.sum(eye * row, axis=1, keepdims=True)


def _lane_roll(shift):
    @jax.custom_vjp
    def f(x):
        return pltpu.roll(x, shift, 1)

    def fwd(x):
        return f(x), None

    def bwd(_, g):
        return (pltpu.roll(g, g.shape[1] - shift, 1),)

    f.defvjp(fwd, bwd)
    return f


def mm_call(name, mode, pairs, n_acc, epilogue, out_dtypes, extras=(), tm=None, tn=None, tk=None):
    a0, b0 = pairs[0][0], pairs[0][1]
    if mode == 'nn':
        (M, K), N = a0.shape, b0.shape[1]
    elif mode == 'nt':
        (M, K), N = a0.shape, b0.shape[0]
    else:
        (K, M), N = a0.shape, b0.shape[1]
    if mode == 'tn':
        tm = tm or _pick(M, FEATURE_TILES)
        tn = tn or (N if N <= 2048 else _pick(N, FEATURE_TILES))
        tk = tk or _pick(K, [t for t in (2112,) + TOKEN_TILES
                             if 2 * t * (tm * a0.dtype.itemsize + tn * b0.dtype.itemsize) <= VMEM_LIMIT // 2])
        dims = ((0,), (0,))
        a_spec = pl.BlockSpec((tk, tm), lambda i, j, k: (k, i))
        b_spec = pl.BlockSpec((tk, tn), lambda i, j, k: (k, j))
    else:
        tm = tm or _pick(M, TOKEN_TILES)
        tn = tn or _pick(N, FEATURE_TILES)
        tk = tk or (K if K <= 2048 else _pick(K, (2816, 2048) + FEATURE_TILES))
        a_spec = pl.BlockSpec((tm, tk), lambda i, j, k: (i, k))
        if mode == 'nn':
            dims = ((1,), (0,))
            b_spec = pl.BlockSpec((tk, tn), lambda i, j, k: (k, j))
        else:
            dims = ((1,), (1,))
            b_spec = pl.BlockSpec((tn, tk), lambda i, j, k: (j, k))
    n_pairs, n_ex, n_out = len(pairs), len(extras), len(out_dtypes)
    nk = K // tk

    def body(*refs):
        ab = refs[:2 * n_pairs]
        ex = refs[2 * n_pairs:2 * n_pairs + n_ex]
        outs = refs[2 * n_pairs + n_ex:2 * n_pairs + n_ex + n_out]
        accs = refs[2 * n_pairs + n_ex + n_out:]
        k = pl.program_id(2)

        @pl.when(k == 0)
        def _():
            for acc in accs:
                acc[...] = jnp.zeros_like(acc)

        for p, (_, _, ai) in enumerate(pairs):
            accs[ai][...] += _dot(_bf(ab[2 * p][...]), _bf(ab[2 * p + 1][...]), dims)

        @pl.when(k == nk - 1)
        def _():
            res = epilogue([acc[...] for acc in accs], [e[...] for e in ex])
            for o, v in zip(outs, res):
                o[...] = v.astype(o.dtype)

    ex_specs = [pl.BlockSpec((tm, tn), lambda i, j, k: (i, j)) if kind == 'mn'
                else pl.BlockSpec((1, tn), lambda i, j, k: (0, j)) if kind == 'n'
                else pl.BlockSpec(e.shape, lambda i, j, k: (0, 0)) for e, kind in extras]
    outs = pl.pallas_call(
        body, grid=(M // tm, N // tn, nk),
        in_specs=[a_spec, b_spec] * n_pairs + ex_specs,
        out_specs=[pl.BlockSpec((tm, tn), lambda i, j, k: (i, j)) for _ in out_dtypes],
        out_shape=[jax.ShapeDtypeStruct((M, N), dt) for dt in out_dtypes],
        scratch_shapes=[pltpu.VMEM((tm, tn), F32) for _ in range(n_acc)],
        compiler_params=pltpu.CompilerParams(dimension_semantics=("arbitrary",) * 3, vmem_limit_bytes=VMEM_LIMIT),
        name=name)(*[t for a, b, _ in pairs for t in (a, b)], *[e for e, _ in extras])
    return outs


def mm_plain(name, mode, a, b, dtype=F32, scale=None, dep=None):
    ep = (lambda accs, ex: [accs[0]]) if scale is None else (lambda accs, ex: [accs[0] * scale])
    return mm_call(name, mode, [(a, b, 0)], 1, ep, [dtype], extras=[] if dep is None else [(dep, 'dep')])[0]


def scan_fwd(name, step_fn, n_outer, n_steps, params, consts, xs, states, ys):
    n_p, n_c, n_x, n_s, n_y = len(params), len(consts), len(xs), len(states), len(ys)

    def body(*refs):
        p_refs = refs[:n_p]
        c_refs = refs[n_p:n_p + n_c]
        x_refs = refs[n_p + n_c:n_p + n_c + n_x]
        o = n_p + n_c + n_x
        y_refs = refs[o:o + n_y]
        sv_refs = refs[o + n_y:o + n_y + n_s]
        st_refs = refs[o + n_y + n_s:]
        s = pl.program_id(1)

        @pl.when(s == 0)
        def _():
            for r in st_refs:
                r[...] = jnp.zeros_like(r)

        st = tuple(r[...] for r in st_refs)
        for sv, v in zip(sv_refs, st):
            sv[...] = v
        new_st, y = step_fn(tuple(r[...] for r in p_refs), st, tuple(r[...] for r in x_refs),
                            tuple(r[...] for r in c_refs))
        for r, v in zip(y_refs, y):
            r[...] = v.astype(r.dtype)
        for r, v in zip(st_refs, new_st):
            r[...] = v

    in_specs = ([pl.BlockSpec(p[1], (lambda o, s, f=p[2]: f(o))) for p in params]
                + [pl.BlockSpec(c[1], c[2]) for c in consts]
                + [pl.BlockSpec(x[1], x[2]) for x in xs])
    out_specs = ([pl.BlockSpec(b, f) for _, _, b, f in ys]
                 + [pl.BlockSpec((None, None) + tuple(sh), (lambda o, s, n=len(sh): (o, s) + (0,) * n)) for sh, _ in states])
    out_shape = ([jax.ShapeDtypeStruct(sh, dt) for sh, dt, _, _ in ys]
                 + [jax.ShapeDtypeStruct((n_outer, n_steps) + tuple(sh), dt) for sh, dt in states])
    outs = pl.pallas_call(
        body, grid=(n_outer, n_steps), in_specs=in_specs, out_specs=out_specs, out_shape=out_shape,
        scratch_shapes=[pltpu.VMEM(tuple(sh), dt) for sh, dt in states],
        compiler_params=pltpu.CompilerParams(dimension_semantics=("arbitrary", "arbitrary"), vmem_limit_bytes=VMEM_LIMIT),
        name=name)(*[p[0] for p in params], *[c[0] for c in consts], *[x[0] for x in xs])
    return tuple(outs[:n_y]), tuple(outs[n_y:])


def scan_bwd(name, step_fn, n_outer, n_steps, params, consts, xs, states, saved, dys, glob):
    n_p, n_c, n_x, n_s, n_y = len(params), len(consts), len(xs), len(states), len(dys)
    rev = lambda f: (lambda o, s: f(o, n_steps - 1 - s))

    def body(*refs):
        p_refs = refs[:n_p]
        c_refs = refs[n_p:n_p + n_c]
        x_refs = refs[n_p + n_c:n_p + n_c + n_x]
        o = n_p + n_c + n_x
        sv_refs = refs[o:o + n_s]
        dy_refs = refs[o + n_s:o + n_s + n_y]
        o = o + n_s + n_y
        dx_refs = refs[o:o + n_x]
        dp_refs = refs[o + n_x:o + n_x + n_p]
        dst_refs = refs[o + n_x + n_p:]
        oi, s = pl.program_id(0), pl.program_id(1)

        @pl.when(s == 0)
        def _():
            for r in dst_refs:
                r[...] = jnp.zeros_like(r)

        for r, g in zip(dp_refs, glob):
            @pl.when(((s == 0) & (oi == 0)) if g else (s == 0))
            def _(r=r):
                r[...] = jnp.zeros_like(r)

        c_vals = tuple(r[...] for r in c_refs)
        f = lambda p, st, x: step_fn(p, st, x, c_vals)
        _, vjp = jax.vjp(f, tuple(r[...] for r in p_refs), tuple(r[...] for r in sv_refs), tuple(r[...] for r in x_refs))
        dp, dst, dx = vjp((tuple(r[...] for r in dst_refs), tuple(r[...] for r in dy_refs)))
        for r, v in zip(dx_refs, dx):
            r[...] = v.astype(r.dtype)
        for r, v in zip(dst_refs, dst):
            r[...] = v
        for r, v in zip(dp_refs, dp):
            r[...] += v

    gshape = lambda t: t[3] if len(t) > 3 else t[0].shape
    gidx = lambda t: t[4] if len(t) > 3 else t[2]
    in_specs = ([pl.BlockSpec(p[1], (lambda o, s, f=p[2]: f(o))) for p in params]
                + [pl.BlockSpec(c[1], rev(c[2])) for c in consts]
                + [pl.BlockSpec(x[1], rev(x[2])) for x in xs]
                + [pl.BlockSpec((None, None) + tuple(sh), (lambda o, s, n=len(sh): (o, n_steps - 1 - s) + (0,) * n)) for sh, _ in states]
                + [pl.BlockSpec(b, rev(f)) for _, b, f in dys])
    out_specs = ([pl.BlockSpec(x[1], rev(gidx(x))) for x in xs]
                 + [pl.BlockSpec(p[1], (lambda o, s, f=gidx(p): f(o))) for p in params])
    out_shape = ([jax.ShapeDtypeStruct(gshape(x), F32) for x in xs]
                 + [jax.ShapeDtypeStruct(gshape(p), F32) for p in params])
    outs = pl.pallas_call(
        body, grid=(n_outer, n_steps), in_specs=in_specs, out_specs=out_specs, out_shape=out_shape,
        scratch_shapes=[pltpu.VMEM(tuple(sh), dt) for sh, dt in states],
        compiler_params=pltpu.CompilerParams(dimension_semantics=("arbitrary", "arbitrary"), vmem_limit_bytes=VMEM_LIMIT),
        name=name)(*[p[0] for p in params], *[c[0] for c in consts], *[x[0] for x in xs], *saved,
                   *[d[0] for d in dys])
    return tuple(outs[:n_x]), tuple(outs[n_x:])


def _rows(a, rt):
    return (a, (rt, a.shape[1]), lambda o, s: (s, 0))


def _whole(a):
    return (a, a.shape, lambda o: (0,) * a.ndim)


def _rms(h, w):
    return h * lax.rsqrt(jnp.mean(h * h, axis=1, keepdims=True) + EPS) * w


def rms_fwd(name, h, w, dtype):
    TP, D = h.shape
    rt = _pick(TP, (384, 128))
    step = lambda p, st, x, c: ((), (_rms(x[0], p[0]),))
    (y,), _ = scan_fwd(name, step, 1, TP // rt, [_whole(w)], [], [_rows(h, rt)], [],
                       [((TP, D), dtype, (rt, D), lambda o, s: (s, 0))])
    return y


def rms_bwd(name, h, w, dxn, dres):
    TP, D = h.shape
    rt = _pick(TP, (384, 128))
    step = lambda p, st, x, c: ((), (_rms(x[0], p[0]), x[0]))
    (dh,), (dw,) = scan_bwd(name, step, 1, TP // rt, [_whole(w)], [], [_rows(h, rt)], [], [],
                            [_rows(dxn, rt), _rows(dres, rt)], [True])
    return dh, dw


def loss_call(h, w, tgt, mask):
    TP, D = h.shape
    rt = _pick(TP, (384, 128))

    def body(h_ref, w_ref, t_ref, m_ref, loss_ref, dh_ref, dw_ref):
        s = pl.program_id(0)

        @pl.when(s == 0)
        def _():
            loss_ref[...] = jnp.zeros_like(loss_ref)
            dw_ref[...] = jnp.zeros_like(dw_ref)

        tg, mk = t_ref[...], m_ref[...]

        def f(wv, hv):
            err = jnp.square(_rms(hv, wv) - tg) * mk
            return 0.5 * jnp.sum(jnp.sum(err, axis=1, keepdims=True), axis=0, keepdims=True) / D

        l, vjp = jax.vjp(f, w_ref[...], h_ref[...])
        dw, dh = vjp(jnp.ones((1, 1), F32))
        loss_ref[...] += l
        dw_ref[...] += dw
        dh_ref[...] = dh

    row = lambda wd: pl.BlockSpec((rt, wd), lambda s: (s, 0))
    const = lambda shape: pl.BlockSpec(shape, lambda s: (0, 0))
    return pl.pallas_call(
        body, grid=(TP // rt,), in_specs=[row(D), const((1, D)), row(D), row(1)],
        out_specs=[const((1, 1)), row(D), const((1, D))],
        out_shape=[jax.ShapeDtypeStruct((1, 1), F32), jax.ShapeDtypeStruct((TP, D), F32), jax.ShapeDtypeStruct((1, D), F32)],
        compiler_params=pltpu.CompilerParams(dimension_semantics=("arbitrary",), vmem_limit_bytes=VMEM_LIMIT),
        name="loss_head")(h, w, tgt, mask)


def colsum(name, a):
    TP, N = a.shape
    rt = _pick(TP, (384, 128))

    def body(a_ref, o_ref):
        @pl.when(pl.program_id(0) == 0)
        def _():
            o_ref[...] = jnp.zeros_like(o_ref)
        o_ref[...] += jnp.sum(a_ref[...], axis=0, keepdims=True)

    return pl.pallas_call(
        body, grid=(TP // rt,), in_specs=[pl.BlockSpec((rt, N), lambda s: (s, 0))],
        out_specs=pl.BlockSpec((1, N), lambda s: (0, 0)), out_shape=jax.ShapeDtypeStruct((1, N), F32),
        compiler_params=pltpu.CompilerParams(dimension_semantics=("arbitrary",)), name=name)(a)


def ffn_fwd(tag, h, nw, wg_t, wu_t, wd):
    xn = rms_fwd(tag + "_norm", h, nw, BF16)

    def ep_up(accs, ex):
        g, u = accs
        return [g, u, jax.nn.silu(g) * u]

    g, u, a = mm_call(tag + "_up", 'nt', [(xn, wg_t, 0), (xn, wu_t, 1)], 2, ep_up, [BF16, BF16, BF16])
    h2 = mm_call(tag + "_down", 'nn', [(a, wd, 0)], 1, lambda accs, ex: [ex[0] + 0.5 * accs[0]], [F32],
                 extras=[(h, 'mn')])[0]
    return h2, (h, xn, g, u, a)


def ffn_bwd(tag, dh2, res, nw, wg_t, wu_t, wd, dep):
    h, xn, g, u, a = res

    def ep_act(accs, ex):
        da = 0.5 * accs[0]
        gv, uv = ex[0].astype(F32), ex[1].astype(F32)
        sg = jax.nn.sigmoid(gv)
        return [da * uv * (sg * (1.0 + gv * (1.0 - sg))), da * (gv * sg)]

    dg, du = mm_call(tag + "_dact", 'nt', [(dh2, wd, 0)], 1, ep_act, [BF16, BF16],
                     extras=[(g, 'mn'), (u, 'mn'), (dep, 'dep')])
    dwd = mm_plain(tag + "_dwd", 'tn', a, dh2, BF16, scale=0.5)
    dxn = mm_call(tag + "_dxn", 'nn', [(dg, wg_t, 0), (du, wu_t, 0)], 1, lambda accs, ex: [accs[0]], [F32])[0]
    dwg_t = mm_plain(tag + "_dwg", 'tn', dg, xn, BF16)
    dwu_t = mm_plain(tag + "_dwu", 'tn', du, xn, BF16)
    dh, dnw = rms_bwd(tag + "_dnorm", h, nw, dxn, dh2)
    return dh, dnw, dwg_t, dwu_t, dwd


def _mlstm_step(params, state, xs, consts, *, dk):
    bif, nw = params
    c_st, n_st, m_st = state
    q, k, v, og, gr = xs
    C = MLSTM_CHUNK
    R = q.shape[0]
    ri, ci = _iota((C, C), 0), _iota((C, C), 1)
    eye = (ri == ci).astype(F32)
    causal = ci <= ri
    upper = (ri <= ci).astype(F32)
    outs = []
    for j in range(R // C):
        sl = slice(j * C, (j + 1) * C)
        qj, kj, vj = q[sl] * (dk ** -0.5), k[sl], v[sl]
        li = gr[0:1, sl] + bif[0:1, 0:1]
        lf = jax.nn.log_sigmoid(gr[1:2, sl] + bif[1:2, 0:1])
        b_row = _mm32(lf, upper)
        b_col = _row2col(b_row, eye)
        log_w = jnp.where(causal, b_col - b_row + li, -jnp.inf)
        log_init = b_col + m_st
        m_t = lax.stop_gradient(jnp.maximum(log_init, jnp.max(log_w, axis=1, keepdims=True)))
        w = jnp.exp(log_w - m_t)
        w_init = jnp.exp(log_init - m_t)
        qk = _mm_nt(qj, kj) * w
        num = w_init * _mm(qj, c_st) + _mm(qk, vj)
        den = w_init * jnp.sum(qj * n_st, axis=1, keepdims=True) + jnp.sum(qk, axis=1, keepdims=True)
        h = num / jnp.maximum(jnp.abs(den), jnp.exp(-m_t))
        b_last = b_row[:, C - 1:C]
        log_end_init = b_last + m_st
        log_end = b_last - b_row + li
        m_new = lax.stop_gradient(jnp.maximum(log_end_init, jnp.max(log_end, axis=1, keepdims=True)))
        a_init = jnp.exp(log_end_init - m_new)
        ka = kj * _row2col(jnp.exp(log_end - m_new), eye)
        c_st = a_init * c_st + _mm_tn(ka, vj)
        n_st = a_init * n_st + jnp.sum(ka, axis=0, keepdims=True)
        m_st = m_new
        hn = h * lax.rsqrt(jnp.mean(h * h, axis=1, keepdims=True) + EPS)
        outs.append(hn * nw * jax.nn.sigmoid(og[sl]))
    return (c_st, n_st, m_st), (jnp.concatenate(outs, axis=0),)


def _mlstm_ops(p, gr, bif, nw):
    H = MLSTM_HEADS
    TP = p.shape[0]
    dv = nw.shape[1] // H
    dk = dv // 2
    R = ROW_TILE
    step = functools.partial(_mlstm_step, dk=dk)
    params = [(bif, (None, 2, LANES), lambda o: (o, 0, 0)), (nw, (1, dv), lambda o: (0, o))]
    col = lambda w, off: (p, (R, w), (lambda o, s: (s, off + o)), (TP, H * w), (lambda o, s: (s, o)))
    xs = [col(dk, 0), col(dk, H), col(dv, H), col(dv, 2 * H), (gr, (None, 2, R), lambda o, s: (o, 0, s))]
    states = [((dk, dv), F32), ((1, dk), F32), ((1, 1), F32)]
    ys = [((TP, H * dv), F32, (R, dv), lambda o, s: (s, o))]
    return step, H, TP // R, params, xs, states, ys


def mlstm_fwd(h, nw1, w_in_t, w_out, b_if, norm_w):
    H = MLSTM_HEADS
    D = h.shape[1]
    u = rms_fwd("mlstm_norm", h, nw1, BF16)
    p = mm_plain("mlstm_in", 'nt', u, w_in_t)
    gr = p[:, 3 * D:3 * D + 2 * H].T.reshape(2, H, -1).transpose(1, 0, 2)
    bif = jnp.broadcast_to(b_if.reshape(2, H).T[:, :, None], (H, 2, LANES))
    step, _, n, params, xs, states, ys = _mlstm_ops(p, gr, bif, norm_w)
    (act,), saved = scan_fwd("mlstm_core", step, H, n, params, [], xs, states, ys)
    h2 = mm_call("mlstm_out", 'nn', [(act, w_out, 0)], 1, lambda accs, ex: [ex[0] + accs[0]], [F32], extras=[(h, 'mn')])[0]
    return h2, (h, u, p, gr, bif, saved, act)


def mlstm_bwd(dh2, res, nw1, w_in_t, w_out, norm_w, dep):
    h, u, p, gr, bif, saved, act = res
    H = MLSTM_HEADS
    TP = h.shape[0]
    dact = mm_plain("mlstm_dact", 'nt', dh2, w_out, dep=dep)
    dw_out = mm_plain("mlstm_dwout", 'tn', act, dh2, BF16)
    step, _, n, params, xs, states, ys = _mlstm_ops(p, gr, bif, norm_w)
    (dq, dk, dv, dog, dgr), (dbif, dnorm) = scan_bwd("mlstm_core_bwd", step, H, n, params, [], xs, states, saved,
                                                     [(dact, ys[0][2], ys[0][3])], [False, False])
    dgates = dgr.transpose(1, 0, 2).reshape(2 * H, TP).T
    pad = p.shape[1] - (dq.shape[1] + dk.shape[1] + dv.shape[1] + dog.shape[1] + 2 * H)
    dp = jnp.concatenate([dq, dk, dv, dog, dgates, jnp.zeros((TP, pad), F32)], axis=1)
    du = mm_plain("mlstm_du", 'nn', dp, w_in_t)
    dw_in_t = mm_plain("mlstm_dwin", 'tn', dp, u, BF16)
    dh, dnw1 = rms_bwd("mlstm_dnorm", h, nw1, du, dh2)
    db_if = dbif[:, :, 0].T.reshape(1, 2 * H)
    return dh, dict(nw=dnw1, w_in_t=dw_in_t, w_out=dw_out, b_if=db_if, norm_w=dnorm)


def _pool_step(params, state, xs, consts):
    w, scale = params
    (prev,) = state
    u, h = xs
    pos, win = consts
    R = u.shape[0]
    wn = win[0:1, 0:1]
    ext = jnp.concatenate([prev, u], axis=0)
    lag = _iota((R, 2 * R), 0) + R - _iota((R, 2 * R), 1)
    band = ((lag >= 0) & (lag < wn)).astype(F32)
    wsum = _mm32(band, ext)
    cnt = jnp.minimum(pos + 1, wn).astype(F32)
    pooled = wsum / cnt - u
    return (u,), (h + _mm(pooled, w) * scale,)


def _pool_ops(u, h, w, scale, pos):
    TP, D = u.shape
    G = D // N_POOL
    R = ROW_TILE
    win = jnp.broadcast_to(jnp.array([2 << g for g in range(N_POOL)], jnp.int32)[:, None, None], (N_POOL, 1, LANES))
    params = [(w, (None, G, G), lambda o: (o, 0, 0)), (scale, (1, G), lambda o: (0, o))]
    consts = [(pos, (R, 1), lambda o, s: (s, 0)), (win, (None, 1, LANES), lambda o, s: (o, 0, 0))]
    grp = lambda a: (a, (R, G), lambda o, s: (s, o))
    return N_POOL, TP // R, params, consts, [grp(u), grp(h)], [((R, G), F32)], [((TP, D), F32, (R, G), lambda o, s: (s, o))]


def pool_fwd(h, nw1, w, scale, pos):
    u = rms_fwd("pool_norm", h, nw1, F32)
    no, n, params, consts, xs, states, ys = _pool_ops(u, h, w, scale, pos)
    (h2,), saved = scan_fwd("pool_core", _pool_step, no, n, params, consts, xs, states, ys)
    return h2, (h, u, saved)


def pool_bwd(dh2, res, nw1, w, scale, pos):
    h, u, saved = res
    no, n, params, consts, xs, states, ys = _pool_ops(u, h, w, scale, pos)
    (du, dres), (dw, dscale) = scan_bwd("pool_core_bwd", _pool_step, no, n, params, consts, xs, states, saved,
                                        [(dh2, ys[0][2], ys[0][3])], [False, False])
    dh, dnw1 = rms_bwd("pool_dnorm", h, nw1, du, dres)
    return dh, dict(nw=dnw1, w=dw, scale=dscale)


def _unit_lower_inverse(low, width):
    n = low.shape[0]
    ri, ci = _iota((n, n), 0), _iota((n, n), 1)
    inv = (ri == ci).astype(F32)
    b = 1
    while b < width:
        blk = 2 * b
        sh = blk.bit_length() - 1
        off = jnp.where(((ri >> sh) == (ci >> sh)) & ((ri & (blk - 1)) >= b) & ((ci & (blk - 1)) < b), low, 0.0)
        inv = inv - (off if b == 1 else _mm(_mm(inv, off), inv))
        b = blk
    return inv


def _conv_silu(prev, x, w):
    R = x.shape[0]
    ext = jnp.concatenate([prev, x], axis=0)
    y = sum(w[j:j + 1, :] * ext[8 - (GDN_CONV - 1) + j:8 - (GDN_CONV - 1) + j + R] for j in range(GDN_CONV))
    return jax.nn.silu(y)


def _gdn_step(params, state, xs, consts):
    cwq, cwk, cwv, ad, gnw = params
    s_cat, pq, pk, pv = state
    q, k, v, z, gb, gbc = xs
    C = GDN_CHUNK
    R, dk = q.shape
    dv = v.shape[1] // 2
    ri, ci = _iota((R, R), 0), _iota((R, R), 1)
    sh = C.bit_length() - 1
    same = (ri >> sh) == (ci >> sh)
    causal, strict = same & (ci <= ri), same & (ci < ri)
    qc, kc, vc = _conv_silu(pq, q, cwq), _conv_silu(pk, k, cwk), _conv_silu(pv, v, cwv)
    qn = qc * lax.rsqrt(jnp.sum(qc * qc, axis=1, keepdims=True) + EPS) * (dk ** -0.5)
    kn = kc * lax.rsqrt(jnp.sum(kc * kc, axis=1, keepdims=True) + EPS)
    kk, qk = _mm_nt(kn, kn), _mm_nt(qn, kn)
    g_rows = -jnp.exp(ad[0:2, 0:1]) * jax.nn.softplus(gb[2:4, :] + ad[2:4, 0:1])
    gc_rows = _mm3(g_rows, (same & (ri <= ci)).astype(F32))
    g_cols = jnp.concatenate([-jnp.exp(ad[e:e + 1, 0:1]) * jax.nn.softplus(gbc[:, 2 + e:3 + e] + ad[2 + e:3 + e, 0:1])
                              for e in range(2)], axis=1)
    gc_cols = _mm3(causal.astype(F32), g_cols)
    beta_cols = jax.nn.sigmoid(gbc[:, 0:2])
    lows, rhss, attns, qgs = [], [], [], []
    for e in range(2):
        gcc, bc = gc_cols[:, e:e + 1], beta_cols[:, e:e + 1]
        decay = jnp.exp(jnp.where(causal, gcc - gc_rows[e:e + 1, :], -jnp.inf))
        lows.append(jnp.where(strict, kk * bc * decay, 0.0))
        attns.append(qk * decay)
        eg = jnp.exp(gcc)
        rhss.append(jnp.concatenate([vc[:, e * dv:(e + 1) * dv] * bc, kn * (bc * eg)], axis=1))
        qgs.append(qn * eg)
    zero = jnp.zeros((R, R), F32)
    big = jnp.concatenate([jnp.concatenate([lows[0], zero], axis=1), jnp.concatenate([zero, lows[1]], axis=1)], axis=0)
    sol = _mm3(_unit_lower_inverse(big, C), jnp.concatenate(rhss, axis=0))
    outs = [[], []]
    n_chunks = R // C
    for j in range(n_chunks):
        sl = slice(j * C, (j + 1) * C)
        lhs = jnp.concatenate([sol[sl, dv:], sol[R + j * C:R + (j + 1) * C, dv:], qgs[0][sl], qgs[1][sl]], axis=0)
        pr = _mm(lhs, s_cat)
        v_new = [sol[e * R + j * C:e * R + (j + 1) * C, :dv] - pr[e * C:(e + 1) * C, e * dv:(e + 1) * dv] for e in range(2)]
        v_lanes = jnp.concatenate(v_new, axis=1)
        pad = jnp.zeros((C, 2 * dv), F32)
        v_rows = jnp.concatenate([v_lanes if i == j else pad for i in range(n_chunks)], axis=0)
        av = _mm(jnp.concatenate([attns[0][sl], attns[1][sl]], axis=0), v_rows)
        g_last = [gc_rows[e:e + 1, (j + 1) * C - 1:(j + 1) * C] for e in range(2)]
        kg = jnp.concatenate([kn[sl] * jnp.exp(g_last[e] - gc_cols[sl, e:e + 1]) for e in range(2)], axis=1)
        kv = _mm_tn(kg, v_lanes)
        s_cat = jnp.concatenate([jnp.exp(g_last[e]) * s_cat[:, e * dv:(e + 1) * dv]
                                 + kv[e * dk:(e + 1) * dk, e * dv:(e + 1) * dv] for e in range(2)], axis=1)
        for e in range(2):
            o = pr[(2 + e) * C:(3 + e) * C, e * dv:(e + 1) * dv] + av[e * C:(e + 1) * C, e * dv:(e + 1) * dv]
            on = o * lax.rsqrt(jnp.mean(o * o, axis=1, keepdims=True) + EPS) * gnw
            outs[e].append(on * jax.nn.silu(z[sl, e * dv:(e + 1) * dv]))
    out = jnp.concatenate([jnp.concatenate(outs[0], axis=0), jnp.concatenate(outs[1], axis=0)], axis=1)
    return (s_cat, q[R - 8:], k[R - 8:], v[R - 8:]), (out,)


def _gdn_ops(p, gb, gbc, conv_w, ad, gnw):
    TP = p.shape[0]
    dk = GDN_DK
    nqk = gb.shape[0]
    R = ROW_TILE
    cw = lambda w, off: (conv_w, (GDN_CONV, w), (lambda o: (0, off + o)), (GDN_CONV, nqk * w), (lambda o: (0, o)))
    params = [cw(dk, 0), cw(dk, nqk), cw(2 * dk, nqk), (ad, (None, 4, LANES), lambda o: (o, 0, 0)), _whole(gnw)]
    col = lambda w, off: (p, (R, w), (lambda o, s: (s, off + o)), (TP, nqk * w), (lambda o, s: (s, o)))
    xs = [col(dk, 0), col(dk, nqk), col(2 * dk, nqk), col(2 * dk, 2 * nqk), (gb, (None, 4, R), lambda o, s: (o, 0, s)),
          (gbc, (None, R, 4), lambda o, s: (o, s, 0))]
    states = [((dk, 2 * dk), F32), ((8, dk), F32), ((8, dk), F32), ((8, 2 * dk), F32)]
    ys = [((TP, 2 * nqk * dk), F32, (R, 2 * dk), lambda o, s: (s, o))]
    return nqk, TP // R, params, xs, states, ys


def gdn_fwd(h, nw1, w_in_t, w_out, conv_w, a_log, dt_bias, gnw):
    D = h.shape[1]
    nqk = D // GDN_DK
    u = rms_fwd("gdn_norm", h, nw1, BF16)
    p = mm_plain("gdn_in", 'nt', u, w_in_t)
    gates = p[:, 6 * D:6 * D + 4 * nqk]
    gb = gates.T.reshape(2, nqk, 2, -1).transpose(1, 0, 2, 3).reshape(nqk, 4, -1)
    gbc = gates.reshape(-1, 2, nqk, 2).transpose(2, 0, 1, 3).reshape(nqk, -1, 4)
    ad = jnp.concatenate([a_log.reshape(nqk, 2), dt_bias.reshape(nqk, 2)], axis=1)
    ad = jnp.broadcast_to(ad[:, :, None], (nqk, 4, LANES))
    no, n, params, xs, states, ys = _gdn_ops(p, gb, gbc, conv_w, ad, gnw)
    (act,), saved = scan_fwd("gdn_core", _gdn_step, no, n, params, [], xs, states, ys)
    h2 = mm_call("gdn_out", 'nn', [(act, w_out, 0)], 1, lambda accs, ex: [ex[0] + accs[0]], [F32], extras=[(h, 'mn')])[0]
    return h2, (h, u, p, gb, gbc, ad, saved, act)


def gdn_bwd(dh2, res, nw1, w_in_t, w_out, conv_w, gnw, dep):
    h, u, p, gb, gbc, ad, saved, act = res
    TP, D = h.shape
    nqk = D // GDN_DK
    dact = mm_plain("gdn_dact", 'nt', dh2, w_out, dep=dep)
    dw_out = mm_plain("gdn_dwout", 'tn', act, dh2, BF16)
    no, n, params, xs, states, ys = _gdn_ops(p, gb, gbc, conv_w, ad, gnw)
    (dq, dk, dv, dz, dgb, dgbc), (dcq, dck, dcv, dad, dgnw) = scan_bwd(
        "gdn_core_bwd", _gdn_step, no, n, params, [], xs, states, saved, [(dact, ys[0][2], ys[0][3])],
        [False, False, False, False, True])
    dgates = (dgb.reshape(nqk, 2, 2, TP).transpose(1, 0, 2, 3).reshape(4 * nqk, TP).T
              + dgbc.reshape(nqk, TP, 2, 2).transpose(1, 2, 0, 3).reshape(TP, 4 * nqk))
    pad = p.shape[1] - (6 * D + 4 * nqk)
    dp = jnp.concatenate([dq, dk, dv, dz, dgates, jnp.zeros((TP, pad), F32)], axis=1)
    du = mm_plain("gdn_du", 'nn', dp, w_in_t)
    dw_in_t = mm_plain("gdn_dwin", 'tn', dp, u, BF16)
    dh, dnw1 = rms_bwd("gdn_dnorm", h, nw1, du, dh2)
    dad = dad[:, :, 0]
    return dh, dict(nw=dnw1, w_in_t=dw_in_t, w_out=dw_out, conv_w=jnp.concatenate([dcq, dck, dcv], axis=1),
                    a_log=dad[:, :2].reshape(1, 2 * nqk), dt_bias=dad[:, 2:].reshape(1, 2 * nqk), norm_w=dgnw)


def _rope(x, cos, sin):
    W = x.shape[1]
    first_half = (_iota(x.shape, 1) & (SWA_DH - 1)) < SWA_DH // 2
    rot = jnp.where(first_half, -_lane_roll(W - SWA_DH // 2)(x), _lane_roll(SWA_DH // 2)(x))
    return x * jnp.tile(cos, (1, W // LANES)) + rot * jnp.tile(sin, (1, W // LANES))


def _swa_step(params, state, xs, consts):
    (sinks,) = params
    kprev, vprev = state
    q, k, v = xs
    cos, sin, pos = consts
    R = q.shape[0]
    hkv = k.shape[1] // SWA_DH
    G = SWA_GROUP
    qr, kr = _rope(q, cos, sin), _rope(k, cos, sin)
    k2, v2 = jnp.concatenate([kprev, kr], axis=0), jnp.concatenate([vprev, v], axis=0)
    lane = _iota((R, LANES), 1)
    qpos = jnp.concatenate([pos] * G, axis=0)
    kpos = pos[0:1, 0:1] - R + _iota((1, 2 * R), 1)
    mask = (kpos <= qpos) & (qpos - kpos < SWA_WINDOW) & (kpos >= 0)
    sel_r, sel_c = _iota((hkv * SWA_DH, LANES), 0), _iota((hkv * SWA_DH, LANES), 1)
    out = []
    for hh in range(hkv):
        sel = _bf((sel_r == hh * SWA_DH + (sel_c & (SWA_DH - 1))).astype(F32))
        kd, vd = _mm(k2, sel), _mm(v2, sel)
        q8 = []
        for i in range(G // 2):
            qb = qr[:, (hh * G // 2 + i) * LANES:(hh * G // 2 + i + 1) * LANES]
            q8 += [jnp.where(lane < SWA_DH, qb, 0.0), jnp.where(lane >= SWA_DH, qb, 0.0)]
        s = _mm_nt(jnp.concatenate(q8, axis=0), kd) * (SWA_DH ** -0.5)
        s = jnp.where(mask, s, -jnp.inf)
        sink = jnp.concatenate([jnp.broadcast_to(sinks[0:1, hh * G + g:hh * G + g + 1], (R, 1)) for g in range(G)], axis=0)
        m = lax.stop_gradient(jnp.maximum(jnp.max(s, axis=1, keepdims=True), sink))
        e = jnp.exp(s - m)
        prob = e / (jnp.sum(e, axis=1, keepdims=True) + jnp.exp(sink - m))
        o8 = _mm(prob, vd)
        for i in range(G // 2):
            out.append(jnp.where(lane < SWA_DH, o8[2 * i * R:(2 * i + 1) * R], o8[(2 * i + 1) * R:(2 * i + 2) * R]))
    return (kr, v), (jnp.concatenate(out, axis=1),)


def _swa_ops(p, sinks, cos, sin, pos):
    TP = p.shape[0]
    R = ROW_TILE
    hq = sinks.shape[1]
    wq, wkv = hq * SWA_DH, hq // SWA_GROUP * SWA_DH
    nb = wq // wkv
    xs = [(p, (R, wq), (lambda o, s: (s, 0)), (TP, wq), (lambda o, s: (s, 0))),
          (p, (R, wkv), (lambda o, s: (s, nb)), (TP, wkv), (lambda o, s: (s, 0))),
          (p, (R, wkv), (lambda o, s: (s, nb + 1)), (TP, wkv), (lambda o, s: (s, 0)))]
    consts = [_rows(cos, R), _rows(sin, R), _rows(pos, R)]
    states = [((R, wkv), F32), ((R, wkv), F32)]
    ys = [((TP, wq), F32, (R, wq), lambda o, s: (s, 0))]
    return TP // R, [_whole(sinks)], consts, xs, states, ys


def swa_fwd(h, nw1, w_qkv_t, b_qkv, w_out, b_out, sinks, cos, sin, pos):
    u = rms_fwd("swa_norm", h, nw1, BF16)
    p = mm_call("swa_in", 'nt', [(u, w_qkv_t, 0)], 1, lambda accs, ex: [accs[0] + ex[0]], [F32], extras=[(b_qkv, 'n')])[0]
    n, params, consts, xs, states, ys = _swa_ops(p, sinks, cos, sin, pos)
    (act,), saved = scan_fwd("swa_core", _swa_step, 1, n, params, consts, xs, states, ys)
    h2 = mm_call("swa_out", 'nn', [(act, w_out, 0)], 1, lambda accs, ex: [ex[0] + accs[0] + ex[1]], [F32],
                 extras=[(h, 'mn'), (b_out, 'n')])[0]
    return h2, (h, u, p, saved, act)


def swa_bwd(dh2, res, nw1, w_qkv_t, w_out, sinks, cos, sin, pos, dep):
    h, u, p, saved, act = res
    dact = mm_plain("swa_dact", 'nt', dh2, w_out, dep=dep)
    dw_out = mm_plain("swa_dwout", 'tn', act, dh2, BF16)
    db_out = colsum("swa_dbout", dh2)
    n, params, consts, xs, states, ys = _swa_ops(p, sinks, cos, sin, pos)
    (dq, dk, dv), (dsinks,) = scan_bwd("swa_core_bwd", _swa_step, 1, n, params, consts, xs, states, saved,
                                       [(dact, ys[0][2], ys[0][3])], [True])
    dp = jnp.concatenate([dq, dk, dv], axis=1)
    db_qkv = colsum("swa_dbqkv", dp)
    du = mm_plain("swa_du", 'nn', dp, w_qkv_t)
    dw_qkv_t = mm_plain("swa_dwqkv", 'tn', dp, u, BF16)
    dh, dnw1 = rms_bwd("swa_dnorm", h, nw1, du, dh2)
    return dh, dict(nw=dnw1, w_qkv_t=dw_qkv_t, w_out=dw_out, b_qkv=db_qkv, b_out=db_out, sinks=dsinks)


def _dev_index(dev):
    return 4 * dev[0] + 2 * dev[1] + dev[2]


def all_gather(name, shards):
    n = len(shards)

    def body(*refs):
        x_refs, out_refs = refs[:n], refs[n:2 * n]
        send_sems, recv_sems, local_sem = refs[2 * n:]
        x, y, c = lax.axis_index("x"), lax.axis_index("y"), lax.axis_index("c")
        me, sibling = (x, y, c), (x, y, 1 - c)
        chips = [(1 - x, y), (x, 1 - y), (1 - x, 1 - y)]

        def copy(a, k, block, to, src=None):
            dst = out_refs[a].at[_dev_index(block)]
            return pltpu.make_async_remote_copy(src_ref=dst if src is None else src, dst_ref=dst,
                                                send_sem=send_sems.at[a, k], recv_sem=recv_sems.at[a, k],
                                                device_id=to, device_id_type=MESH_ID)

        mine = [pltpu.make_async_copy(x_refs[a], out_refs[a].at[_dev_index(me)], local_sem.at[a]) for a in range(n)]
        first, passed = [], []
        for a in range(n):
            mine[a].start()
            first += [copy(a, 0, me, sibling, src=x_refs[a])]
            first += [copy(a, 1 + j, me, (*chip, c), src=x_refs[a]) for j, chip in enumerate(chips)]
        for cp in first:
            cp.start()
        for a in range(n):
            for j, chip in enumerate(chips):
                copy(a, 1 + j, (*chip, c), me).wait_recv()
                fwd = copy(a, 4 + j, (*chip, c), sibling)
                fwd.start()
                passed.append(fwd)
        for a in range(n):
            copy(a, 0, sibling, me).wait_recv()
            for j, chip in enumerate(chips):
                copy(a, 4 + j, (*chip, 1 - c), me).wait_recv()
        for cp in first + passed:
            cp.wait_send()
        for cp in mine:
            cp.wait()

    any_spec = pl.BlockSpec(memory_space=pl.ANY)
    return pl.pallas_call(
        body, in_specs=[any_spec] * n, out_specs=[any_spec] * n,
        out_shape=[jax.ShapeDtypeStruct((N_DEV,) + s.shape, s.dtype) for s in shards],
        scratch_shapes=[pltpu.SemaphoreType.DMA((n, 7)), pltpu.SemaphoreType.DMA((n, 7)), pltpu.SemaphoreType.DMA((n,))],
        name=name)(*shards)


def exchange_blocks(name, fulls):
    n = len(fulls)

    def body(*refs):
        g_refs, out_refs = refs[:n], refs[n:2 * n]
        send_sems, recv_sems, local_sem = refs[2 * n:]
        x, y, c = lax.axis_index("x"), lax.axis_index("y"), lax.axis_index("c")
        me = (x, y, c)
        peers = [(1 - x if r & 4 else x, 1 - y if r & 2 else y, 1 - c if r & 1 else c) for r in range(1, N_DEV)]

        def copy(a, k, peer):
            return pltpu.make_async_remote_copy(src_ref=g_refs[a].at[_dev_index(peer)], dst_ref=out_refs[a].at[_dev_index(me)],
                                                send_sem=send_sems.at[a, k], recv_sem=recv_sems.at[a, k],
                                                device_id=peer, device_id_type=MESH_ID)

        mine = [pltpu.make_async_copy(g_refs[a].at[_dev_index(me)], out_refs[a].at[_dev_index(me)], local_sem.at[a])
                for a in range(n)]
        sends = [copy(a, k, peer) for a in range(n) for k, peer in enumerate(peers)]
        for cp in mine + sends:
            cp.start()
        for a in range(n):
            for k, peer in enumerate(peers):
                pltpu.make_async_remote_copy(src_ref=g_refs[a].at[_dev_index(peer)], dst_ref=out_refs[a].at[_dev_index(peer)],
                                             send_sem=send_sems.at[a, k], recv_sem=recv_sems.at[a, k],
                                             device_id=peer, device_id_type=MESH_ID).wait_recv()
        for cp in sends:
            cp.wait_send()
        for cp in mine:
            cp.wait()

    any_spec = pl.BlockSpec(memory_space=pl.ANY)
    return pl.pallas_call(
        body, in_specs=[any_spec] * n, out_specs=[any_spec] * n,
        out_shape=[jax.ShapeDtypeStruct(g.shape, g.dtype) for g in fulls],
        scratch_shapes=[pltpu.SemaphoreType.DMA((n, 7)), pltpu.SemaphoreType.DMA((n, 7)), pltpu.SemaphoreType.DMA((n,))],
        name=name)(*fulls)


def _peers_of(x, y, c):
    return [(1 - x if r & 4 else x, 1 - y if r & 2 else y, 1 - c if r & 1 else c) for r in range(1, N_DEV)]


def push_start(name, srcs, lands, after, gather):
    n = len(srcs)

    def body(*refs):
        src_refs, land_refs = refs[:n], refs[n:2 * n]
        send_sems, recv_sems = refs[2 * n + 1], refs[2 * n + 2]
        token = refs[-1]
        x, y, c = lax.axis_index("x"), lax.axis_index("y"), lax.axis_index("c")
        me = (x, y, c)
        for a in range(n):
            for k, peer in enumerate(_peers_of(x, y, c)):
                pltpu.make_async_remote_copy(
                    src_ref=src_refs[a] if gather else src_refs[a].at[_dev_index(peer)],
                    dst_ref=land_refs[a].at[_dev_index(me)], send_sem=send_sems.at[a * (N_DEV - 1) + k],
                    recv_sem=recv_sems.at[a * (N_DEV - 1) + k],
                    device_id=peer, device_id_type=MESH_ID).start()
        token[...] = jnp.zeros_like(token)

    hbm = pl.BlockSpec(memory_space=pltpu.HBM)
    sem = pl.BlockSpec(memory_space=pltpu.SEMAPHORE)
    outs = pl.pallas_call(
        body, name=name,
        out_shape=(pltpu.SemaphoreType.DMA((n * (N_DEV - 1),)), pltpu.SemaphoreType.DMA((n * (N_DEV - 1),)),
                   *[pltpu.HBM(s.shape, s.dtype) for s in srcs], *[pltpu.HBM(l.shape, l.dtype) for l in lands],
                   jax.ShapeDtypeStruct((8, LANES), F32)),
        in_specs=[hbm] * (2 * n) + [pl.BlockSpec(memory_space=pl.ANY)],
        out_specs=(sem, sem, *[hbm] * (2 * n), pl.BlockSpec(memory_space=pltpu.VMEM)),
        input_output_aliases={i: 2 + i for i in range(2 * n)},
        compiler_params=pltpu.CompilerParams(has_side_effects=pltpu.SideEffectType.DATAFLOW_SIDE_EFFECTING),
    )(*[pltpu.with_memory_space_constraint(s, pltpu.HBM) for s in srcs],
      *[pltpu.with_memory_space_constraint(l, pltpu.HBM) for l in lands], after)
    return (outs[0], outs[1], outs[2:2 + n], outs[2 + n:2 + 2 * n], gather), outs[-1]


def push_wait(name, handle, after):
    send_sems, recv_sems, srcs, lands, gather = handle
    n = len(srcs)

    def body(*refs):
        src_refs, land_refs = refs[:n], refs[n:2 * n]
        send_sem_ref, recv_sem_ref = refs[2 * n], refs[2 * n + 1]
        x, y, c = lax.axis_index("x"), lax.axis_index("y"), lax.axis_index("c")
        for a in range(n):
            for k, peer in enumerate(_peers_of(x, y, c)):
                cp = pltpu.make_async_remote_copy(
                    src_ref=src_refs[a] if gather else src_refs[a].at[_dev_index(peer)],
                    dst_ref=land_refs[a].at[_dev_index(peer)], send_sem=send_sem_ref.at[a * (N_DEV - 1) + k],
                    recv_sem=recv_sem_ref.at[a * (N_DEV - 1) + k],
                    device_id=peer, device_id_type=MESH_ID)
                cp.wait_send()
                cp.wait_recv()

    hbm = pl.BlockSpec(memory_space=pltpu.HBM)
    sem = pl.BlockSpec(memory_space=pltpu.SEMAPHORE)
    outs = pl.pallas_call(
        body, name=name,
        out_shape=(*[pltpu.HBM(s.shape, s.dtype) for s in srcs], *[pltpu.HBM(l.shape, l.dtype) for l in lands]),
        in_specs=[hbm] * (2 * n) + [sem, sem, pl.BlockSpec(memory_space=pl.ANY)],
        out_specs=tuple([hbm] * (2 * n)),
        input_output_aliases={i: i for i in range(2 * n)},
        compiler_params=pltpu.CompilerParams(has_side_effects=pltpu.SideEffectType.DATAFLOW_SIDE_EFFECTING),
    )(*srcs, *lands, send_sems, recv_sems, after)
    return outs[n:]


def _own_block(block, me):
    land = lax.empty((N_DEV,) + block.shape, block.dtype)
    return lax.dynamic_update_slice(land, block[None], (me,) + (0,) * block.ndim)


def sum_blocks(name, parts):
    _, r, c = parts.shape
    tr = _pick(r, ROW_TILES)

    def body(p_ref, o_ref):
        acc = p_ref[0].astype(F32)
        for b in range(1, N_DEV):
            acc = acc + p_ref[b].astype(F32)
        o_ref[...] = acc

    return pl.pallas_call(
        body, grid=(r // tr,), in_specs=[pl.BlockSpec((N_DEV, tr, c), lambda i: (0, i, 0))],
        out_specs=pl.BlockSpec((tr, c), lambda i: (i, 0)), out_shape=jax.ShapeDtypeStruct((r, c), F32),
        compiler_params=pltpu.CompilerParams(dimension_semantics=("arbitrary",), vmem_limit_bytes=VMEM_LIMIT),
        name=name)(parts)


def adamw(name, w, g, m, v):
    r, c = w.shape
    tr = _pick(r, [t for t in ROW_TILES if t * c * 4 * 7 * 2 <= VMEM_LIMIT // 2])

    def body(w_ref, g_ref, m_ref, v_ref, d_ref, nm_ref, nv_ref):
        gv = g_ref[...]
        nm = ADAM_B1 * m_ref[...] + (1.0 - ADAM_B1) * gv
        nv = ADAM_B2 * v_ref[...] + (1.0 - ADAM_B2) * jnp.square(gv)
        m_hat = nm / (1.0 - ADAM_B1 ** ADAM_STEP)
        v_hat = nv / (1.0 - ADAM_B2 ** ADAM_STEP)
        d_ref[...] = -ADAM_LR * (m_hat / (jnp.sqrt(v_hat) + ADAM_EPS) + ADAM_WD * w_ref[...])
        nm_ref[...] = nm
        nv_ref[...] = nv

    spec = pl.BlockSpec((tr, c), lambda i: (i, 0))
    return pl.pallas_call(
        body, grid=(r // tr,), in_specs=[spec] * 4, out_specs=[spec] * 3,
        out_shape=[jax.ShapeDtypeStruct((r, c), F32)] * 3,
        compiler_params=pltpu.CompilerParams(dimension_semantics=("arbitrary",), vmem_limit_bytes=VMEM_LIMIT),
        name=name)(w, g, m, v)


def _pack(arrays):
    flat = jnp.concatenate([a.reshape(-1) for a in arrays])
    n = _round_up(flat.shape[0], 8 * LANES)
    return jnp.pad(flat, (0, n - flat.shape[0])).reshape(-1, LANES)


def _unpack(packed, shapes):
    flat = packed.reshape(-1)
    out, o = [], 0
    for sh in shapes:
        sz = 1
        for d in sh:
            sz *= d
        out.append(flat[o:o + sz].reshape(sh))
        o += sz
    return out


def _gather_axis(g, ax):
    g = jnp.moveaxis(g, 0, ax)
    return g.reshape(g.shape[:ax] + (g.shape[ax] * g.shape[ax + 1],) + g.shape[ax + 2:])


def _comm_rows(a):
    r = a.shape[0]
    rp = r if r % 16 == 0 else _round_up(r, ROW_PAD)
    return jnp.pad(_bf(a), ((0, rp - r), (0, 0)))


def _natural(g, r, pad_to=None):
    full = g[:, :r].reshape(N_DEV * r, g.shape[2])
    if pad_to is not None and full.shape[0] % pad_to:
        full = jnp.pad(full, ((0, _round_up(full.shape[0], pad_to) - full.shape[0]), (0, 0)))
    return full


def _blocked(full, r):
    blocks = full[:N_DEV * r].reshape(N_DEV, r, full.shape[1])
    rp = r if r % 16 == 0 else _round_up(r, ROW_PAD)
    return jnp.pad(_bf(blocks), ((0, 0), (0, rp - r), (0, 0)))


def kernel(x, *rest):
    nw_ = len(WEIGHTS)
    W = dict(zip(WEIGHTS, rest[:nw_]))
    loss_target = rest[nw_]
    M = dict(zip(WEIGHTS, rest[nw_ + 1:2 * nw_ + 1]))
    V = dict(zip(WEIGHTS, rest[2 * nw_ + 1:3 * nw_ + 1]))

    T0, D = x.shape[1], x.shape[2]
    T = T0 + N_META
    TP = _round_up(T, ROW_TILE)
    me = 4 * lax.axis_index("x") + 2 * lax.axis_index("y") + lax.axis_index("c")

    small_sharded = [k for k in WEIGHTS if k in SMALL and SMALL[k] is not None]
    (sg,) = all_gather("ag_small", [_pack([W[k] for k in small_sharded])])
    per_dev = [_unpack(sg[b], [W[k].shape for k in small_sharded]) for b in range(N_DEV)]
    full = {k: _gather_axis(jnp.stack([per_dev[b][i] for b in range(N_DEV)]), SMALL[k]) for i, k in enumerate(small_sharded)}
    for k in SMALL:
        if SMALL[k] is None:
            full[k] = W[k]

    r_ff = W['ffn_w_gate'].shape[3]
    r_mi, r_mo = W['mlstm_w_in'].shape[2], W['mlstm_w_out'].shape[1]
    r_gi, r_go = W['gdn_w_in'].shape[2], W['gdn_w_out'].shape[1]
    r_si, r_so = W['swa_w_qkv'].shape[2], W['swa_w_out'].shape[1]
    pw = W['pool_w'][0]
    stages = [(i, j) for i in range(DEPTH) for j in range(3)]

    def stage_shards(i, j):
        if j != 1:
            s = j // 2
            return [_comm_rows(W['ffn_w_gate'][i, s].T), _comm_rows(W['ffn_w_up'][i, s].T), _comm_rows(W['ffn_w_down'][i, s])]
        if i % 4 == 0:
            return [_comm_rows(W['mlstm_w_in'][0].T), _comm_rows(W['mlstm_w_out'][0])]
        if i % 4 == 1:
            return [_bf(pw.reshape(-1, pw.shape[2]))]
        if i % 4 == 2:
            return [_comm_rows(W['gdn_w_in'][0].T), _comm_rows(W['gdn_w_out'][0])]
        return [_comm_rows(W['swa_w_qkv'][0].T), _comm_rows(W['swa_w_out'][0])]

    def stage_weights(i, j, g):
        if j != 1:
            return tuple(_natural(a, r_ff) for a in g)
        if i % 4 == 0:
            return (_natural(g[0], r_mi, IN_PAD), _natural(g[1], r_mo))
        if i % 4 == 1:
            return (g[0].reshape(N_DEV, N_POOL, pw.shape[1], pw.shape[2]).transpose(1, 0, 2, 3)
                    .reshape(N_POOL, pw.shape[2], pw.shape[2]).astype(F32),)
        if i % 4 == 2:
            return (_natural(g[0], r_gi, IN_PAD), _natural(g[1], r_go))
        return (_natural(g[0], r_si), _natural(g[1], r_so))

    def gather_start(k, after):
        sh = stage_shards(*stages[k])
        return push_start(f"ag_start_{k}", sh, [_own_block(s, me) for s in sh], after, True)

    first = all_gather("ag_stage_0", stage_shards(*stages[0]))
    wts = {0: stage_weights(*stages[0], first)}
    pending, zero = {}, jnp.zeros((), F32)
    pending[1], tok1 = gather_start(1, first[0])
    pending[2], tok2 = gather_start(2, tok1)
    tok = tok1[0, 0] + tok2[0, 0]

    pos = jnp.arange(TP, dtype=jnp.int32)[:, None]
    inv = ROPE_THETA ** (-jnp.arange(0, SWA_DH, 2, dtype=F32) / SWA_DH)
    ang = jnp.arange(TP, dtype=F32)[:, None] * inv[None, :]
    ang = jnp.concatenate([ang, ang, ang, ang], axis=1)
    cos, sin = jnp.cos(ang), jnp.sin(ang)
    row_mask = ((pos >= N_META) & (pos < T)).astype(F32)
    tgt = jnp.pad(loss_target[0], ((N_META, TP - T), (0, 0)))
    nrm = lambda i, j: full['norm_w'][i, j][None, :]

    h = jnp.concatenate([full['meta_tokens'], x[0], jnp.zeros((TP - T, D), F32)], axis=0)
    res = {}
    for k, (i, j) in enumerate(stages):
        if k >= 1:
            wts[k] = stage_weights(i, j, push_wait(f"ag_wait_{k}", pending.pop(k), h))
            tok = zero
            if k + 2 < len(stages):
                pending[k + 2], t = gather_start(k + 2, h)
                tok = t[0, 0]
        nw = nrm(i, j) + tok
        w = wts[k]
        if j != 1:
            h, res[k] = ffn_fwd(f"ffn_{i}_{j // 2}", h, nw, *w)
        elif i % 4 == 0:
            h, res[k] = mlstm_fwd(h, nw, w[0], w[1], full['mlstm_b_if'], full['mlstm_norm_w'])
        elif i % 4 == 1:
            h, res[k] = pool_fwd(h, nw, w[0], full['pool_scale'], pos)
        elif i % 4 == 2:
            h, res[k] = gdn_fwd(h, nw, w[0], w[1], full['gdn_conv_w'][0], full['gdn_a_log'], full['gdn_dt_bias'],
                                full['gdn_norm_w'])
        else:
            h, res[k] = swa_fwd(h, nw, w[0], full['swa_b_qkv'], w[1], full['swa_b_out'], full['swa_sinks'], cos, sin, pos)

    loss_local, dh, d_final = loss_call(h, full['final_norm_w'][None, :], tgt, row_mask)
    loss = lax.psum(loss_local[0, 0], ("x", "y", "c"))

    gs = {'final_norm_w': d_final[0]}
    d_norm = [[None] * 3 for _ in range(DEPTH)]
    sent = {}
    dep = jnp.zeros((8, LANES), F32)
    for k in reversed(range(len(stages))):
        i, j = stages[k]
        nw, w = nrm(i, j), wts[k]
        if j != 1:
            dh, d_norm[i][j], dg, du, dd = ffn_bwd(f"ffn_{i}_{j // 2}", dh, res[k], nw, *w, dep)
            blocks = [_blocked(dg, r_ff), _blocked(du, r_ff), _blocked(dd, r_ff)]
        else:
            if i % 4 == 0:
                dh, gm = mlstm_bwd(dh, res[k], nw, w[0], w[1], full['mlstm_norm_w'], dep)
                blocks = [_blocked(gm['w_in_t'], r_mi), _blocked(gm['w_out'], r_mo)]
                gs.update(mlstm_b_if=gm['b_if'], mlstm_norm_w=gm['norm_w'])
            elif i % 4 == 1:
                dh, gm = pool_bwd(dh, res[k], nw + dep[0, 0], w[0], full['pool_scale'], pos)
                dw = gm['w'].reshape(N_POOL, N_DEV, pw.shape[1], pw.shape[2]).transpose(1, 0, 2, 3)
                blocks = [_bf(dw.reshape(N_DEV, N_POOL * pw.shape[1], pw.shape[2]))]
                gs.update(pool_scale=gm['scale'])
            elif i % 4 == 2:
                dh, gm = gdn_bwd(dh, res[k], nw, w[0], w[1], full['gdn_conv_w'][0], full['gdn_norm_w'], dep)
                blocks = [_blocked(gm['w_in_t'], r_gi), _blocked(gm['w_out'], r_go)]
                gs.update(gdn_conv_w=gm['conv_w'][None], gdn_a_log=gm['a_log'], gdn_dt_bias=gm['dt_bias'],
                          gdn_norm_w=gm['norm_w'])
            else:
                dh, gm = swa_bwd(dh, res[k], nw, w[0], w[1], full['swa_sinks'], cos, sin, pos, dep)
                blocks = [_blocked(gm['w_qkv_t'], r_si), _blocked(gm['w_out'], r_so)]
                gs.update(swa_b_qkv=gm['b_qkv'], swa_b_out=gm['b_out'], swa_sinks=gm['sinks'])
            d_norm[i][1] = gm['nw']
        lands = [_own_block(lax.dynamic_index_in_dim(b, me, 0, keepdims=False), me) for b in blocks]
        sent[k], dep = push_start(f"rs_start_{k}", blocks, lands, dh, False)
    gs['norm_w'] = jnp.stack([jnp.concatenate(d_norm[i], axis=0) for i in range(DEPTH)])
    gs['meta_tokens'] = dh[:N_META]
    grad_x = dh[N_META:T][None]

    grads = {}
    small_names = [k for k in WEIGHTS if k in SMALL]
    (parts,) = all_gather("ag_small_grads", [_pack([gs[k].reshape(full[k].shape) for k in small_names])])
    tot = _unpack(sum_blocks("sum_small_grads", parts), [full[k].shape for k in small_names])
    for k, g in zip(small_names, tot):
        ax = SMALL[k]
        grads[k] = g if ax is None else lax.dynamic_slice_in_dim(g, me * W[k].shape[ax], W[k].shape[ax], axis=ax)

    summed, after = {}, dh
    for k in reversed(range(len(stages))):
        parts = push_wait(f"rs_wait_{k}", sent.pop(k), after)
        summed[k] = [sum_blocks(f"sum_{k}_{n}", p) for n, p in enumerate(parts)]
        after = summed[k][0]

    gg, gu, gd = [], [], []
    for i in range(DEPTH):
        for j in (0, 2):
            a, b, c_ = summed[3 * i + j]
            gg.append(a[:r_ff].T)
            gu.append(b[:r_ff].T)
            gd.append(c_[:r_ff])
    shape4 = lambda lst, ref: jnp.stack(lst).reshape(ref.shape)
    grads['ffn_w_gate'] = shape4(gg, W['ffn_w_gate'])
    grads['ffn_w_up'] = shape4(gu, W['ffn_w_up'])
    grads['ffn_w_down'] = shape4(gd, W['ffn_w_down'])
    for i in range(DEPTH):
        g = summed[3 * i + 1]
        if i % 4 == 0:
            grads['mlstm_w_in'], grads['mlstm_w_out'] = g[0][:r_mi].T[None], g[1][:r_mo][None]
        elif i % 4 == 1:
            grads['pool_w'] = g[0].reshape(W['pool_w'].shape)
        elif i % 4 == 2:
            grads['gdn_w_in'], grads['gdn_w_out'] = g[0][:r_gi].T[None], g[1][:r_go][None]
        else:
            grads['swa_w_qkv'], grads['swa_w_out'] = g[0][:r_si].T[None], g[1][:r_so][None]

    delta, new_m, new_v = {}, {}, {}
    shapes = [W[k].shape for k in small_names]
    d, nm, nv = adamw("adamw_small", _pack([W[k] for k in small_names]), _pack([grads[k] for k in small_names]),
                      _pack([M[k] for k in small_names]), _pack([V[k] for k in small_names]))
    for k, a, b, c_ in zip(small_names, _unpack(d, shapes), _unpack(nm, shapes), _unpack(nv, shapes)):
        delta[k], new_m[k], new_v[k] = a, b, c_
    for k in WEIGHTS:
        if k not in SMALL:
            two = lambda a: a.reshape(-1, a.shape[-1])
            d, nm, nv = adamw("adamw_" + k, two(W[k]), two(grads[k]), two(M[k]), two(V[k]))
            delta[k], new_m[k], new_v[k] = d.reshape(W[k].shape), nm.reshape(W[k].shape), nv.reshape(W[k].shape)

    return (loss, grad_x, *[grads[k].reshape(W[k].shape) for k in WEIGHTS], *[delta[k] for k in WEIGHTS],
            *[new_m[k] for k in WEIGHTS], *[new_v[k] for k in WEIGHTS])
```

```python
import functools

import jax
import jax.numpy as jnp
from jax import lax
from jax.experimental import pallas as pl
from jax.experimental.pallas import tpu as pltpu

F32 = jnp.float32
BF16 = jnp.bfloat16

N_DEV = 8
N_META = 16
EPS = 1e-6
DEPTH = 4
MLSTM_HEADS = 8
MLSTM_CHUNK = 64
N_POOL = 4
GDN_DK = 128
GDN_CHUNK = 64
GDN_CONV = 4
SWA_DH = 64
SWA_GROUP = 8
SWA_WINDOW = 128
ROPE_THETA = 10000.0
ADAM_LR = 0.001
ADAM_B1 = 0.9
ADAM_B2 = 0.999
ADAM_EPS = 1e-08
ADAM_WD = 0.01
ADAM_STEP = 10

LANES = 128
ROW_TILE = 128
ROW_PAD = 112
IN_PAD = 896
VMEM_LIMIT = 56 * 1024 * 1024
TOKEN_TILES = (1056, 768, 512, 384, 256, 128)
FEATURE_TILES = (512, 896, 768, 640, 384, 256, 128)
ROW_TILES = (512, 352, 256, 176, 160, 128, 112, 64, 48, 32, 16, 8)
HIGHEST = lax.Precision.HIGHEST
HIGH = lax.Precision.HIGH
MESH_ID = pl.DeviceIdType.MESH

WEIGHTS = ('meta_tokens', 'norm_w', 'ffn_w_gate', 'ffn_w_up', 'ffn_w_down', 'mlstm_w_in', 'mlstm_b_if',
           'mlstm_norm_w', 'mlstm_w_out', 'pool_w', 'pool_scale', 'gdn_w_in', 'gdn_conv_w', 'gdn_a_log',
           'gdn_dt_bias', 'gdn_norm_w', 'gdn_w_out', 'swa_w_qkv', 'swa_b_qkv', 'swa_sinks', 'swa_w_out',
           'swa_b_out', 'final_norm_w')
SMALL = {'meta_tokens': 1, 'norm_w': 2, 'mlstm_b_if': None, 'mlstm_norm_w': None, 'pool_scale': 1,
         'gdn_conv_w': 2, 'gdn_a_log': None, 'gdn_dt_bias': None, 'gdn_norm_w': None, 'swa_b_qkv': 1,
         'swa_sinks': None, 'swa_b_out': 1, 'final_norm_w': None}


def _pick(n, cands):
    for c in cands:
        if n % c == 0:
            return c
    return n


def _round_up(n, m):
    return -(-n // m) * m


def _bf(x):
    return x.astype(BF16)


def _dot(a, b, dims, precision=None):
    return lax.dot_general(a, b, (dims, ((), ())), preferred_element_type=F32, precision=precision)


def _mm(a, b):
    return _dot(_bf(a), _bf(b), ((1,), (0,)))


def _mm_nt(a, b):
    return _dot(_bf(a), _bf(b), ((1,), (1,)))


def _mm_tn(a, b):
    return _dot(_bf(a), _bf(b), ((0,), (0,)))


def _mm32(a, b):
    return _dot(a, b, ((1,), (0,)), precision=HIGHEST)


def _mm3(a, b):
    return _dot(a, b, ((1,), (0,)), precision=HIGH)


def _iota(shape, axis):
    return lax.broadcasted_iota(jnp.int32, shape, axis)


def _row2col(row, eye):
    return jnp.sum(eye * row, axis=1, keepdims=True)


def _lane_roll(shift):
    @jax.custom_vjp
    def f(x):
        return pltpu.roll(x, shift, 1)

    def fwd(x):
        return f(x), None

    def bwd(_, g):
        return (pltpu.roll(g, g.shape[1] - shift, 1),)

    f.defvjp(fwd, bwd)
    return f


def mm_call(name, mode, pairs, n_acc, epilogue, out_dtypes, extras=(), tm=None, tn=None, tk=None):
    a0, b0 = pairs[0][0], pairs[0][1]
    if mode == 'nn':
        (M, K), N = a0.shape, b0.shape[1]
    elif mode == 'nt':
        (M, K), N = a0.shape, b0.shape[0]
    else:
        (K, M), N = a0.shape, b0.shape[1]
    if mode == 'tn':
        tm = tm or _pick(M, FEATURE_TILES)
        tn = tn or (N if N <= 2048 else _pick(N, FEATURE_TILES))
        tk = tk or _pick(K, [t for t in (2112,) + TOKEN_TILES
                             if 2 * t * (tm * a0.dtype.itemsize + tn * b0.dtype.itemsize) <= VMEM_LIMIT // 2])
        dims = ((0,), (0,))
        a_spec = pl.BlockSpec((tk, tm), lambda i, j, k: (k, i))
        b_spec = pl.BlockSpec((tk, tn), lambda i, j, k: (k, j))
    else:
        tm = tm or _pick(M, TOKEN_TILES)
        tn = tn or _pick(N, FEATURE_TILES)
        fixed = tm * tn * 4 * (n_acc + 2 * len(out_dtypes) + 2 * sum(kind == 'mn' for _, kind in extras))
        fits = lambda t: 2 * len(pairs) * t * (tm * a0.dtype.itemsize + tn * b0.dtype.itemsize) + fixed <= VMEM_LIMIT * 7 // 8
        tk = tk or _pick(K, [t for t in (K, 2816, 2048) + FEATURE_TILES if fits(t)])
        a_spec = pl.BlockSpec((tm, tk), lambda i, j, k: (i, k))
        if mode == 'nn':
            dims = ((1,), (0,))
            b_spec = pl.BlockSpec((tk, tn), lambda i, j, k: (k, j))
        else:
            dims = ((1,), (1,))
            b_spec = pl.BlockSpec((tn, tk), lambda i, j, k: (j, k))
    n_pairs, n_ex, n_out = len(pairs), len(extras), len(out_dtypes)
    nk = K // tk

    def body(*refs):
        ab = refs[:2 * n_pairs]
        ex = refs[2 * n_pairs:2 * n_pairs + n_ex]
        outs = refs[2 * n_pairs + n_ex:2 * n_pairs + n_ex + n_out]
        accs = refs[2 * n_pairs + n_ex + n_out:]
        k = pl.program_id(2)

        @pl.when(k == 0)
        def _():
            for acc in accs:
                acc[...] = jnp.zeros_like(acc)

        for p, (_, _, ai) in enumerate(pairs):
            accs[ai][...] += _dot(_bf(ab[2 * p][...]), _bf(ab[2 * p + 1][...]), dims)

        @pl.when(k == nk - 1)
        def _():
            res = epilogue([acc[...] for acc in accs], [e[...] for e in ex])
            for o, v in zip(outs, res):
                o[...] = v.astype(o.dtype)

    ex_specs = [pl.BlockSpec((tm, tn), lambda i, j, k: (i, j)) if kind == 'mn'
                else pl.BlockSpec((1, tn), lambda i, j, k: (0, j)) if kind == 'n'
                else pl.BlockSpec(e.shape, lambda i, j, k: (0, 0)) for e, kind in extras]
    outs = pl.pallas_call(
        body, grid=(M // tm, N // tn, nk),
        in_specs=[a_spec, b_spec] * n_pairs + ex_specs,
        out_specs=[pl.BlockSpec((tm, tn), lambda i, j, k: (i, j)) for _ in out_dtypes],
        out_shape=[jax.ShapeDtypeStruct((M, N), dt) for dt in out_dtypes],
        scratch_shapes=[pltpu.VMEM((tm, tn), F32) for _ in range(n_acc)],
        compiler_params=pltpu.CompilerParams(dimension_semantics=("arbitrary",) * 3, vmem_limit_bytes=VMEM_LIMIT),
        name=name)(*[t for a, b, _ in pairs for t in (a, b)], *[e for e, _ in extras])
    return outs


def mm_plain(name, mode, a, b, dtype=F32, scale=None, dep=None):
    ep = (lambda accs, ex: [accs[0]]) if scale is None else (lambda accs, ex: [accs[0] * scale])
    return mm_call(name, mode, [(a, b, 0)], 1, ep, [dtype], extras=[] if dep is None else [(dep, 'dep')])[0]


def scan_fwd(name, step_fn, n_outer, n_steps, params, consts, xs, states, ys):
    n_p, n_c, n_x, n_s, n_y = len(params), len(consts), len(xs), len(states), len(ys)

    def body(*refs):
        p_refs = refs[:n_p]
        c_refs = refs[n_p:n_p + n_c]
        x_refs = refs[n_p + n_c:n_p + n_c + n_x]
        o = n_p + n_c + n_x
        y_refs = refs[o:o + n_y]
        sv_refs = refs[o + n_y:o + n_y + n_s]
        st_refs = refs[o + n_y + n_s:]
        s = pl.program_id(1)

        @pl.when(s == 0)
        def _():
            for r in st_refs:
                r[...] = jnp.zeros_like(r)

        st = tuple(r[...] for r in st_refs)
        for sv, v in zip(sv_refs, st):
            sv[...] = v
        new_st, y = step_fn(tuple(r[...] for r in p_refs), st, tuple(r[...] for r in x_refs),
                            tuple(r[...] for r in c_refs))
        for r, v in zip(y_refs, y):
            r[...] = v.astype(r.dtype)
        for r, v in zip(st_refs, new_st):
            r[...] = v

    in_specs = ([pl.BlockSpec(p[1], (lambda o, s, f=p[2]: f(o))) for p in params]
                + [pl.BlockSpec(c[1], c[2]) for c in consts]
                + [pl.BlockSpec(x[1], x[2]) for x in xs])
    out_specs = ([pl.BlockSpec(b, f) for _, _, b, f in ys]
                 + [pl.BlockSpec((None, None) + tuple(sh), (lambda o, s, n=len(sh): (o, s) + (0,) * n)) for sh, _ in states])
    out_shape = ([jax.ShapeDtypeStruct(sh, dt) for sh, dt, _, _ in ys]
                 + [jax.ShapeDtypeStruct((n_outer, n_steps) + tuple(sh), dt) for sh, dt in states])
    outs = pl.pallas_call(
        body, grid=(n_outer, n_steps), in_specs=in_specs, out_specs=out_specs, out_shape=out_shape,
        scratch_shapes=[pltpu.VMEM(tuple(sh), dt) for sh, dt in states],
        compiler_params=pltpu.CompilerParams(dimension_semantics=("arbitrary", "arbitrary"), vmem_limit_bytes=VMEM_LIMIT),
        name=name)(*[p[0] for p in params], *[c[0] for c in consts], *[x[0] for x in xs])
    return tuple(outs[:n_y]), tuple(outs[n_y:])


def scan_bwd(name, step_fn, n_outer, n_steps, params, consts, xs, states, saved, dys, glob, bf16_copies=()):
    n_p, n_c, n_x, n_s, n_y = len(params), len(consts), len(xs), len(states), len(dys)
    n_cp = len(bf16_copies)
    rev = lambda f: (lambda o, s: f(o, n_steps - 1 - s))

    def body(*refs):
        p_refs = refs[:n_p]
        c_refs = refs[n_p:n_p + n_c]
        x_refs = refs[n_p + n_c:n_p + n_c + n_x]
        o = n_p + n_c + n_x
        sv_refs = refs[o:o + n_s]
        dy_refs = refs[o + n_s:o + n_s + n_y]
        o = o + n_s + n_y
        dx_refs = refs[o:o + n_x]
        dp_refs = refs[o + n_x:o + n_x + n_p]
        cp_refs = refs[o + n_x + n_p:o + n_x + n_p + n_cp]
        dst_refs = refs[o + n_x + n_p + n_cp:]
        oi, s = pl.program_id(0), pl.program_id(1)

        @pl.when(s == 0)
        def _():
            for r in dst_refs:
                r[...] = jnp.zeros_like(r)

        for r, g in zip(dp_refs, glob):
            @pl.when(((s == 0) & (oi == 0)) if g else (s == 0))
            def _(r=r):
                r[...] = jnp.zeros_like(r)

        c_vals = tuple(r[...] for r in c_refs)
        f = lambda p, st, x: step_fn(p, st, x, c_vals)
        _, vjp = jax.vjp(f, tuple(r[...] for r in p_refs), tuple(r[...] for r in sv_refs), tuple(r[...] for r in x_refs))
        dp, dst, dx = vjp((tuple(r[...] for r in dst_refs), tuple(r[...] for r in dy_refs)))
        for r, v in zip(dx_refs, dx):
            r[...] = v.astype(r.dtype)
        for r, i in zip(cp_refs, bf16_copies):
            r[...] = dx[i].astype(r.dtype)
        for r, v in zip(dst_refs, dst):
            r[...] = v
        for r, v in zip(dp_refs, dp):
            r[...] += v

    gshape = lambda t: t[3] if len(t) > 3 else t[0].shape
    gidx = lambda t: t[4] if len(t) > 3 else t[2]
    in_specs = ([pl.BlockSpec(p[1], (lambda o, s, f=p[2]: f(o))) for p in params]
                + [pl.BlockSpec(c[1], rev(c[2])) for c in consts]
                + [pl.BlockSpec(x[1], rev(x[2])) for x in xs]
                + [pl.BlockSpec((None, None) + tuple(sh), (lambda o, s, n=len(sh): (o, n_steps - 1 - s) + (0,) * n)) for sh, _ in states]
                + [pl.BlockSpec(b, rev(f)) for _, b, f in dys])
    out_specs = ([pl.BlockSpec(x[1], rev(gidx(x))) for x in xs]
                 + [pl.BlockSpec(p[1], (lambda o, s, f=gidx(p): f(o))) for p in params]
                 + [pl.BlockSpec(xs[i][1], rev(gidx(xs[i]))) for i in bf16_copies])
    out_shape = ([jax.ShapeDtypeStruct(gshape(x), F32) for x in xs]
                 + [jax.ShapeDtypeStruct(gshape(p), F32) for p in params]
                 + [jax.ShapeDtypeStruct(gshape(xs[i]), BF16) for i in bf16_copies])
    outs = pl.pallas_call(
        body, grid=(n_outer, n_steps), in_specs=in_specs, out_specs=out_specs, out_shape=out_shape,
        scratch_shapes=[pltpu.VMEM(tuple(sh), dt) for sh, dt in states],
        compiler_params=pltpu.CompilerParams(dimension_semantics=("arbitrary", "arbitrary"), vmem_limit_bytes=VMEM_LIMIT),
        name=name)(*[p[0] for p in params], *[c[0] for c in consts], *[x[0] for x in xs], *saved,
                   *[d[0] for d in dys])
    if n_cp:
        return tuple(outs[:n_x]), tuple(outs[n_x:n_x + n_p]), tuple(outs[n_x + n_p:])
    return tuple(outs[:n_x]), tuple(outs[n_x:])


def _rows(a, rt):
    return (a, (rt, a.shape[1]), lambda o, s: (s, 0))


def _whole(a):
    return (a, a.shape, lambda o: (0,) * a.ndim)


def _rms(h, w):
    return h * lax.rsqrt(jnp.mean(h * h, axis=1, keepdims=True) + EPS) * w


def rms_fwd(name, h, w, dtype):
    TP, D = h.shape
    rt = _pick(TP, (384, 128))
    step = lambda p, st, x, c: ((), (_rms(x[0], p[0]),))
    (y,), _ = scan_fwd(name, step, 1, TP // rt, [_whole(w)], [], [_rows(h, rt)], [],
                       [((TP, D), dtype, (rt, D), lambda o, s: (s, 0))])
    return y


def rms_bwd(name, h, w, dxn, dres):
    TP, D = h.shape
    rt = _pick(TP, (384, 128))
    step = lambda p, st, x, c: ((), (_rms(x[0], p[0]), x[0]))
    (dh,), (dw,), (dh_bf,) = scan_bwd(name, step, 1, TP // rt, [_whole(w)], [], [_rows(h, rt)], [], [],
                                      [_rows(dxn, rt), _rows(dres, rt)], [True], bf16_copies=(0,))
    return (dh, dh_bf), dw


def loss_call(h, w, tgt, mask):
    TP, D = h.shape
    rt = _pick(TP, (384, 128))

    def body(h_ref, w_ref, t_ref, m_ref, loss_ref, dh_ref, dw_ref, dhb_ref):
        s = pl.program_id(0)

        @pl.when(s == 0)
        def _():
            loss_ref[...] = jnp.zeros_like(loss_ref)
            dw_ref[...] = jnp.zeros_like(dw_ref)

        tg, mk = t_ref[...], m_ref[...]

        def f(wv, hv):
            err = jnp.square(_rms(hv, wv) - tg) * mk
            return 0.5 * jnp.sum(jnp.sum(err, axis=1, keepdims=True), axis=0, keepdims=True) / D

        l, vjp = jax.vjp(f, w_ref[...], h_ref[...])
        dw, dh = vjp(jnp.ones((1, 1), F32))
        loss_ref[...] += l
        dw_ref[...] += dw
        dh_ref[...] = dh
        dhb_ref[...] = dh.astype(dhb_ref.dtype)

    row = lambda wd: pl.BlockSpec((rt, wd), lambda s: (s, 0))
    const = lambda shape: pl.BlockSpec(shape, lambda s: (0, 0))
    return pl.pallas_call(
        body, grid=(TP // rt,), in_specs=[row(D), const((1, D)), row(D), row(1)],
        out_specs=[const((1, 1)), row(D), const((1, D)), row(D)],
        out_shape=[jax.ShapeDtypeStruct((1, 1), F32), jax.ShapeDtypeStruct((TP, D), F32), jax.ShapeDtypeStruct((1, D), F32),
                   jax.ShapeDtypeStruct((TP, D), BF16)],
        compiler_params=pltpu.CompilerParams(dimension_semantics=("arbitrary",), vmem_limit_bytes=VMEM_LIMIT),
        name="loss_head")(h, w, tgt, mask)


def colsum(name, a):
    TP, N = a.shape
    rt = _pick(TP, (384, 128))

    def body(a_ref, o_ref):
        @pl.when(pl.program_id(0) == 0)
        def _():
            o_ref[...] = jnp.zeros_like(o_ref)
        o_ref[...] += jnp.sum(a_ref[...], axis=0, keepdims=True)

    return pl.pallas_call(
        body, grid=(TP // rt,), in_specs=[pl.BlockSpec((rt, N), lambda s: (s, 0))],
        out_specs=pl.BlockSpec((1, N), lambda s: (0, 0)), out_shape=jax.ShapeDtypeStruct((1, N), F32),
        compiler_params=pltpu.CompilerParams(dimension_semantics=("arbitrary",)), name=name)(a)


def ffn_fwd(tag, h, nw, wg_t, wu_t, wd):
    xn = rms_fwd(tag + "_norm", h, nw, BF16)

    def ep_up(accs, ex):
        g, u = accs
        return [g, u, jax.nn.silu(g) * u]

    g, u, a = mm_call(tag + "_up", 'nt', [(xn, wg_t, 0), (xn, wu_t, 1)], 2, ep_up, [BF16, BF16, BF16])
    h2 = mm_call(tag + "_down", 'nn', [(a, wd, 0)], 1, lambda accs, ex: [ex[0] + 0.5 * accs[0]], [F32],
                 extras=[(h, 'mn')])[0]
    return h2, (h, xn, g, u, a)


def ffn_bwd(tag, dh2, res, nw, wg_t, wu_t, wd, dep, send):
    h, xn, g, u, a = res
    dh2, dhb = dh2

    def ep_act(accs, ex):
        da = 0.5 * accs[0]
        gv, uv = ex[0].astype(F32), ex[1].astype(F32)
        sg = jax.nn.sigmoid(gv)
        return [da * uv * (sg * (1.0 + gv * (1.0 - sg))), da * (gv * sg)]

    dg, du = mm_call(tag + "_dact", 'nt', [(dhb, wd, 0)], 1, ep_act, [BF16, BF16],
                     extras=[(g, 'mn'), (u, 'mn'), (dep, 'dep')])
    dwd = mm_plain(tag + "_dwd", 'tn', a, dhb, BF16, scale=0.5)
    tok = send('d', [dwd])
    dxn = mm_call(tag + "_dxn", 'nn', [(dg, wg_t, 0), (du, wu_t, 0)], 1, lambda accs, ex: [accs[0]], [F32],
                  extras=[(tok, 'dep')])[0]
    dwg_t = mm_plain(tag + "_dwg", 'tn', dg, xn, BF16)
    dwu_t = mm_plain(tag + "_dwu", 'tn', du, xn, BF16)
    tok = send('gu', [dwg_t, dwu_t])
    dh, dnw = rms_bwd(tag + "_dnorm", h, nw, dxn, dh2)
    return dh, dnw, tok


def _mlstm_step(params, state, xs, consts, *, dk):
    bif, nw = params
    c_st, n_st, m_st = state
    q, k, v, og, gr = xs
    C = MLSTM_CHUNK
    R = q.shape[0]
    ri, ci = _iota((C, C), 0), _iota((C, C), 1)
    eye = (ri == ci).astype(F32)
    causal = ci <= ri
    upper = (ri <= ci).astype(F32)
    outs = []
    for j in range(R // C):
        sl = slice(j * C, (j + 1) * C)
        qj, kj, vj = q[sl] * (dk ** -0.5), k[sl], v[sl]
        li = gr[0:1, sl] + bif[0:1, 0:1]
        lf = jax.nn.log_sigmoid(gr[1:2, sl] + bif[1:2, 0:1])
        b_row = _mm32(lf, upper)
        b_col = _row2col(b_row, eye)
        log_w = jnp.where(causal, b_col - b_row + li, -jnp.inf)
        log_init = b_col + m_st
        m_t = lax.stop_gradient(jnp.maximum(log_init, jnp.max(log_w, axis=1, keepdims=True)))
        w = jnp.exp(log_w - m_t)
        w_init = jnp.exp(log_init - m_t)
        qk = _mm_nt(qj, kj) * w
        num = w_init * _mm(qj, c_st) + _mm(qk, vj)
        den = w_init * jnp.sum(qj * n_st, axis=1, keepdims=True) + jnp.sum(qk, axis=1, keepdims=True)
        h = num / jnp.maximum(jnp.abs(den), jnp.exp(-m_t))
        b_last = b_row[:, C - 1:C]
        log_end_init = b_last + m_st
        log_end = b_last - b_row + li
        m_new = lax.stop_gradient(jnp.maximum(log_end_init, jnp.max(log_end, axis=1, keepdims=True)))
        a_init = jnp.exp(log_end_init - m_new)
        ka = kj * _row2col(jnp.exp(log_end - m_new), eye)
        c_st = a_init * c_st + _mm_tn(ka, vj)
        n_st = a_init * n_st + jnp.sum(ka, axis=0, keepdims=True)
        m_st = m_new
        hn = h * lax.rsqrt(jnp.mean(h * h, axis=1, keepdims=True) + EPS)
        outs.append(hn * nw * jax.nn.sigmoid(og[sl]))
    return (c_st, n_st, m_st), (jnp.concatenate(outs, axis=0),)


def _mlstm_ops(p, gr, bif, nw):
    H = MLSTM_HEADS
    TP = p.shape[0]
    dv = nw.shape[1] // H
    dk = dv // 2
    R = ROW_TILE
    step = functools.partial(_mlstm_step, dk=dk)
    params = [(bif, (None, 2, LANES), lambda o: (o, 0, 0)), (nw, (1, dv), lambda o: (0, o))]
    col = lambda w, off: (p, (R, w), (lambda o, s: (s, off + o)), (TP, H * w), (lambda o, s: (s, o)))
    xs = [col(dk, 0), col(dk, H), col(dv, H), col(dv, 2 * H), (gr, (None, 2, R), lambda o, s: (o, 0, s))]
    states = [((dk, dv), F32), ((1, dk), F32), ((1, 1), F32)]
    ys = [((TP, H * dv), F32, (R, dv), lambda o, s: (s, o))]
    return step, H, TP // R, params, xs, states, ys


def mlstm_fwd(h, nw1, w_in_t, w_out, b_if, norm_w):
    H = MLSTM_HEADS
    D = h.shape[1]
    u = rms_fwd("mlstm_norm", h, nw1, BF16)
    p = mm_plain("mlstm_in", 'nt', u, w_in_t)
    gr = p[:, 3 * D:3 * D + 2 * H].T.reshape(2, H, -1).transpose(1, 0, 2)
    bif = jnp.broadcast_to(b_if.reshape(2, H).T[:, :, None], (H, 2, LANES))
    step, _, n, params, xs, states, ys = _mlstm_ops(p, gr, bif, norm_w)
    (act,), saved = scan_fwd("mlstm_core", step, H, n, params, [], xs, states, ys)
    h2 = mm_call("mlstm_out", 'nn', [(act, w_out, 0)], 1, lambda accs, ex: [ex[0] + accs[0]], [F32], extras=[(h, 'mn')])[0]
    return h2, (h, u, p, gr, bif, saved, act)


def mlstm_bwd(dh2, res, nw1, w_in_t, w_out, norm_w, dep):
    h, u, p, gr, bif, saved, act = res
    H = MLSTM_HEADS
    TP = h.shape[0]
    dh2, dhb = dh2
    dact = mm_plain("mlstm_dact", 'nt', dhb, w_out, dep=dep)
    dw_out = mm_plain("mlstm_dwout", 'tn', act, dhb, BF16)
    step, _, n, params, xs, states, ys = _mlstm_ops(p, gr, bif, norm_w)
    (dq, dk, dv, dog, dgr), (dbif, dnorm) = scan_bwd("mlstm_core_bwd", step, H, n, params, [], xs, states, saved,
                                                     [(dact, ys[0][2], ys[0][3])], [False, False])
    dgates = dgr.transpose(1, 0, 2).reshape(2 * H, TP).T
    pad = p.shape[1] - (dq.shape[1] + dk.shape[1] + dv.shape[1] + dog.shape[1] + 2 * H)
    dp = jnp.concatenate([dq, dk, dv, dog, dgates, jnp.zeros((TP, pad), F32)], axis=1)
    du = mm_plain("mlstm_du", 'nn', dp, w_in_t)
    dw_in_t = mm_plain("mlstm_dwin", 'tn', dp, u, BF16)
    dh, dnw1 = rms_bwd("mlstm_dnorm", h, nw1, du, dh2)
    db_if = dbif[:, :, 0].T.reshape(1, 2 * H)
    return dh, dict(nw=dnw1, w_in_t=dw_in_t, w_out=dw_out, b_if=db_if, norm_w=dnorm)


def _pool_step(params, state, xs, consts):
    w, scale = params
    (prev,) = state
    u, h = xs
    pos, win = consts
    R = u.shape[0]
    wn = win[0:1, 0:1]
    ext = jnp.concatenate([prev, u], axis=0)
    lag = _iota((R, 2 * R), 0) + R - _iota((R, 2 * R), 1)
    band = ((lag >= 0) & (lag < wn)).astype(F32)
    wsum = _mm32(band, ext)
    cnt = jnp.minimum(pos + 1, wn).astype(F32)
    pooled = wsum / cnt - u
    return (u,), (h + _mm(pooled, w) * scale,)


def _pool_ops(u, h, w, scale, pos):
    TP, D = u.shape
    G = D // N_POOL
    R = ROW_TILE
    win = jnp.broadcast_to(jnp.array([2 << g for g in range(N_POOL)], jnp.int32)[:, None, None], (N_POOL, 1, LANES))
    params = [(w, (None, G, G), lambda o: (o, 0, 0)), (scale, (1, G), lambda o: (0, o))]
    consts = [(pos, (R, 1), lambda o, s: (s, 0)), (win, (None, 1, LANES), lambda o, s: (o, 0, 0))]
    grp = lambda a: (a, (R, G), lambda o, s: (s, o))
    return N_POOL, TP // R, params, consts, [grp(u), grp(h)], [((R, G), F32)], [((TP, D), F32, (R, G), lambda o, s: (s, o))]


def pool_fwd(h, nw1, w, scale, pos):
    u = rms_fwd("pool_norm", h, nw1, F32)
    no, n, params, consts, xs, states, ys = _pool_ops(u, h, w, scale, pos)
    (h2,), saved = scan_fwd("pool_core", _pool_step, no, n, params, consts, xs, states, ys)
    return h2, (h, u, saved)


def pool_bwd(dh2, res, nw1, w, scale, pos):
    h, u, saved = res
    dh2 = dh2[0]
    no, n, params, consts, xs, states, ys = _pool_ops(u, h, w, scale, pos)
    (du, dres), (dw, dscale) = scan_bwd("pool_core_bwd", _pool_step, no, n, params, consts, xs, states, saved,
                                        [(dh2, ys[0][2], ys[0][3])], [False, False])
    dh, dnw1 = rms_bwd("pool_dnorm", h, nw1, du, dres)
    return dh, dict(nw=dnw1, w=dw, scale=dscale)


def _unit_lower_inverse(low, width):
    n = low.shape[0]
    ri, ci = _iota((n, n), 0), _iota((n, n), 1)
    inv = (ri == ci).astype(F32)
    b = 1
    while b < width:
        blk = 2 * b
        sh = blk.bit_length() - 1
        off = jnp.where(((ri >> sh) == (ci >> sh)) & ((ri & (blk - 1)) >= b) & ((ci & (blk - 1)) < b), low, 0.0)
        inv = inv - (off if b == 1 else _mm(_mm(inv, off), inv))
        b = blk
    return inv


def _conv_silu(prev, x, w):
    R = x.shape[0]
    ext = jnp.concatenate([prev, x], axis=0)
    y = sum(w[j:j + 1, :] * ext[8 - (GDN_CONV - 1) + j:8 - (GDN_CONV - 1) + j + R] for j in range(GDN_CONV))
    return jax.nn.silu(y)


def _gdn_step(params, state, xs, consts):
    cwq, cwk, cwv, ad, gnw = params
    s_cat, pq, pk, pv = state
    q, k, v, z, gb, gbc = xs
    C = GDN_CHUNK
    R, dk = q.shape
    dv = v.shape[1] // 2
    ri, ci = _iota((R, R), 0), _iota((R, R), 1)
    sh = C.bit_length() - 1
    same = (ri >> sh) == (ci >> sh)
    causal, strict = same & (ci <= ri), same & (ci < ri)
    qc, kc, vc = _conv_silu(pq, q, cwq), _conv_silu(pk, k, cwk), _conv_silu(pv, v, cwv)
    qn = qc * lax.rsqrt(jnp.sum(qc * qc, axis=1, keepdims=True) + EPS) * (dk ** -0.5)
    kn = kc * lax.rsqrt(jnp.sum(kc * kc, axis=1, keepdims=True) + EPS)
    kk, qk = _mm_nt(kn, kn), _mm_nt(qn, kn)
    g_rows = -jnp.exp(ad[0:2, 0:1]) * jax.nn.softplus(gb[2:4, :] + ad[2:4, 0:1])
    gc_rows = _mm3(g_rows, (same & (ri <= ci)).astype(F32))
    g_cols = jnp.concatenate([-jnp.exp(ad[e:e + 1, 0:1]) * jax.nn.softplus(gbc[:, 2 + e:3 + e] + ad[2 + e:3 + e, 0:1])
                              for e in range(2)], axis=1)
    gc_cols = _mm3(causal.astype(F32), g_cols)
    beta_cols = jax.nn.sigmoid(gbc[:, 0:2])
    lows, rhss, attns, qgs = [], [], [], []
    for e in range(2):
        gcc, bc = gc_cols[:, e:e + 1], beta_cols[:, e:e + 1]
        decay = jnp.exp(jnp.where(causal, gcc - gc_rows[e:e + 1, :], -jnp.inf))
        lows.append(jnp.where(strict, kk * bc * decay, 0.0))
        attns.append(qk * decay)
        eg = jnp.exp(gcc)
        rhss.append(jnp.concatenate([vc[:, e * dv:(e + 1) * dv] * bc, kn * (bc * eg)], axis=1))
        qgs.append(qn * eg)
    zero = jnp.zeros((R, R), F32)
    big = jnp.concatenate([jnp.concatenate([lows[0], zero], axis=1), jnp.concatenate([zero, lows[1]], axis=1)], axis=0)
    sol = _mm3(_unit_lower_inverse(big, C), jnp.concatenate(rhss, axis=0))
    outs = [[], []]
    n_chunks = R // C
    for j in range(n_chunks):
        sl = slice(j * C, (j + 1) * C)
        lhs = jnp.concatenate([sol[sl, dv:], sol[R + j * C:R + (j + 1) * C, dv:], qgs[0][sl], qgs[1][sl]], axis=0)
        pr = _mm(lhs, s_cat)
        v_new = [sol[e * R + j * C:e * R + (j + 1) * C, :dv] - pr[e * C:(e + 1) * C, e * dv:(e + 1) * dv] for e in range(2)]
        v_lanes = jnp.concatenate(v_new, axis=1)
        pad = jnp.zeros((C, 2 * dv), F32)
        v_rows = jnp.concatenate([v_lanes if i == j else pad for i in range(n_chunks)], axis=0)
        av = _mm(jnp.concatenate([attns[0][sl], attns[1][sl]], axis=0), v_rows)
        g_last = [gc_rows[e:e + 1, (j + 1) * C - 1:(j + 1) * C] for e in range(2)]
        kg = jnp.concatenate([kn[sl] * jnp.exp(g_last[e] - gc_cols[sl, e:e + 1]) for e in range(2)], axis=1)
        kv = _mm_tn(kg, v_lanes)
        s_cat = jnp.concatenate([jnp.exp(g_last[e]) * s_cat[:, e * dv:(e + 1) * dv]
                                 + kv[e * dk:(e + 1) * dk, e * dv:(e + 1) * dv] for e in range(2)], axis=1)
        for e in range(2):
            o = pr[(2 + e) * C:(3 + e) * C, e * dv:(e + 1) * dv] + av[e * C:(e + 1) * C, e * dv:(e + 1) * dv]
            on = o * lax.rsqrt(jnp.mean(o * o, axis=1, keepdims=True) + EPS) * gnw
            outs[e].append(on * jax.nn.silu(z[sl, e * dv:(e + 1) * dv]))
    out = jnp.concatenate([jnp.concatenate(outs[0], axis=0), jnp.concatenate(outs[1], axis=0)], axis=1)
    return (s_cat, q[R - 8:], k[R - 8:], v[R - 8:]), (out,)


def _gdn_ops(p, gb, gbc, conv_w, ad, gnw):
    TP = p.shape[0]
    dk = GDN_DK
    nqk = gb.shape[0]
    R = ROW_TILE
    cw = lambda w, off: (conv_w, (GDN_CONV, w), (lambda o: (0, off + o)), (GDN_CONV, nqk * w), (lambda o: (0, o)))
    params = [cw(dk, 0), cw(dk, nqk), cw(2 * dk, nqk), (ad, (None, 4, LANES), lambda o: (o, 0, 0)), _whole(gnw)]
    col = lambda w, off: (p, (R, w), (lambda o, s: (s, off + o)), (TP, nqk * w), (lambda o, s: (s, o)))
    xs = [col(dk, 0), col(dk, nqk), col(2 * dk, nqk), col(2 * dk, 2 * nqk), (gb, (None, 4, R), lambda o, s: (o, 0, s)),
          (gbc, (None, R, 4), lambda o, s: (o, s, 0))]
    states = [((dk, 2 * dk), F32), ((8, dk), F32), ((8, dk), F32), ((8, 2 * dk), F32)]
    ys = [((TP, 2 * nqk * dk), F32, (R, 2 * dk), lambda o, s: (s, o))]
    return nqk, TP // R, params, xs, states, ys


def gdn_fwd(h, nw1, w_in_t, w_out, conv_w, a_log, dt_bias, gnw):
    D = h.shape[1]
    nqk = D // GDN_DK
    u = rms_fwd("gdn_norm", h, nw1, BF16)
    p = mm_plain("gdn_in", 'nt', u, w_in_t)
    gates = p[:, 6 * D:6 * D + 4 * nqk]
    gb = gates.T.reshape(2, nqk, 2, -1).transpose(1, 0, 2, 3).reshape(nqk, 4, -1)
    gbc = gates.reshape(-1, 2, nqk, 2).transpose(2, 0, 1, 3).reshape(nqk, -1, 4)
    ad = jnp.concatenate([a_log.reshape(nqk, 2), dt_bias.reshape(nqk, 2)], axis=1)
    ad = jnp.broadcast_to(ad[:, :, None], (nqk, 4, LANES))
    no, n, params, xs, states, ys = _gdn_ops(p, gb, gbc, conv_w, ad, gnw)
    (act,), saved = scan_fwd("gdn_core", _gdn_step, no, n, params, [], xs, states, ys)
    h2 = mm_call("gdn_out", 'nn', [(act, w_out, 0)], 1, lambda accs, ex: [ex[0] + accs[0]], [F32], extras=[(h, 'mn')])[0]
    return h2, (h, u, p, gb, gbc, ad, saved, act)


def gdn_bwd(dh2, res, nw1, w_in_t, w_out, conv_w, gnw, dep):
    h, u, p, gb, gbc, ad, saved, act = res
    TP, D = h.shape
    nqk = D // GDN_DK
    dh2, dhb = dh2
    dact = mm_plain("gdn_dact", 'nt', dhb, w_out, dep=dep)
    dw_out = mm_plain("gdn_dwout", 'tn', act, dhb, BF16)
    no, n, params, xs, states, ys = _gdn_ops(p, gb, gbc, conv_w, ad, gnw)
    (dq, dk, dv, dz, dgb, dgbc), (dcq, dck, dcv, dad, dgnw) = scan_bwd(
        "gdn_core_bwd", _gdn_step, no, n, params, [], xs, states, saved, [(dact, ys[0][2], ys[0][3])],
        [False, False, False, False, True])
    dgates = (dgb.reshape(nqk, 2, 2, TP).transpose(1, 0, 2, 3).reshape(4 * nqk, TP).T
              + dgbc.reshape(nqk, TP, 2, 2).transpose(1, 2, 0, 3).reshape(TP, 4 * nqk))
    pad = p.shape[1] - (6 * D + 4 * nqk)
    dp = jnp.concatenate([dq, dk, dv, dz, dgates, jnp.zeros((TP, pad), F32)], axis=1)
    du = mm_plain("gdn_du", 'nn', dp, w_in_t)
    dw_in_t = mm_plain("gdn_dwin", 'tn', dp, u, BF16)
    dh, dnw1 = rms_bwd("gdn_dnorm", h, nw1, du, dh2)
    dad = dad[:, :, 0]
    return dh, dict(nw=dnw1, w_in_t=dw_in_t, w_out=dw_out, conv_w=jnp.concatenate([dcq, dck, dcv], axis=1),
                    a_log=dad[:, :2].reshape(1, 2 * nqk), dt_bias=dad[:, 2:].reshape(1, 2 * nqk), norm_w=dgnw)


def _rope(x, cos, sin):
    W = x.shape[1]
    first_half = (_iota(x.shape, 1) & (SWA_DH - 1)) < SWA_DH // 2
    rot = jnp.where(first_half, -_lane_roll(W - SWA_DH // 2)(x), _lane_roll(SWA_DH // 2)(x))
    return x * jnp.tile(cos, (1, W // LANES)) + rot * jnp.tile(sin, (1, W // LANES))


def _swa_step(params, state, xs, consts):
    (sinks,) = params
    kprev, vprev = state
    q, k, v = xs
    cos, sin, pos = consts
    R = q.shape[0]
    hkv = k.shape[1] // SWA_DH
    G = SWA_GROUP
    qr, kr = _rope(q, cos, sin), _rope(k, cos, sin)
    k2, v2 = jnp.concatenate([kprev, kr], axis=0), jnp.concatenate([vprev, v], axis=0)
    lane = _iota((R, LANES), 1)
    qpos = jnp.concatenate([pos] * G, axis=0)
    kpos = pos[0:1, 0:1] - R + _iota((1, 2 * R), 1)
    mask = (kpos <= qpos) & (qpos - kpos < SWA_WINDOW) & (kpos >= 0)
    sel_r, sel_c = _iota((hkv * SWA_DH, LANES), 0), _iota((hkv * SWA_DH, LANES), 1)
    out = []
    for hh in range(hkv):
        sel = _bf((sel_r == hh * SWA_DH + (sel_c & (SWA_DH - 1))).astype(F32))
        kd, vd = _mm(k2, sel), _mm(v2, sel)
        q8 = []
        for i in range(G // 2):
            qb = qr[:, (hh * G // 2 + i) * LANES:(hh * G // 2 + i + 1) * LANES]
            q8 += [jnp.where(lane < SWA_DH, qb, 0.0), jnp.where(lane >= SWA_DH, qb, 0.0)]
        s = _mm_nt(jnp.concatenate(q8, axis=0), kd) * (SWA_DH ** -0.5)
        s = jnp.where(mask, s, -jnp.inf)
        sink = jnp.concatenate([jnp.broadcast_to(sinks[0:1, hh * G + g:hh * G + g + 1], (R, 1)) for g in range(G)], axis=0)
        m = lax.stop_gradient(jnp.maximum(jnp.max(s, axis=1, keepdims=True), sink))
        e = jnp.exp(s - m)
        prob = e / (jnp.sum(e, axis=1, keepdims=True) + jnp.exp(sink - m))
        o8 = _mm(prob, vd)
        for i in range(G // 2):
            out.append(jnp.where(lane < SWA_DH, o8[2 * i * R:(2 * i + 1) * R], o8[(2 * i + 1) * R:(2 * i + 2) * R]))
    return (kr, v), (jnp.concatenate(out, axis=1),)


def _swa_ops(p, sinks, cos, sin, pos):
    TP = p.shape[0]
    R = ROW_TILE
    hq = sinks.shape[1]
    wq, wkv = hq * SWA_DH, hq // SWA_GROUP * SWA_DH
    nb = wq // wkv
    xs = [(p, (R, wq), (lambda o, s: (s, 0)), (TP, wq), (lambda o, s: (s, 0))),
          (p, (R, wkv), (lambda o, s: (s, nb)), (TP, wkv), (lambda o, s: (s, 0))),
          (p, (R, wkv), (lambda o, s: (s, nb + 1)), (TP, wkv), (lambda o, s: (s, 0)))]
    consts = [_rows(cos, R), _rows(sin, R), _rows(pos, R)]
    states = [((R, wkv), F32), ((R, wkv), F32)]
    ys = [((TP, wq), F32, (R, wq), lambda o, s: (s, 0))]
    return TP // R, [_whole(sinks)], consts, xs, states, ys


def swa_fwd(h, nw1, w_qkv_t, b_qkv, w_out, b_out, sinks, cos, sin, pos):
    u = rms_fwd("swa_norm", h, nw1, BF16)
    p = mm_call("swa_in", 'nt', [(u, w_qkv_t, 0)], 1, lambda accs, ex: [accs[0] + ex[0]], [F32], extras=[(b_qkv, 'n')])[0]
    n, params, consts, xs, states, ys = _swa_ops(p, sinks, cos, sin, pos)
    (act,), saved = scan_fwd("swa_core", _swa_step, 1, n, params, consts, xs, states, ys)
    h2 = mm_call("swa_out", 'nn', [(act, w_out, 0)], 1, lambda accs, ex: [ex[0] + accs[0] + ex[1]], [F32],
                 extras=[(h, 'mn'), (b_out, 'n')])[0]
    return h2, (h, u, p, saved, act)


def swa_bwd(dh2, res, nw1, w_qkv_t, w_out, sinks, cos, sin, pos, dep):
    h, u, p, saved, act = res
    dh2, dhb = dh2
    dact = mm_plain("swa_dact", 'nt', dhb, w_out, dep=dep)
    dw_out = mm_plain("swa_dwout", 'tn', act, dhb, BF16)
    db_out = colsum("swa_dbout", dh2)
    n, params, consts, xs, states, ys = _swa_ops(p, sinks, cos, sin, pos)
    (dq, dk, dv), (dsinks,) = scan_bwd("swa_core_bwd", _swa_step, 1, n, params, consts, xs, states, saved,
                                       [(dact, ys[0][2], ys[0][3])], [True])
    dp = jnp.concatenate([dq, dk, dv], axis=1)
    db_qkv = colsum("swa_dbqkv", dp)
    du = mm_plain("swa_du", 'nn', dp, w_qkv_t)
    dw_qkv_t = mm_plain("swa_dwqkv", 'tn', dp, u, BF16)
    dh, dnw1 = rms_bwd("swa_dnorm", h, nw1, du, dh2)
    return dh, dict(nw=dnw1, w_qkv_t=dw_qkv_t, w_out=dw_out, b_qkv=db_qkv, b_out=db_out, sinks=dsinks)


def _dev_index(dev):
    return 4 * dev[0] + 2 * dev[1] + dev[2]


def all_gather(name, shards):
    n = len(shards)

    def body(*refs):
        x_refs, out_refs = refs[:n], refs[n:2 * n]
        send_sems, recv_sems, local_sem = refs[2 * n:]
        x, y, c = lax.axis_index("x"), lax.axis_index("y"), lax.axis_index("c")
        me, sibling = (x, y, c), (x, y, 1 - c)
        chips = [(1 - x, y), (x, 1 - y), (1 - x, 1 - y)]

        def copy(a, k, block, to, src=None):
            dst = out_refs[a].at[_dev_index(block)]
            return pltpu.make_async_remote_copy(src_ref=dst if src is None else src, dst_ref=dst,
                                                send_sem=send_sems.at[a, k], recv_sem=recv_sems.at[a, k],
                                                device_id=to, device_id_type=MESH_ID)

        mine = [pltpu.make_async_copy(x_refs[a], out_refs[a].at[_dev_index(me)], local_sem.at[a]) for a in range(n)]
        first, passed = [], []
        for a in range(n):
            mine[a].start()
            first += [copy(a, 0, me, sibling, src=x_refs[a])]
            first += [copy(a, 1 + j, me, (*chip, c), src=x_refs[a]) for j, chip in enumerate(chips)]
        for cp in first:
            cp.start()
        for a in range(n):
            for j, chip in enumerate(chips):
                copy(a, 1 + j, (*chip, c), me).wait_recv()
                fwd = copy(a, 4 + j, (*chip, c), sibling)
                fwd.start()
                passed.append(fwd)
        for a in range(n):
            copy(a, 0, sibling, me).wait_recv()
            for j, chip in enumerate(chips):
                copy(a, 4 + j, (*chip, 1 - c), me).wait_recv()
        for cp in first + passed:
            cp.wait_send()
        for cp in mine:
            cp.wait()

    any_spec = pl.BlockSpec(memory_space=pl.ANY)
    return pl.pallas_call(
        body, in_specs=[any_spec] * n, out_specs=[any_spec] * n,
        out_shape=[jax.ShapeDtypeStruct((N_DEV,) + s.shape, s.dtype) for s in shards],
        scratch_shapes=[pltpu.SemaphoreType.DMA((n, 7)), pltpu.SemaphoreType.DMA((n, 7)), pltpu.SemaphoreType.DMA((n,))],
        name=name)(*shards)


def exchange_blocks(name, fulls):
    n = len(fulls)

    def body(*refs):
        g_refs, out_refs = refs[:n], refs[n:2 * n]
        send_sems, recv_sems, local_sem = refs[2 * n:]
        x, y, c = lax.axis_index("x"), lax.axis_index("y"), lax.axis_index("c")
        me = (x, y, c)
        peers = [(1 - x if r & 4 else x, 1 - y if r & 2 else y, 1 - c if r & 1 else c) for r in range(1, N_DEV)]

        def copy(a, k, peer):
            return pltpu.make_async_remote_copy(src_ref=g_refs[a].at[_dev_index(peer)], dst_ref=out_refs[a].at[_dev_index(me)],
                                                send_sem=send_sems.at[a, k], recv_sem=recv_sems.at[a, k],
                                                device_id=peer, device_id_type=MESH_ID)

        mine = [pltpu.make_async_copy(g_refs[a].at[_dev_index(me)], out_refs[a].at[_dev_index(me)], local_sem.at[a])
                for a in range(n)]
        sends = [copy(a, k, peer) for a in range(n) for k, peer in enumerate(peers)]
        for cp in mine + sends:
            cp.start()
        for a in range(n):
            for k, peer in enumerate(peers):
                pltpu.make_async_remote_copy(src_ref=g_refs[a].at[_dev_index(peer)], dst_ref=out_refs[a].at[_dev_index(peer)],
                                             send_sem=send_sems.at[a, k], recv_sem=recv_sems.at[a, k],
                                             device_id=peer, device_id_type=MESH_ID).wait_recv()
        for cp in sends:
            cp.wait_send()
        for cp in mine:
            cp.wait()

    any_spec = pl.BlockSpec(memory_space=pl.ANY)
    return pl.pallas_call(
        body, in_specs=[any_spec] * n, out_specs=[any_spec] * n,
        out_shape=[jax.ShapeDtypeStruct(g.shape, g.dtype) for g in fulls],
        scratch_shapes=[pltpu.SemaphoreType.DMA((n, 7)), pltpu.SemaphoreType.DMA((n, 7)), pltpu.SemaphoreType.DMA((n,))],
        name=name)(*fulls)


def _peers_of(x, y, c):
    return [(1 - x if r & 4 else x, 1 - y if r & 2 else y, 1 - c if r & 1 else c) for r in range(1, N_DEV)]


def push_start(name, srcs, lands, after, gather):
    n = len(srcs)

    def body(*refs):
        src_refs, land_refs = refs[:n], refs[n:2 * n]
        send_sems, recv_sems = refs[2 * n + 1], refs[2 * n + 2]
        token = refs[-1]
        x, y, c = lax.axis_index("x"), lax.axis_index("y"), lax.axis_index("c")
        me = (x, y, c)
        for a in range(n):
            for k, peer in enumerate(_peers_of(x, y, c)):
                pltpu.make_async_remote_copy(
                    src_ref=src_refs[a] if gather else src_refs[a].at[_dev_index(peer)],
                    dst_ref=land_refs[a].at[_dev_index(me)], send_sem=send_sems.at[a * (N_DEV - 1) + k],
                    recv_sem=recv_sems.at[a * (N_DEV - 1) + k],
                    device_id=peer, device_id_type=MESH_ID).start()
        token[...] = jnp.zeros_like(token)

    hbm = pl.BlockSpec(memory_space=pltpu.HBM)
    sem = pl.BlockSpec(memory_space=pltpu.SEMAPHORE)
    outs = pl.pallas_call(
        body, name=name,
        out_shape=(pltpu.SemaphoreType.DMA((n * (N_DEV - 1),)), pltpu.SemaphoreType.DMA((n * (N_DEV - 1),)),
                   *[pltpu.HBM(s.shape, s.dtype) for s in srcs], *[pltpu.HBM(l.shape, l.dtype) for l in lands],
                   jax.ShapeDtypeStruct((8, LANES), F32)),
        in_specs=[hbm] * (2 * n) + [pl.BlockSpec(memory_space=pl.ANY)],
        out_specs=(sem, sem, *[hbm] * (2 * n), pl.BlockSpec(memory_space=pltpu.VMEM)),
        input_output_aliases={i: 2 + i for i in range(2 * n)},
        compiler_params=pltpu.CompilerParams(has_side_effects=pltpu.SideEffectType.DATAFLOW_SIDE_EFFECTING),
    )(*[pltpu.with_memory_space_constraint(s, pltpu.HBM) for s in srcs],
      *[pltpu.with_memory_space_constraint(l, pltpu.HBM) for l in lands], after)
    return (outs[0], outs[1], outs[2:2 + n], outs[2 + n:2 + 2 * n], gather), outs[-1]


def push_wait(name, handle, after):
    send_sems, recv_sems, srcs, lands, gather = handle
    n = len(srcs)

    def body(*refs):
        src_refs, land_refs = refs[:n], refs[n:2 * n]
        send_sem_ref, recv_sem_ref = refs[2 * n], refs[2 * n + 1]
        x, y, c = lax.axis_index("x"), lax.axis_index("y"), lax.axis_index("c")
        for a in range(n):
            for k, peer in enumerate(_peers_of(x, y, c)):
                cp = pltpu.make_async_remote_copy(
                    src_ref=src_refs[a] if gather else src_refs[a].at[_dev_index(peer)],
                    dst_ref=land_refs[a].at[_dev_index(peer)], send_sem=send_sem_ref.at[a * (N_DEV - 1) + k],
                    recv_sem=recv_sem_ref.at[a * (N_DEV - 1) + k],
                    device_id=peer, device_id_type=MESH_ID)
                cp.wait_send()
                cp.wait_recv()

    hbm = pl.BlockSpec(memory_space=pltpu.HBM)
    sem = pl.BlockSpec(memory_space=pltpu.SEMAPHORE)
    outs = pl.pallas_call(
        body, name=name,
        out_shape=(*[pltpu.HBM(s.shape, s.dtype) for s in srcs], *[pltpu.HBM(l.shape, l.dtype) for l in lands]),
        in_specs=[hbm] * (2 * n) + [sem, sem, pl.BlockSpec(memory_space=pl.ANY)],
        out_specs=tuple([hbm] * (2 * n)),
        input_output_aliases={i: i for i in range(2 * n)},
        compiler_params=pltpu.CompilerParams(has_side_effects=pltpu.SideEffectType.DATAFLOW_SIDE_EFFECTING),
    )(*srcs, *lands, send_sems, recv_sems, after)
    return outs[n:]


def _own_block(block, me):
    land = lax.empty((N_DEV,) + block.shape, block.dtype)
    return lax.dynamic_update_slice(land, block[None], (me,) + (0,) * block.ndim)


def sum_blocks(name, parts):
    _, r, c = parts.shape
    tr = _pick(r, ROW_TILES)

    def body(p_ref, o_ref):
        acc = p_ref[0].astype(F32)
        for b in range(1, N_DEV):
            acc = acc + p_ref[b].astype(F32)
        o_ref[...] = acc

    return pl.pallas_call(
        body, grid=(r // tr,), in_specs=[pl.BlockSpec((N_DEV, tr, c), lambda i: (0, i, 0))],
        out_specs=pl.BlockSpec((tr, c), lambda i: (i, 0)), out_shape=jax.ShapeDtypeStruct((r, c), F32),
        compiler_params=pltpu.CompilerParams(dimension_semantics=("arbitrary",), vmem_limit_bytes=VMEM_LIMIT),
        name=name)(parts)


def adamw(name, w, g, m, v):
    r, c = w.shape
    tr = _pick(r, [t for t in ROW_TILES if t * c * 4 * 7 * 2 <= VMEM_LIMIT // 2])

    def body(w_ref, g_ref, m_ref, v_ref, d_ref, nm_ref, nv_ref):
        gv = g_ref[...]
        nm = ADAM_B1 * m_ref[...] + (1.0 - ADAM_B1) * gv
        nv = ADAM_B2 * v_ref[...] + (1.0 - ADAM_B2) * jnp.square(gv)
        m_hat = nm / (1.0 - ADAM_B1 ** ADAM_STEP)
        v_hat = nv / (1.0 - ADAM_B2 ** ADAM_STEP)
        d_ref[...] = -ADAM_LR * (m_hat / (jnp.sqrt(v_hat) + ADAM_EPS) + ADAM_WD * w_ref[...])
        nm_ref[...] = nm
        nv_ref[...] = nv

    spec = pl.BlockSpec((tr, c), lambda i: (i, 0))
    return pl.pallas_call(
        body, grid=(r // tr,), in_specs=[spec] * 4, out_specs=[spec] * 3,
        out_shape=[jax.ShapeDtypeStruct((r, c), F32)] * 3,
        compiler_params=pltpu.CompilerParams(dimension_semantics=("arbitrary",), vmem_limit_bytes=VMEM_LIMIT),
        name=name)(w, g, m, v)


def _pack(arrays):
    flat = jnp.concatenate([a.reshape(-1) for a in arrays])
    n = _round_up(flat.shape[0], 8 * LANES)
    return jnp.pad(flat, (0, n - flat.shape[0])).reshape(-1, LANES)


def _unpack(packed, shapes):
    flat = packed.reshape(-1)
    out, o = [], 0
    for sh in shapes:
        sz = 1
        for d in sh:
            sz *= d
        out.append(flat[o:o + sz].reshape(sh))
        o += sz
    return out


def _gather_axis(g, ax):
    g = jnp.moveaxis(g, 0, ax)
    return g.reshape(g.shape[:ax] + (g.shape[ax] * g.shape[ax + 1],) + g.shape[ax + 2:])


def _comm_rows(a):
    r = a.shape[0]
    rp = r if r % 16 == 0 else _round_up(r, ROW_PAD)
    return jnp.pad(_bf(a), ((0, rp - r), (0, 0)))


def _natural(g, r, pad_to=None):
    full = g[:, :r].reshape(N_DEV * r, g.shape[2])
    if pad_to is not None and full.shape[0] % pad_to:
        full = jnp.pad(full, ((0, _round_up(full.shape[0], pad_to) - full.shape[0]), (0, 0)))
    return full


def _blocked(full, r):
    blocks = full[:N_DEV * r].reshape(N_DEV, r, full.shape[1])
    rp = r if r % 16 == 0 else _round_up(r, ROW_PAD)
    return jnp.pad(_bf(blocks), ((0, 0), (0, rp - r), (0, 0)))


def kernel(x, *rest):
    nw_ = len(WEIGHTS)
    W = dict(zip(WEIGHTS, rest[:nw_]))
    loss_target = rest[nw_]
    M = dict(zip(WEIGHTS, rest[nw_ + 1:2 * nw_ + 1]))
    V = dict(zip(WEIGHTS, rest[2 * nw_ + 1:3 * nw_ + 1]))

    T0, D = x.shape[1], x.shape[2]
    T = T0 + N_META
    TP = _round_up(T, ROW_TILE)
    me = 4 * lax.axis_index("x") + 2 * lax.axis_index("y") + lax.axis_index("c")

    small_sharded = [k for k in WEIGHTS if k in SMALL and SMALL[k] is not None]
    (sg,) = all_gather("ag_small", [_pack([W[k] for k in small_sharded])])
    per_dev = [_unpack(sg[b], [W[k].shape for k in small_sharded]) for b in range(N_DEV)]
    full = {k: _gather_axis(jnp.stack([per_dev[b][i] for b in range(N_DEV)]), SMALL[k]) for i, k in enumerate(small_sharded)}
    for k in SMALL:
        if SMALL[k] is None:
            full[k] = W[k]

    r_ff = W['ffn_w_gate'].shape[3]
    r_mi, r_mo = W['mlstm_w_in'].shape[2], W['mlstm_w_out'].shape[1]
    r_gi, r_go = W['gdn_w_in'].shape[2], W['gdn_w_out'].shape[1]
    r_si, r_so = W['swa_w_qkv'].shape[2], W['swa_w_out'].shape[1]
    pw = W['pool_w'][0]
    stages = [(i, j) for i in range(DEPTH) for j in range(3)]

    def stage_shards(i, j):
        if j != 1:
            s = j // 2
            return [_comm_rows(W['ffn_w_gate'][i, s].T), _comm_rows(W['ffn_w_up'][i, s].T), _comm_rows(W['ffn_w_down'][i, s])]
        if i % 4 == 0:
            return [_comm_rows(W['mlstm_w_in'][0].T), _comm_rows(W['mlstm_w_out'][0])]
        if i % 4 == 1:
            return [_bf(pw.reshape(-1, pw.shape[2]))]
        if i % 4 == 2:
            return [_comm_rows(W['gdn_w_in'][0].T), _comm_rows(W['gdn_w_out'][0])]
        return [_comm_rows(W['swa_w_qkv'][0].T), _comm_rows(W['swa_w_out'][0])]

    def stage_weights(i, j, g):
        if j != 1:
            return tuple(_natural(a, r_ff) for a in g)
        if i % 4 == 0:
            return (_natural(g[0], r_mi, IN_PAD), _natural(g[1], r_mo))
        if i % 4 == 1:
            return (g[0].reshape(N_DEV, N_POOL, pw.shape[1], pw.shape[2]).transpose(1, 0, 2, 3)
                    .reshape(N_POOL, pw.shape[2], pw.shape[2]).astype(F32),)
        if i % 4 == 2:
            return (_natural(g[0], r_gi, IN_PAD), _natural(g[1], r_go))
        return (_natural(g[0], r_si), _natural(g[1], r_so))

    def gather_start(k, after):
        sh = stage_shards(*stages[k])
        return push_start(f"ag_start_{k}", sh, [_own_block(s, me) for s in sh], after, True)

    first = all_gather("ag_stage_0", stage_shards(*stages[0]))
    wts = {0: stage_weights(*stages[0], first)}
    pending, zero = {}, jnp.zeros((), F32)
    ahead = 4
    tok, after = zero, first[0]
    for k in range(1, ahead + 1):
        pending[k], after = gather_start(k, after)
        tok = tok + after[0, 0]

    pos = jnp.arange(TP, dtype=jnp.int32)[:, None]
    inv = ROPE_THETA ** (-jnp.arange(0, SWA_DH, 2, dtype=F32) / SWA_DH)
    ang = jnp.arange(TP, dtype=F32)[:, None] * inv[None, :]
    ang = jnp.concatenate([ang, ang, ang, ang], axis=1)
    cos, sin = jnp.cos(ang), jnp.sin(ang)
    row_mask = ((pos >= N_META) & (pos < T)).astype(F32)
    tgt = jnp.pad(loss_target[0], ((N_META, TP - T), (0, 0)))
    nrm = lambda i, j: full['norm_w'][i, j][None, :]

    h = jnp.concatenate([full['meta_tokens'], x[0], jnp.zeros((TP - T, D), F32)], axis=0)
    res = {}
    for k, (i, j) in enumerate(stages):
        if k >= 1:
            wts[k] = stage_weights(i, j, push_wait(f"ag_wait_{k}", pending.pop(k), h))
            tok = zero
            if k + ahead < len(stages):
                pending[k + ahead], t = gather_start(k + ahead, h)
                tok = t[0, 0]
        nw = nrm(i, j) + tok
        w = wts[k]
        if j != 1:
            h, res[k] = ffn_fwd(f"ffn_{i}_{j // 2}", h, nw, *w)
        elif i % 4 == 0:
            h, res[k] = mlstm_fwd(h, nw, w[0], w[1], full['mlstm_b_if'], full['mlstm_norm_w'])
        elif i % 4 == 1:
            h, res[k] = pool_fwd(h, nw, w[0], full['pool_scale'], pos)
        elif i % 4 == 2:
            h, res[k] = gdn_fwd(h, nw, w[0], w[1], full['gdn_conv_w'][0], full['gdn_a_log'], full['gdn_dt_bias'],
                                full['gdn_norm_w'])
        else:
            h, res[k] = swa_fwd(h, nw, w[0], full['swa_b_qkv'], w[1], full['swa_b_out'], full['swa_sinks'], cos, sin, pos)

    loss_local, dh_f32, d_final, dh_bf = loss_call(h, full['final_norm_w'][None, :], tgt, row_mask)
    dh = (dh_f32, dh_bf)
    loss = lax.psum(loss_local[0, 0], ("x", "y", "c"))

    gs = {'final_norm_w': d_final[0]}
    d_norm = [[None] * 3 for _ in range(DEPTH)]
    sent = []
    dep = jnp.zeros((8, LANES), F32)

    def exchange(k, part, blocks, after):
        lands = [_own_block(lax.dynamic_index_in_dim(b, me, 0, keepdims=False), me) for b in blocks]
        handle, token = push_start(f"rs_start_{k}_{part}", blocks, lands, after, False)
        sent.append((k, part, handle))
        return token

    for k in reversed(range(len(stages))):
        i, j = stages[k]
        nw, w = nrm(i, j), wts[k]
        if j != 1:
            send = lambda part, gs_, k=k: exchange(k, part, [_blocked(g, r_ff) for g in gs_], gs_[0])
            dh, d_norm[i][j], dep = ffn_bwd(f"ffn_{i}_{j // 2}", dh, res[k], nw, *w, dep, send)
            continue
        else:
            if i % 4 == 0:
                dh, gm = mlstm_bwd(dh, res[k], nw, w[0], w[1], full['mlstm_norm_w'], dep)
                blocks = [_blocked(gm['w_in_t'], r_mi), _blocked(gm['w_out'], r_mo)]
                gs.update(mlstm_b_if=gm['b_if'], mlstm_norm_w=gm['norm_w'])
            elif i % 4 == 1:
                dh, gm = pool_bwd(dh, res[k], nw + dep[0, 0], w[0], full['pool_scale'], pos)
                dw = gm['w'].reshape(N_POOL, N_DEV, pw.shape[1], pw.shape[2]).transpose(1, 0, 2, 3)
                blocks = [_bf(dw.reshape(N_DEV, N_POOL * pw.shape[1], pw.shape[2]))]
                gs.update(pool_scale=gm['scale'])
            elif i % 4 == 2:
                dh, gm = gdn_bwd(dh, res[k], nw, w[0], w[1], full['gdn_conv_w'][0], full['gdn_norm_w'], dep)
                blocks = [_blocked(gm['w_in_t'], r_gi), _blocked(gm['w_out'], r_go)]
                gs.update(gdn_conv_w=gm['conv_w'][None], gdn_a_log=gm['a_log'], gdn_dt_bias=gm['dt_bias'],
                          gdn_norm_w=gm['norm_w'])
            else:
                dh, gm = swa_bwd(dh, res[k], nw, w[0], w[1], full['swa_sinks'], cos, sin, pos, dep)
                blocks = [_blocked(gm['w_qkv_t'], r_si), _blocked(gm['w_out'], r_so)]
                gs.update(swa_b_qkv=gm['b_qkv'], swa_b_out=gm['b_out'], swa_sinks=gm['sinks'])
            d_norm[i][1] = gm['nw']
        dep = exchange(k, 'm', blocks, dh[0])
    gs['norm_w'] = jnp.stack([jnp.concatenate(d_norm[i], axis=0) for i in range(DEPTH)])
    gs['meta_tokens'] = dh[0][:N_META]
    grad_x = dh[0][N_META:T][None]

    grads = {}
    small_names = [k for k in WEIGHTS if k in SMALL]
    (parts,) = all_gather("ag_small_grads", [_pack([gs[k].reshape(full[k].shape) for k in small_names])])
    tot = _unpack(sum_blocks("sum_small_grads", parts), [full[k].shape for k in small_names])
    for k, g in zip(small_names, tot):
        ax = SMALL[k]
        grads[k] = g if ax is None else lax.dynamic_slice_in_dim(g, me * W[k].shape[ax], W[k].shape[ax], axis=ax)

    summed, after = {}, dh[0]
    for k, part, handle in sent:
        parts = push_wait(f"rs_wait_{k}_{part}", handle, after)
        summed[k, part] = [sum_blocks(f"sum_{k}_{part}_{n}", p) for n, p in enumerate(parts)]
        after = summed[k, part][0]

    gg, gu, gd = [], [], []
    for i in range(DEPTH):
        for j in (0, 2):
            a, b = summed[3 * i + j, 'gu']
            gg.append(a[:r_ff].T)
            gu.append(b[:r_ff].T)
            gd.append(summed[3 * i + j, 'd'][0][:r_ff])
    shape4 = lambda lst, ref: jnp.stack(lst).reshape(ref.shape)
    grads['ffn_w_gate'] = shape4(gg, W['ffn_w_gate'])
    grads['ffn_w_up'] = shape4(gu, W['ffn_w_up'])
    grads['ffn_w_down'] = shape4(gd, W['ffn_w_down'])
    for i in range(DEPTH):
        g = summed[3 * i + 1, 'm']
        if i % 4 == 0:
            grads['mlstm_w_in'], grads['mlstm_w_out'] = g[0][:r_mi].T[None], g[1][:r_mo][None]
        elif i % 4 == 1:
            grads['pool_w'] = g[0].reshape(W['pool_w'].shape)
        elif i % 4 == 2:
            grads['gdn_w_in'], grads['gdn_w_out'] = g[0][:r_gi].T[None], g[1][:r_go][None]
        else:
            grads['swa_w_qkv'], grads['swa_w_out'] = g[0][:r_si].T[None], g[1][:r_so][None]

    delta, new_m, new_v = {}, {}, {}
    shapes = [W[k].shape for k in small_names]
    d, nm, nv = adamw("adamw_small", _pack([W[k] for k in small_names]), _pack([grads[k] for k in small_names]),
                      _pack([M[k] for k in small_names]), _pack([V[k] for k in small_names]))
    for k, a, b, c_ in zip(small_names, _unpack(d, shapes), _unpack(nm, shapes), _unpack(nv, shapes)):
        delta[k], new_m[k], new_v[k] = a, b, c_
    for k in WEIGHTS:
        if k not in SMALL:
            two = lambda a: a.reshape(-1, a.shape[-1])
            d, nm, nv = adamw("adamw_" + k, two(W[k]), two(grads[k]), two(M[k]), two(V[k]))
            delta[k], new_m[k], new_v[k] = d.reshape(W[k].shape), nm.reshape(W[k].shape), nv.reshape(W[k].shape)

    return (loss, grad_x, *[grads[k].reshape(W[k].shape) for k in WEIGHTS], *[delta[k] for k in WEIGHTS],
            *[new_m[k] for k in WEIGHTS], *[new_v[k] for k in WEIGHTS])
```

```python
import functools

import jax
import jax.numpy as jnp
from jax import lax
from jax.experimental import pallas as pl
from jax.experimental.pallas import tpu as pltpu

F32 = jnp.float32
BF16 = jnp.bfloat16

N_DEV = 8
N_META = 16
EPS = 1e-6
DEPTH = 4
MLSTM_HEADS = 8
MLSTM_CHUNK = 64
N_POOL = 4
GDN_DK = 128
GDN_CHUNK = 64
GDN_CONV = 4
GDN_HEADS_PER_STEP = 2
SWA_DH = 64
SWA_GROUP = 8
SWA_WINDOW = 128
ROPE_THETA = 10000.0
ADAM_LR = 0.001
ADAM_B1 = 0.9
ADAM_B2 = 0.999
ADAM_EPS = 1e-08
ADAM_WD = 0.01
ADAM_STEP = 10

LANES = 128
ROW_TILE = 128
ROW_PAD = 112
IN_PAD = 896
VMEM_LIMIT = 56 * 1024 * 1024
TOKEN_TILES = (1056, 768, 512, 384, 256, 128)
FEATURE_TILES = (512, 896, 768, 640, 384, 256, 128)
ROW_TILES = (512, 352, 256, 176, 160, 128, 112, 64, 48, 32, 16, 8)
HIGHEST = lax.Precision.HIGHEST
HIGH = lax.Precision.HIGH
MESH_ID = pl.DeviceIdType.MESH

WEIGHTS = ('meta_tokens', 'norm_w', 'ffn_w_gate', 'ffn_w_up', 'ffn_w_down', 'mlstm_w_in', 'mlstm_b_if',
           'mlstm_norm_w', 'mlstm_w_out', 'pool_w', 'pool_scale', 'gdn_w_in', 'gdn_conv_w', 'gdn_a_log',
           'gdn_dt_bias', 'gdn_norm_w', 'gdn_w_out', 'swa_w_qkv', 'swa_b_qkv', 'swa_sinks', 'swa_w_out',
           'swa_b_out', 'final_norm_w')
SMALL = {'meta_tokens': 1, 'norm_w': 2, 'mlstm_b_if': None, 'mlstm_norm_w': None, 'pool_scale': 1,
         'gdn_conv_w': 2, 'gdn_a_log': None, 'gdn_dt_bias': None, 'gdn_norm_w': None, 'swa_b_qkv': 1,
         'swa_sinks': None, 'swa_b_out': 1, 'final_norm_w': None}


def _pick(n, cands):
    for c in cands:
        if n % c == 0:
            return c
    return n


def _round_up(n, m):
    return -(-n // m) * m


def _bf(x):
    return x.astype(BF16)


def _dot(a, b, dims, precision=None):
    return lax.dot_general(a, b, (dims, ((), ())), preferred_element_type=F32, precision=precision)


def _mm(a, b):
    return _dot(_bf(a), _bf(b), ((1,), (0,)))


def _mm_nt(a, b):
    return _dot(_bf(a), _bf(b), ((1,), (1,)))


def _mm_tn(a, b):
    return _dot(_bf(a), _bf(b), ((0,), (0,)))


def _mm32(a, b):
    return _dot(a, b, ((1,), (0,)), precision=HIGHEST)


def _mm3(a, b):
    return _dot(a, b, ((1,), (0,)), precision=HIGH)


def _iota(shape, axis):
    return lax.broadcasted_iota(jnp.int32, shape, axis)


def _row2col(row, eye):
    return jnp.sum(eye * row, axis=1, keepdims=True)


def _lane_roll(shift):
    @jax.custom_vjp
    def f(x):
        return pltpu.roll(x, shift, 1)

    def fwd(x):
        return f(x), None

    def bwd(_, g):
        return (pltpu.roll(g, g.shape[1] - shift, 1),)

    f.defvjp(fwd, bwd)
    return f


def mm_call(name, mode, pairs, n_acc, epilogue, out_dtypes, extras=(), tm=None, tn=None, tk=None):
    a0, b0 = pairs[0][0], pairs[0][1]
    if mode == 'nn':
        (M, K), N = a0.shape, b0.shape[1]
    elif mode == 'nt':
        (M, K), N = a0.shape, b0.shape[0]
    else:
        (K, M), N = a0.shape, b0.shape[1]
    if mode == 'tn':
        tm = tm or _pick(M, FEATURE_TILES)
        tn = tn or (N if N <= 2048 else _pick(N, FEATURE_TILES))
        tk = tk or _pick(K, [t for t in (2112,) + TOKEN_TILES
                             if 2 * t * (tm * a0.dtype.itemsize + tn * b0.dtype.itemsize) <= VMEM_LIMIT // 2])
        dims = ((0,), (0,))
        a_spec = pl.BlockSpec((tk, tm), lambda i, j, k: (k, i))
        b_spec = pl.BlockSpec((tk, tn), lambda i, j, k: (k, j))
    else:
        tm = tm or _pick(M, TOKEN_TILES)
        tn = tn or _pick(N, FEATURE_TILES)
        fixed = tm * tn * 4 * (n_acc + 2 * len(out_dtypes) + 2 * sum(kind == 'mn' for _, kind in extras))
        fits = lambda t: 2 * len(pairs) * t * (tm * a0.dtype.itemsize + tn * b0.dtype.itemsize) + fixed <= VMEM_LIMIT * 7 // 8
        tk = tk or _pick(K, [t for t in (K, 2816, 2048) + FEATURE_TILES if fits(t)])
        a_spec = pl.BlockSpec((tm, tk), lambda i, j, k: (i, k))
        if mode == 'nn':
            dims = ((1,), (0,))
            b_spec = pl.BlockSpec((tk, tn), lambda i, j, k: (k, j))
        else:
            dims = ((1,), (1,))
            b_spec = pl.BlockSpec((tn, tk), lambda i, j, k: (j, k))
    n_pairs, n_ex, n_out = len(pairs), len(extras), len(out_dtypes)
    nk = K // tk

    def body(*refs):
        ab = refs[:2 * n_pairs]
        ex = refs[2 * n_pairs:2 * n_pairs + n_ex]
        outs = refs[2 * n_pairs + n_ex:2 * n_pairs + n_ex + n_out]
        accs = refs[2 * n_pairs + n_ex + n_out:]
        k = pl.program_id(2)

        @pl.when(k == 0)
        def _():
            for acc in accs:
                acc[...] = jnp.zeros_like(acc)

        for p, (_, _, ai) in enumerate(pairs):
            accs[ai][...] += _dot(_bf(ab[2 * p][...]), _bf(ab[2 * p + 1][...]), dims)

        @pl.when(k == nk - 1)
        def _():
            res = epilogue([acc[...] for acc in accs], [e[...] for e in ex])
            for o, v in zip(outs, res):
                o[...] = v.astype(o.dtype)

    ex_specs = [pl.BlockSpec((tm, tn), lambda i, j, k: (i, j)) if kind == 'mn'
                else pl.BlockSpec((1, tn), lambda i, j, k: (0, j)) if kind == 'n'
                else pl.BlockSpec(e.shape, lambda i, j, k: (0, 0)) for e, kind in extras]
    outs = pl.pallas_call(
        body, grid=(M // tm, N // tn, nk),
        in_specs=[a_spec, b_spec] * n_pairs + ex_specs,
        out_specs=[pl.BlockSpec((tm, tn), lambda i, j, k: (i, j)) for _ in out_dtypes],
        out_shape=[jax.ShapeDtypeStruct((M, N), dt) for dt in out_dtypes],
        scratch_shapes=[pltpu.VMEM((tm, tn), F32) for _ in range(n_acc)],
        compiler_params=pltpu.CompilerParams(dimension_semantics=("arbitrary",) * 3, vmem_limit_bytes=VMEM_LIMIT),
        name=name)(*[t for a, b, _ in pairs for t in (a, b)], *[e for e, _ in extras])
    return outs


def mm_plain(name, mode, a, b, dtype=F32, scale=None, dep=None):
    ep = (lambda accs, ex: [accs[0]]) if scale is None else (lambda accs, ex: [accs[0] * scale])
    return mm_call(name, mode, [(a, b, 0)], 1, ep, [dtype], extras=[] if dep is None else [(dep, 'dep')])[0]


def scan_fwd(name, step_fn, n_outer, n_steps, params, consts, xs, states, ys):
    n_p, n_c, n_x, n_s, n_y = len(params), len(consts), len(xs), len(states), len(ys)

    def body(*refs):
        p_refs = refs[:n_p]
        c_refs = refs[n_p:n_p + n_c]
        x_refs = refs[n_p + n_c:n_p + n_c + n_x]
        o = n_p + n_c + n_x
        y_refs = refs[o:o + n_y]
        sv_refs = refs[o + n_y:o + n_y + n_s]
        st_refs = refs[o + n_y + n_s:]
        s = pl.program_id(1)

        @pl.when(s == 0)
        def _():
            for r in st_refs:
                r[...] = jnp.zeros_like(r)

        st = tuple(r[...] for r in st_refs)
        for sv, v in zip(sv_refs, st):
            sv[...] = v
        new_st, y = step_fn(tuple(r[...] for r in p_refs), st, tuple(r[...] for r in x_refs),
                            tuple(r[...] for r in c_refs))
        for r, v in zip(y_refs, y):
            r[...] = v.astype(r.dtype)
        for r, v in zip(st_refs, new_st):
            r[...] = v

    in_specs = ([pl.BlockSpec(p[1], (lambda o, s, f=p[2]: f(o))) for p in params]
                + [pl.BlockSpec(c[1], c[2]) for c in consts]
                + [pl.BlockSpec(x[1], x[2]) for x in xs])
    out_specs = ([pl.BlockSpec(b, f) for _, _, b, f in ys]
                 + [pl.BlockSpec((None, None) + tuple(sh), (lambda o, s, n=len(sh): (o, s) + (0,) * n)) for sh, _ in states])
    out_shape = ([jax.ShapeDtypeStruct(sh, dt) for sh, dt, _, _ in ys]
                 + [jax.ShapeDtypeStruct((n_outer, n_steps) + tuple(sh), dt) for sh, dt in states])
    outs = pl.pallas_call(
        body, grid=(n_outer, n_steps), in_specs=in_specs, out_specs=out_specs, out_shape=out_shape,
        scratch_shapes=[pltpu.VMEM(tuple(sh), dt) for sh, dt in states],
        compiler_params=pltpu.CompilerParams(dimension_semantics=("arbitrary", "arbitrary"), vmem_limit_bytes=VMEM_LIMIT),
        name=name)(*[p[0] for p in params], *[c[0] for c in consts], *[x[0] for x in xs])
    return tuple(outs[:n_y]), tuple(outs[n_y:])


def scan_bwd(name, step_fn, n_outer, n_steps, params, consts, xs, states, saved, dys, glob, bf16_copies=()):
    n_p, n_c, n_x, n_s, n_y = len(params), len(consts), len(xs), len(states), len(dys)
    n_cp = len(bf16_copies)
    rev = lambda f: (lambda o, s: f(o, n_steps - 1 - s))

    def body(*refs):
        p_refs = refs[:n_p]
        c_refs = refs[n_p:n_p + n_c]
        x_refs = refs[n_p + n_c:n_p + n_c + n_x]
        o = n_p + n_c + n_x
        sv_refs = refs[o:o + n_s]
        dy_refs = refs[o + n_s:o + n_s + n_y]
        o = o + n_s + n_y
        dx_refs = refs[o:o + n_x]
        dp_refs = refs[o + n_x:o + n_x + n_p]
        cp_refs = refs[o + n_x + n_p:o + n_x + n_p + n_cp]
        dst_refs = refs[o + n_x + n_p + n_cp:]
        oi, s = pl.program_id(0), pl.program_id(1)

        @pl.when(s == 0)
        def _():
            for r in dst_refs:
                r[...] = jnp.zeros_like(r)

        for r, g in zip(dp_refs, glob):
            @pl.when(((s == 0) & (oi == 0)) if g else (s == 0))
            def _(r=r):
                r[...] = jnp.zeros_like(r)

        c_vals = tuple(r[...] for r in c_refs)
        f = lambda p, st, x: step_fn(p, st, x, c_vals)
        _, vjp = jax.vjp(f, tuple(r[...] for r in p_refs), tuple(r[...] for r in sv_refs), tuple(r[...] for r in x_refs))
        dp, dst, dx = vjp((tuple(r[...] for r in dst_refs), tuple(r[...] for r in dy_refs)))
        for r, v in zip(dx_refs, dx):
            r[...] = v.astype(r.dtype)
        for r, i in zip(cp_refs, bf16_copies):
            r[...] = dx[i].astype(r.dtype)
        for r, v in zip(dst_refs, dst):
            r[...] = v
        for r, v in zip(dp_refs, dp):
            r[...] += v

    gshape = lambda t: t[3] if len(t) > 3 else t[0].shape
    gidx = lambda t: t[4] if len(t) > 3 else t[2]
    in_specs = ([pl.BlockSpec(p[1], (lambda o, s, f=p[2]: f(o))) for p in params]
                + [pl.BlockSpec(c[1], rev(c[2])) for c in consts]
                + [pl.BlockSpec(x[1], rev(x[2])) for x in xs]
                + [pl.BlockSpec((None, None) + tuple(sh), (lambda o, s, n=len(sh): (o, n_steps - 1 - s) + (0,) * n)) for sh, _ in states]
                + [pl.BlockSpec(b, rev(f)) for _, b, f in dys])
    out_specs = ([pl.BlockSpec(x[1], rev(gidx(x))) for x in xs]
                 + [pl.BlockSpec(p[1], (lambda o, s, f=gidx(p): f(o))) for p in params]
                 + [pl.BlockSpec(xs[i][1], rev(gidx(xs[i]))) for i in bf16_copies])
    out_shape = ([jax.ShapeDtypeStruct(gshape(x), F32) for x in xs]
                 + [jax.ShapeDtypeStruct(gshape(p), F32) for p in params]
                 + [jax.ShapeDtypeStruct(gshape(xs[i]), BF16) for i in bf16_copies])
    outs = pl.pallas_call(
        body, grid=(n_outer, n_steps), in_specs=in_specs, out_specs=out_specs, out_shape=out_shape,
        scratch_shapes=[pltpu.VMEM(tuple(sh), dt) for sh, dt in states],
        compiler_params=pltpu.CompilerParams(dimension_semantics=("arbitrary", "arbitrary"), vmem_limit_bytes=VMEM_LIMIT),
        name=name)(*[p[0] for p in params], *[c[0] for c in consts], *[x[0] for x in xs], *saved,
                   *[d[0] for d in dys])
    if n_cp:
        return tuple(outs[:n_x]), tuple(outs[n_x:n_x + n_p]), tuple(outs[n_x + n_p:])
    return tuple(outs[:n_x]), tuple(outs[n_x:])


def _rows(a, rt):
    return (a, (rt, a.shape[1]), lambda o, s: (s, 0))


def _whole(a):
    return (a, a.shape, lambda o: (0,) * a.ndim)


def _rms(h, w):
    return h * lax.rsqrt(jnp.mean(h * h, axis=1, keepdims=True) + EPS) * w


def rms_fwd(name, h, w, dtype):
    TP, D = h.shape
    rt = _pick(TP, (384, 128))
    step = lambda p, st, x, c: ((), (_rms(x[0], p[0]),))
    (y,), _ = scan_fwd(name, step, 1, TP // rt, [_whole(w)], [], [_rows(h, rt)], [],
                       [((TP, D), dtype, (rt, D), lambda o, s: (s, 0))])
    return y


def rms_bwd(name, h, w, dxn, dres):
    TP, D = h.shape
    rt = _pick(TP, (384, 128))
    step = lambda p, st, x, c: ((), (_rms(x[0], p[0]), x[0]))
    (dh,), (dw,), (dh_bf,) = scan_bwd(name, step, 1, TP // rt, [_whole(w)], [], [_rows(h, rt)], [], [],
                                      [_rows(dxn, rt), _rows(dres, rt)], [True], bf16_copies=(0,))
    return (dh, dh_bf), dw


def loss_call(h, w, tgt, mask):
    TP, D = h.shape
    rt = _pick(TP, (384, 128))

    def body(h_ref, w_ref, t_ref, m_ref, loss_ref, dh_ref, dw_ref, dhb_ref):
        s = pl.program_id(0)

        @pl.when(s == 0)
        def _():
            loss_ref[...] = jnp.zeros_like(loss_ref)
            dw_ref[...] = jnp.zeros_like(dw_ref)

        tg, mk = t_ref[...], m_ref[...]

        def f(wv, hv):
            err = jnp.square(_rms(hv, wv) - tg) * mk
            return 0.5 * jnp.sum(jnp.sum(err, axis=1, keepdims=True), axis=0, keepdims=True) / D

        l, vjp = jax.vjp(f, w_ref[...], h_ref[...])
        dw, dh = vjp(jnp.ones((1, 1), F32))
        loss_ref[...] += l
        dw_ref[...] += dw
        dh_ref[...] = dh
        dhb_ref[...] = dh.astype(dhb_ref.dtype)

    row = lambda wd: pl.BlockSpec((rt, wd), lambda s: (s, 0))
    const = lambda shape: pl.BlockSpec(shape, lambda s: (0, 0))
    return pl.pallas_call(
        body, grid=(TP // rt,), in_specs=[row(D), const((1, D)), row(D), row(1)],
        out_specs=[const((1, 1)), row(D), const((1, D)), row(D)],
        out_shape=[jax.ShapeDtypeStruct((1, 1), F32), jax.ShapeDtypeStruct((TP, D), F32), jax.ShapeDtypeStruct((1, D), F32),
                   jax.ShapeDtypeStruct((TP, D), BF16)],
        compiler_params=pltpu.CompilerParams(dimension_semantics=("arbitrary",), vmem_limit_bytes=VMEM_LIMIT),
        name="loss_head")(h, w, tgt, mask)


def colsum(name, a):
    TP, N = a.shape
    rt = _pick(TP, (384, 128))

    def body(a_ref, o_ref):
        @pl.when(pl.program_id(0) == 0)
        def _():
            o_ref[...] = jnp.zeros_like(o_ref)
        o_ref[...] += jnp.sum(a_ref[...], axis=0, keepdims=True)

    return pl.pallas_call(
        body, grid=(TP // rt,), in_specs=[pl.BlockSpec((rt, N), lambda s: (s, 0))],
        out_specs=pl.BlockSpec((1, N), lambda s: (0, 0)), out_shape=jax.ShapeDtypeStruct((1, N), F32),
        compiler_params=pltpu.CompilerParams(dimension_semantics=("arbitrary",)), name=name)(a)


def ffn_fwd(tag, h, nw, wg_t, wu_t, wd):
    xn = rms_fwd(tag + "_norm", h, nw, BF16)

    def ep_up(accs, ex):
        g, u = accs
        return [g, u, jax.nn.silu(g) * u]

    g, u, a = mm_call(tag + "_up", 'nt', [(xn, wg_t, 0), (xn, wu_t, 1)], 2, ep_up, [BF16, BF16, BF16])
    h2 = mm_call(tag + "_down", 'nn', [(a, wd, 0)], 1, lambda accs, ex: [ex[0] + 0.5 * accs[0]], [F32],
                 extras=[(h, 'mn')])[0]
    return h2, (h, xn, g, u, a)


def ffn_bwd(tag, dh2, res, nw, wg_t, wu_t, wd, dep, send):
    h, xn, g, u, a = res
    dh2, dhb = dh2

    def ep_act(accs, ex):
        da = 0.5 * accs[0]
        gv, uv = ex[0].astype(F32), ex[1].astype(F32)
        sg = jax.nn.sigmoid(gv)
        return [da * uv * (sg * (1.0 + gv * (1.0 - sg))), da * (gv * sg)]

    dg, du = mm_call(tag + "_dact", 'nt', [(dhb, wd, 0)], 1, ep_act, [BF16, BF16],
                     extras=[(g, 'mn'), (u, 'mn'), (dep, 'dep')])
    dwd = mm_plain(tag + "_dwd", 'tn', a, dhb, BF16, scale=0.5)
    tok = send('d', [dwd])
    dwg_t = mm_plain(tag + "_dwg", 'tn', dg, xn, BF16, dep=tok)
    dwu_t = mm_plain(tag + "_dwu", 'tn', du, xn, BF16)
    tok = send('gu', [dwg_t, dwu_t])
    dxn = mm_call(tag + "_dxn", 'nn', [(dg, wg_t, 0), (du, wu_t, 0)], 1, lambda accs, ex: [accs[0]], [F32],
                  extras=[(tok, 'dep')])[0]
    dh, dnw = rms_bwd(tag + "_dnorm", h, nw, dxn, dh2)
    return dh, dnw, tok


def _mlstm_step(params, state, xs, consts, *, dk):
    bif, nw = params
    c_st, n_st, m_st = state
    q, k, v, og, gr = xs
    C = MLSTM_CHUNK
    R = q.shape[0]
    ri, ci = _iota((C, C), 0), _iota((C, C), 1)
    eye = (ri == ci).astype(F32)
    causal = ci <= ri
    upper = (ri <= ci).astype(F32)
    outs = []
    for j in range(R // C):
        sl = slice(j * C, (j + 1) * C)
        qj, kj, vj = q[sl] * (dk ** -0.5), k[sl], v[sl]
        li = gr[0:1, sl] + bif[0:1, 0:1]
        lf = jax.nn.log_sigmoid(gr[1:2, sl] + bif[1:2, 0:1])
        b_row = _mm32(lf, upper)
        b_col = _row2col(b_row, eye)
        log_w = jnp.where(causal, b_col - b_row + li, -jnp.inf)
        log_init = b_col + m_st
        m_t = lax.stop_gradient(jnp.maximum(log_init, jnp.max(log_w, axis=1, keepdims=True)))
        w = jnp.exp(log_w - m_t)
        w_init = jnp.exp(log_init - m_t)
        qk = _mm_nt(qj, kj) * w
        num = w_init * _mm(qj, c_st) + _mm(qk, vj)
        den = w_init * jnp.sum(qj * n_st, axis=1, keepdims=True) + jnp.sum(qk, axis=1, keepdims=True)
        h = num / jnp.maximum(jnp.abs(den), jnp.exp(-m_t))
        b_last = b_row[:, C - 1:C]
        log_end_init = b_last + m_st
        log_end = b_last - b_row + li
        m_new = lax.stop_gradient(jnp.maximum(log_end_init, jnp.max(log_end, axis=1, keepdims=True)))
        a_init = jnp.exp(log_end_init - m_new)
        ka = kj * _row2col(jnp.exp(log_end - m_new), eye)
        c_st = a_init * c_st + _mm_tn(ka, vj)
        n_st = a_init * n_st + jnp.sum(ka, axis=0, keepdims=True)
        m_st = m_new
        hn = h * lax.rsqrt(jnp.mean(h * h, axis=1, keepdims=True) + EPS)
        outs.append(hn * nw * jax.nn.sigmoid(og[sl]))
    return (c_st, n_st, m_st), (jnp.concatenate(outs, axis=0),)


def _mlstm_ops(p, gr, bif, nw):
    H = MLSTM_HEADS
    TP = p.shape[0]
    dv = nw.shape[1] // H
    dk = dv // 2
    R = ROW_TILE
    step = functools.partial(_mlstm_step, dk=dk)
    params = [(bif, (None, 2, LANES), lambda o: (o, 0, 0)), (nw, (1, dv), lambda o: (0, o))]
    col = lambda w, off: (p, (R, w), (lambda o, s: (s, off + o)), (TP, H * w), (lambda o, s: (s, o)))
    xs = [col(dk, 0), col(dk, H), col(dv, H), col(dv, 2 * H), (gr, (None, 2, R), lambda o, s: (o, 0, s))]
    states = [((dk, dv), F32), ((1, dk), F32), ((1, 1), F32)]
    ys = [((TP, H * dv), F32, (R, dv), lambda o, s: (s, o))]
    return step, H, TP // R, params, xs, states, ys


def mlstm_fwd(h, nw1, w_in_t, w_out, b_if, norm_w):
    H = MLSTM_HEADS
    D = h.shape[1]
    u = rms_fwd("mlstm_norm", h, nw1, BF16)
    p = mm_plain("mlstm_in", 'nt', u, w_in_t)
    gr = p[:, 3 * D:3 * D + 2 * H].T.reshape(2, H, -1).transpose(1, 0, 2)
    bif = jnp.broadcast_to(b_if.reshape(2, H).T[:, :, None], (H, 2, LANES))
    step, _, n, params, xs, states, ys = _mlstm_ops(p, gr, bif, norm_w)
    (act,), saved = scan_fwd("mlstm_core", step, H, n, params, [], xs, states, ys)
    h2 = mm_call("mlstm_out", 'nn', [(act, w_out, 0)], 1, lambda accs, ex: [ex[0] + accs[0]], [F32], extras=[(h, 'mn')])[0]
    return h2, (h, u, p, gr, bif, saved, act)


def mlstm_bwd(dh2, res, nw1, w_in_t, w_out, norm_w, dep):
    h, u, p, gr, bif, saved, act = res
    H = MLSTM_HEADS
    TP = h.shape[0]
    dh2, dhb = dh2
    dact = mm_plain("mlstm_dact", 'nt', dhb, w_out, dep=dep)
    dw_out = mm_plain("mlstm_dwout", 'tn', act, dhb, BF16)
    step, _, n, params, xs, states, ys = _mlstm_ops(p, gr, bif, norm_w)
    (dq, dk, dv, dog, dgr), (dbif, dnorm) = scan_bwd("mlstm_core_bwd", step, H, n, params, [], xs, states, saved,
                                                     [(dact, ys[0][2], ys[0][3])], [False, False])
    dgates = dgr.transpose(1, 0, 2).reshape(2 * H, TP).T
    pad = p.shape[1] - (dq.shape[1] + dk.shape[1] + dv.shape[1] + dog.shape[1] + 2 * H)
    dp = jnp.concatenate([dq, dk, dv, dog, dgates, jnp.zeros((TP, pad), F32)], axis=1)
    du = mm_plain("mlstm_du", 'nn', dp, w_in_t)
    dw_in_t = mm_plain("mlstm_dwin", 'tn', dp, u, BF16)
    dh, dnw1 = rms_bwd("mlstm_dnorm", h, nw1, du, dh2)
    db_if = dbif[:, :, 0].T.reshape(1, 2 * H)
    return dh, dict(nw=dnw1, w_in_t=dw_in_t, w_out=dw_out, b_if=db_if, norm_w=dnorm)


def _pool_step(params, state, xs, consts):
    w, scale = params
    (prev,) = state
    u, h = xs
    pos, win = consts
    R = u.shape[0]
    wn = win[0:1, 0:1]
    ext = jnp.concatenate([prev, u], axis=0)
    lag = _iota((R, 2 * R), 0) + R - _iota((R, 2 * R), 1)
    band = ((lag >= 0) & (lag < wn)).astype(F32)
    wsum = _mm32(band, ext)
    cnt = jnp.minimum(pos + 1, wn).astype(F32)
    pooled = wsum / cnt - u
    return (u,), (h + _mm(pooled, w) * scale,)


def _pool_ops(u, h, w, scale, pos):
    TP, D = u.shape
    G = D // N_POOL
    R = ROW_TILE
    win = jnp.broadcast_to(jnp.array([2 << g for g in range(N_POOL)], jnp.int32)[:, None, None], (N_POOL, 1, LANES))
    params = [(w, (None, G, G), lambda o: (o, 0, 0)), (scale, (1, G), lambda o: (0, o))]
    consts = [(pos, (R, 1), lambda o, s: (s, 0)), (win, (None, 1, LANES), lambda o, s: (o, 0, 0))]
    grp = lambda a: (a, (R, G), lambda o, s: (s, o))
    return N_POOL, TP // R, params, consts, [grp(u), grp(h)], [((R, G), F32)], [((TP, D), F32, (R, G), lambda o, s: (s, o))]


def pool_fwd(h, nw1, w, scale, pos):
    u = rms_fwd("pool_norm", h, nw1, F32)
    no, n, params, consts, xs, states, ys = _pool_ops(u, h, w, scale, pos)
    (h2,), saved = scan_fwd("pool_core", _pool_step, no, n, params, consts, xs, states, ys)
    return h2, (h, u, saved)


def pool_bwd(dh2, res, nw1, w, scale, pos):
    h, u, saved = res
    dh2 = dh2[0]
    no, n, params, consts, xs, states, ys = _pool_ops(u, h, w, scale, pos)
    (du, dres), (dw, dscale) = scan_bwd("pool_core_bwd", _pool_step, no, n, params, consts, xs, states, saved,
                                        [(dh2, ys[0][2], ys[0][3])], [False, False])
    dh, dnw1 = rms_bwd("pool_dnorm", h, nw1, du, dres)
    return dh, dict(nw=dnw1, w=dw, scale=dscale)


def _unit_lower_inverse(low, width):
    n = low.shape[0]
    ri, ci = _iota((n, n), 0), _iota((n, n), 1)
    inv = (ri == ci).astype(F32)
    b = 1
    while b < width:
        blk = 2 * b
        sh = blk.bit_length() - 1
        off = jnp.where(((ri >> sh) == (ci >> sh)) & ((ri & (blk - 1)) >= b) & ((ci & (blk - 1)) < b), low, 0.0)
        inv = inv - (off if b == 1 else _mm(_mm(inv, off), inv))
        b = blk
    return inv


def _unit_lower_solve(width):
    @jax.custom_vjp
    def solve(low, rhs):
        return _mm3(_unit_lower_inverse(low, width), rhs)

    def fwd(low, rhs):
        inv = _unit_lower_inverse(low, width)
        sol = _mm3(inv, rhs)
        return sol, (inv, sol)

    def bwd(saved, g):
        inv, sol = saved
        d_rhs = _dot(inv, g, ((0,), (0,)), precision=HIGH)
        return -_dot(d_rhs, sol, ((1,), (1,)), precision=HIGH), d_rhs

    solve.defvjp(fwd, bwd)
    return solve


def _conv_silu(prev, x, w):
    R = x.shape[0]
    ext = jnp.concatenate([prev, x], axis=0)
    y = sum(w[j:j + 1, :] * ext[8 - (GDN_CONV - 1) + j:8 - (GDN_CONV - 1) + j + R] for j in range(GDN_CONV))
    return jax.nn.silu(y)


def _gdn_step(params, state, xs, consts):
    cwq, cwk, cwv, ad, gnw = params
    s_cat, pq, pk, pv = state
    q, k, v, z, gb, gbc = xs
    n = gb.shape[0]
    dk = q.shape[1] // n
    one = lambda a, i, w: a[:, i * w:(i + 1) * w]
    states, outs = [], []
    for i in range(n):
        st, (y,) = _gdn_head_step(
            (one(cwq, i, dk), one(cwk, i, dk), one(cwv, i, 2 * dk), ad[i], gnw),
            (one(s_cat, i, 2 * dk), one(pq, i, dk), one(pk, i, dk), one(pv, i, 2 * dk)),
            (one(q, i, dk), one(k, i, dk), one(v, i, 2 * dk), one(z, i, 2 * dk), gb[i], gbc[i]), consts)
        states.append(st)
        outs.append(y)
    cat = lambda j: jnp.concatenate([st[j] for st in states], axis=1)
    return (cat(0), cat(1), cat(2), cat(3)), (jnp.concatenate(outs, axis=1),)


def _gdn_head_step(params, state, xs, consts):
    cwq, cwk, cwv, ad, gnw = params
    s_cat, pq, pk, pv = state
    q, k, v, z, gb, gbc = xs
    C = GDN_CHUNK
    R, dk = q.shape
    dv = v.shape[1] // 2
    ri, ci = _iota((R, R), 0), _iota((R, R), 1)
    sh = C.bit_length() - 1
    same = (ri >> sh) == (ci >> sh)
    causal, strict = same & (ci <= ri), same & (ci < ri)
    qc, kc, vc = _conv_silu(pq, q, cwq), _conv_silu(pk, k, cwk), _conv_silu(pv, v, cwv)
    qn = qc * lax.rsqrt(jnp.sum(qc * qc, axis=1, keepdims=True) + EPS) * (dk ** -0.5)
    kn = kc * lax.rsqrt(jnp.sum(kc * kc, axis=1, keepdims=True) + EPS)
    kk, qk = _mm_nt(kn, kn), _mm_nt(qn, kn)
    g_rows = -jnp.exp(ad[0:2, 0:1]) * jax.nn.softplus(gb[2:4, :] + ad[2:4, 0:1])
    gc_rows = _mm3(g_rows, (same & (ri <= ci)).astype(F32))
    g_cols = jnp.concatenate([-jnp.exp(ad[e:e + 1, 0:1]) * jax.nn.softplus(gbc[:, 2 + e:3 + e] + ad[2 + e:3 + e, 0:1])
                              for e in range(2)], axis=1)
    gc_cols = _mm3(causal.astype(F32), g_cols)
    beta_cols = jax.nn.sigmoid(gbc[:, 0:2])
    lows, rhss, attns, qgs = [], [], [], []
    for e in range(2):
        gcc, bc = gc_cols[:, e:e + 1], beta_cols[:, e:e + 1]
        decay = jnp.exp(jnp.where(causal, gcc - gc_rows[e:e + 1, :], -jnp.inf))
        lows.append(jnp.where(strict, kk * bc * decay, 0.0))
        attns.append(qk * decay)
        eg = jnp.exp(gcc)
        rhss.append(jnp.concatenate([vc[:, e * dv:(e + 1) * dv] * bc, kn * (bc * eg)], axis=1))
        qgs.append(qn * eg)
    zero = jnp.zeros((R, R), F32)
    big = jnp.concatenate([jnp.concatenate([lows[0], zero], axis=1), jnp.concatenate([zero, lows[1]], axis=1)], axis=0)
    sol = _unit_lower_solve(C)(big, jnp.concatenate(rhss, axis=0))
    outs = [[], []]
    n_chunks = R // C
    for j in range(n_chunks):
        sl = slice(j * C, (j + 1) * C)
        lhs = jnp.concatenate([sol[sl, dv:], sol[R + j * C:R + (j + 1) * C, dv:], qgs[0][sl], qgs[1][sl]], axis=0)
        pr = _mm(lhs, s_cat)
        v_new = [sol[e * R + j * C:e * R + (j + 1) * C, :dv] - pr[e * C:(e + 1) * C, e * dv:(e + 1) * dv] for e in range(2)]
        v_lanes = jnp.concatenate(v_new, axis=1)
        pad = jnp.zeros((C, 2 * dv), F32)
        v_rows = jnp.concatenate([v_lanes if i == j else pad for i in range(n_chunks)], axis=0)
        av = _mm(jnp.concatenate([attns[0][sl], attns[1][sl]], axis=0), v_rows)
        g_last = [gc_rows[e:e + 1, (j + 1) * C - 1:(j + 1) * C] for e in range(2)]
        kg = jnp.concatenate([kn[sl] * jnp.exp(g_last[e] - gc_cols[sl, e:e + 1]) for e in range(2)], axis=1)
        kv = _mm_tn(kg, v_lanes)
        s_cat = jnp.concatenate([jnp.exp(g_last[e]) * s_cat[:, e * dv:(e + 1) * dv]
                                 + kv[e * dk:(e + 1) * dk, e * dv:(e + 1) * dv] for e in range(2)], axis=1)
        for e in range(2):
            o = pr[(2 + e) * C:(3 + e) * C, e * dv:(e + 1) * dv] + av[e * C:(e + 1) * C, e * dv:(e + 1) * dv]
            on = o * lax.rsqrt(jnp.mean(o * o, axis=1, keepdims=True) + EPS) * gnw
            outs[e].append(on * jax.nn.silu(z[sl, e * dv:(e + 1) * dv]))
    out = jnp.concatenate([jnp.concatenate(outs[0], axis=0), jnp.concatenate(outs[1], axis=0)], axis=1)
    return (s_cat, q[R - 8:], k[R - 8:], v[R - 8:]), (out,)


def _gdn_ops(p, gb, gbc, conv_w, ad, gnw):
    TP = p.shape[0]
    dk = GDN_DK
    nqk = gb.shape[0]
    hp = GDN_HEADS_PER_STEP
    no = nqk // hp
    R = ROW_TILE
    cw = lambda w, off: (conv_w, (GDN_CONV, hp * w), (lambda o: (0, off + o)), (GDN_CONV, nqk * w), (lambda o: (0, o)))
    params = [cw(dk, 0), cw(dk, no), cw(2 * dk, no), (ad, (hp, 4, LANES), lambda o: (o, 0, 0)), _whole(gnw)]
    col = lambda w, off: (p, (R, hp * w), (lambda o, s: (s, off + o)), (TP, nqk * w), (lambda o, s: (s, o)))
    xs = [col(dk, 0), col(dk, no), col(2 * dk, no), col(2 * dk, 2 * no), (gb, (hp, 4, R), lambda o, s: (o, 0, s)),
          (gbc, (hp, R, 4), lambda o, s: (o, s, 0))]
    states = [((dk, hp * 2 * dk), F32), ((8, hp * dk), F32), ((8, hp * dk), F32), ((8, hp * 2 * dk), F32)]
    ys = [((TP, 2 * nqk * dk), F32, (R, hp * 2 * dk), lambda o, s: (s, o))]
    return no, TP // R, params, xs, states, ys


def gdn_fwd(h, nw1, w_in_t, w_out, conv_w, a_log, dt_bias, gnw):
    D = h.shape[1]
    nqk = D // GDN_DK
    u = rms_fwd("gdn_norm", h, nw1, BF16)
    p = mm_plain("gdn_in", 'nt', u, w_in_t)
    gates = p[:, 6 * D:6 * D + 4 * nqk]
    gb = gates.T.reshape(2, nqk, 2, -1).transpose(1, 0, 2, 3).reshape(nqk, 4, -1)
    gbc = gates.reshape(-1, 2, nqk, 2).transpose(2, 0, 1, 3).reshape(nqk, -1, 4)
    ad = jnp.concatenate([a_log.reshape(nqk, 2), dt_bias.reshape(nqk, 2)], axis=1)
    ad = jnp.broadcast_to(ad[:, :, None], (nqk, 4, LANES))
    no, n, params, xs, states, ys = _gdn_ops(p, gb, gbc, conv_w, ad, gnw)
    (act,), saved = scan_fwd("gdn_core", _gdn_step, no, n, params, [], xs, states, ys)
    h2 = mm_call("gdn_out", 'nn', [(act, w_out, 0)], 1, lambda accs, ex: [ex[0] + accs[0]], [F32], extras=[(h, 'mn')])[0]
    return h2, (h, u, p, gb, gbc, ad, saved, act)


def gdn_bwd(dh2, res, nw1, w_in_t, w_out, conv_w, gnw, dep):
    h, u, p, gb, gbc, ad, saved, act = res
    TP, D = h.shape
    nqk = D // GDN_DK
    dh2, dhb = dh2
    dact = mm_plain("gdn_dact", 'nt', dhb, w_out, dep=dep)
    dw_out = mm_plain("gdn_dwout", 'tn', act, dhb, BF16)
    no, n, params, xs, states, ys = _gdn_ops(p, gb, gbc, conv_w, ad, gnw)
    (dq, dk, dv, dz, dgb, dgbc), (dcq, dck, dcv, dad, dgnw) = scan_bwd(
        "gdn_core_bwd", _gdn_step, no, n, params, [], xs, states, saved, [(dact, ys[0][2], ys[0][3])],
        [False, False, False, False, True])
    dgates = (dgb.reshape(nqk, 2, 2, TP).transpose(1, 0, 2, 3).reshape(4 * nqk, TP).T
              + dgbc.reshape(nqk, TP, 2, 2).transpose(1, 2, 0, 3).reshape(TP, 4 * nqk))
    pad = p.shape[1] - (6 * D + 4 * nqk)
    dp = jnp.concatenate([dq, dk, dv, dz, dgates, jnp.zeros((TP, pad), F32)], axis=1)
    du = mm_plain("gdn_du", 'nn', dp, w_in_t)
    dw_in_t = mm_plain("gdn_dwin", 'tn', dp, u, BF16)
    dh, dnw1 = rms_bwd("gdn_dnorm", h, nw1, du, dh2)
    dad = dad[:, :, 0]
    return dh, dict(nw=dnw1, w_in_t=dw_in_t, w_out=dw_out, conv_w=jnp.concatenate([dcq, dck, dcv], axis=1),
                    a_log=dad[:, :2].reshape(1, 2 * nqk), dt_bias=dad[:, 2:].reshape(1, 2 * nqk), norm_w=dgnw)


def _rope(x, cos, sin):
    W = x.shape[1]
    first_half = (_iota(x.shape, 1) & (SWA_DH - 1)) < SWA_DH // 2
    rot = jnp.where(first_half, -_lane_roll(W - SWA_DH // 2)(x), _lane_roll(SWA_DH // 2)(x))
    return x * jnp.tile(cos, (1, W // LANES)) + rot * jnp.tile(sin, (1, W // LANES))


def _swa_step(params, state, xs, consts):
    (sinks,) = params
    kprev, vprev = state
    q, k, v = xs
    cos, sin, pos = consts
    R = q.shape[0]
    hkv = k.shape[1] // SWA_DH
    G = SWA_GROUP
    qr, kr = _rope(q, cos, sin), _rope(k, cos, sin)
    k2, v2 = jnp.concatenate([kprev, kr], axis=0), jnp.concatenate([vprev, v], axis=0)
    lane = _iota((R, LANES), 1)
    qpos = jnp.concatenate([pos] * G, axis=0)
    kpos = pos[0:1, 0:1] - R + _iota((1, 2 * R), 1)
    mask = (kpos <= qpos) & (qpos - kpos < SWA_WINDOW) & (kpos >= 0)
    sel_r, sel_c = _iota((hkv * SWA_DH, LANES), 0), _iota((hkv * SWA_DH, LANES), 1)
    out = []
    for hh in range(hkv):
        sel = _bf((sel_r == hh * SWA_DH + (sel_c & (SWA_DH - 1))).astype(F32))
        kd, vd = _mm(k2, sel), _mm(v2, sel)
        q8 = []
        for i in range(G // 2):
            qb = qr[:, (hh * G // 2 + i) * LANES:(hh * G // 2 + i + 1) * LANES]
            q8 += [jnp.where(lane < SWA_DH, qb, 0.0), jnp.where(lane >= SWA_DH, qb, 0.0)]
        s = _mm_nt(jnp.concatenate(q8, axis=0), kd) * (SWA_DH ** -0.5)
        s = jnp.where(mask, s, -jnp.inf)
        sink = jnp.concatenate([jnp.broadcast_to(sinks[0:1, hh * G + g:hh * G + g + 1], (R, 1)) for g in range(G)], axis=0)
        m = lax.stop_gradient(jnp.maximum(jnp.max(s, axis=1, keepdims=True), sink))
        e = jnp.exp(s - m)
        prob = e / (jnp.sum(e, axis=1, keepdims=True) + jnp.exp(sink - m))
        o8 = _mm(prob, vd)
        for i in range(G // 2):
            out.append(jnp.where(lane < SWA_DH, o8[2 * i * R:(2 * i + 1) * R], o8[(2 * i + 1) * R:(2 * i + 2) * R]))
    return (kr, v), (jnp.concatenate(out, axis=1),)


def _swa_ops(p, sinks, cos, sin, pos):
    TP = p.shape[0]
    R = ROW_TILE
    hq = sinks.shape[1]
    wq, wkv = hq * SWA_DH, hq // SWA_GROUP * SWA_DH
    nb = wq // wkv
    xs = [(p, (R, wq), (lambda o, s: (s, 0)), (TP, wq), (lambda o, s: (s, 0))),
          (p, (R, wkv), (lambda o, s: (s, nb)), (TP, wkv), (lambda o, s: (s, 0))),
          (p, (R, wkv), (lambda o, s: (s, nb + 1)), (TP, wkv), (lambda o, s: (s, 0)))]
    consts = [_rows(cos, R), _rows(sin, R), _rows(pos, R)]
    states = [((R, wkv), F32), ((R, wkv), F32)]
    ys = [((TP, wq), F32, (R, wq), lambda o, s: (s, 0))]
    return TP // R, [_whole(sinks)], consts, xs, states, ys


def swa_fwd(h, nw1, w_qkv_t, b_qkv, w_out, b_out, sinks, cos, sin, pos):
    u = rms_fwd("swa_norm", h, nw1, BF16)
    p = mm_call("swa_in", 'nt', [(u, w_qkv_t, 0)], 1, lambda accs, ex: [accs[0] + ex[0]], [F32], extras=[(b_qkv, 'n')])[0]
    n, params, consts, xs, states, ys = _swa_ops(p, sinks, cos, sin, pos)
    (act,), saved = scan_fwd("swa_core", _swa_step, 1, n, params, consts, xs, states, ys)
    h2 = mm_call("swa_out", 'nn', [(act, w_out, 0)], 1, lambda accs, ex: [ex[0] + accs[0] + ex[1]], [F32],
                 extras=[(h, 'mn'), (b_out, 'n')])[0]
    return h2, (h, u, p, saved, act)


def swa_bwd(dh2, res, nw1, w_qkv_t, w_out, sinks, cos, sin, pos, dep):
    h, u, p, saved, act = res
    dh2, dhb = dh2
    dact = mm_plain("swa_dact", 'nt', dhb, w_out, dep=dep)
    dw_out = mm_plain("swa_dwout", 'tn', act, dhb, BF16)
    db_out = colsum("swa_dbout", dh2)
    n, params, consts, xs, states, ys = _swa_ops(p, sinks, cos, sin, pos)
    (dq, dk, dv), (dsinks,) = scan_bwd("swa_core_bwd", _swa_step, 1, n, params, consts, xs, states, saved,
                                       [(dact, ys[0][2], ys[0][3])], [True])
    dp = jnp.concatenate([dq, dk, dv], axis=1)
    db_qkv = colsum("swa_dbqkv", dp)
    du = mm_plain("swa_du", 'nn', dp, w_qkv_t)
    dw_qkv_t = mm_plain("swa_dwqkv", 'tn', dp, u, BF16)
    dh, dnw1 = rms_bwd("swa_dnorm", h, nw1, du, dh2)
    return dh, dict(nw=dnw1, w_qkv_t=dw_qkv_t, w_out=dw_out, b_qkv=db_qkv, b_out=db_out, sinks=dsinks)


def _dev_index(dev):
    return 4 * dev[0] + 2 * dev[1] + dev[2]


def all_gather(name, shards):
    n = len(shards)

    def body(*refs):
        x_refs, out_refs = refs[:n], refs[n:2 * n]
        send_sems, recv_sems, local_sem = refs[2 * n:]
        x, y, c = lax.axis_index("x"), lax.axis_index("y"), lax.axis_index("c")
        me, sibling = (x, y, c), (x, y, 1 - c)
        chips = [(1 - x, y), (x, 1 - y), (1 - x, 1 - y)]

        def copy(a, k, block, to, src=None):
            dst = out_refs[a].at[_dev_index(block)]
            return pltpu.make_async_remote_copy(src_ref=dst if src is None else src, dst_ref=dst,
                                                send_sem=send_sems.at[a, k], recv_sem=recv_sems.at[a, k],
                                                device_id=to, device_id_type=MESH_ID)

        mine = [pltpu.make_async_copy(x_refs[a], out_refs[a].at[_dev_index(me)], local_sem.at[a]) for a in range(n)]
        first, passed = [], []
        for a in range(n):
            mine[a].start()
            first += [copy(a, 0, me, sibling, src=x_refs[a])]
            first += [copy(a, 1 + j, me, (*chip, c), src=x_refs[a]) for j, chip in enumerate(chips)]
        for cp in first:
            cp.start()
        for a in range(n):
            for j, chip in enumerate(chips):
                copy(a, 1 + j, (*chip, c), me).wait_recv()
                fwd = copy(a, 4 + j, (*chip, c), sibling)
                fwd.start()
                passed.append(fwd)
        for a in range(n):
            copy(a, 0, sibling, me).wait_recv()
            for j, chip in enumerate(chips):
                copy(a, 4 + j, (*chip, 1 - c), me).wait_recv()
        for cp in first + passed:
            cp.wait_send()
        for cp in mine:
            cp.wait()

    any_spec = pl.BlockSpec(memory_space=pl.ANY)
    return pl.pallas_call(
        body, in_specs=[any_spec] * n, out_specs=[any_spec] * n,
        out_shape=[jax.ShapeDtypeStruct((N_DEV,) + s.shape, s.dtype) for s in shards],
        scratch_shapes=[pltpu.SemaphoreType.DMA((n, 7)), pltpu.SemaphoreType.DMA((n, 7)), pltpu.SemaphoreType.DMA((n,))],
        name=name)(*shards)


def exchange_blocks(name, fulls):
    n = len(fulls)

    def body(*refs):
        g_refs, out_refs = refs[:n], refs[n:2 * n]
        send_sems, recv_sems, local_sem = refs[2 * n:]
        x, y, c = lax.axis_index("x"), lax.axis_index("y"), lax.axis_index("c")
        me = (x, y, c)
        peers = [(1 - x if r & 4 else x, 1 - y if r & 2 else y, 1 - c if r & 1 else c) for r in range(1, N_DEV)]

        def copy(a, k, peer):
            return pltpu.make_async_remote_copy(src_ref=g_refs[a].at[_dev_index(peer)], dst_ref=out_refs[a].at[_dev_index(me)],
                                                send_sem=send_sems.at[a, k], recv_sem=recv_sems.at[a, k],
                                                device_id=peer, device_id_type=MESH_ID)

        mine = [pltpu.make_async_copy(g_refs[a].at[_dev_index(me)], out_refs[a].at[_dev_index(me)], local_sem.at[a])
                for a in range(n)]
        sends = [copy(a, k, peer) for a in range(n) for k, peer in enumerate(peers)]
        for cp in mine + sends:
            cp.start()
        for a in range(n):
            for k, peer in enumerate(peers):
                pltpu.make_async_remote_copy(src_ref=g_refs[a].at[_dev_index(peer)], dst_ref=out_refs[a].at[_dev_index(peer)],
                                             send_sem=send_sems.at[a, k], recv_sem=recv_sems.at[a, k],
                                             device_id=peer, device_id_type=MESH_ID).wait_recv()
        for cp in sends:
            cp.wait_send()
        for cp in mine:
            cp.wait()

    any_spec = pl.BlockSpec(memory_space=pl.ANY)
    return pl.pallas_call(
        body, in_specs=[any_spec] * n, out_specs=[any_spec] * n,
        out_shape=[jax.ShapeDtypeStruct(g.shape, g.dtype) for g in fulls],
        scratch_shapes=[pltpu.SemaphoreType.DMA((n, 7)), pltpu.SemaphoreType.DMA((n, 7)), pltpu.SemaphoreType.DMA((n,))],
        name=name)(*fulls)


def _peers_of(x, y, c):
    return [(1 - x if r & 4 else x, 1 - y if r & 2 else y, 1 - c if r & 1 else c) for r in range(1, N_DEV)]


def push_start(name, srcs, lands, after, gather):
    n = len(srcs)

    def body(*refs):
        src_refs, land_refs = refs[:n], refs[n:2 * n]
        send_sems, recv_sems = refs[2 * n + 1], refs[2 * n + 2]
        token = refs[-1]
        x, y, c = lax.axis_index("x"), lax.axis_index("y"), lax.axis_index("c")
        me = (x, y, c)
        for a in range(n):
            for k, peer in enumerate(_peers_of(x, y, c)):
                pltpu.make_async_remote_copy(
                    src_ref=src_refs[a] if gather else src_refs[a].at[_dev_index(peer)],
                    dst_ref=land_refs[a].at[_dev_index(me)], send_sem=send_sems.at[a * (N_DEV - 1) + k],
                    recv_sem=recv_sems.at[a * (N_DEV - 1) + k],
                    device_id=peer, device_id_type=MESH_ID).start()
        token[...] = jnp.zeros_like(token)

    hbm = pl.BlockSpec(memory_space=pltpu.HBM)
    sem = pl.BlockSpec(memory_space=pltpu.SEMAPHORE)
    outs = pl.pallas_call(
        body, name=name,
        out_shape=(pltpu.SemaphoreType.DMA((n * (N_DEV - 1),)), pltpu.SemaphoreType.DMA((n * (N_DEV - 1),)),
                   *[pltpu.HBM(s.shape, s.dtype) for s in srcs], *[pltpu.HBM(l.shape, l.dtype) for l in lands],
                   jax.ShapeDtypeStruct((8, LANES), F32)),
        in_specs=[hbm] * (2 * n) + [pl.BlockSpec(memory_space=pl.ANY)],
        out_specs=(sem, sem, *[hbm] * (2 * n), pl.BlockSpec(memory_space=pltpu.VMEM)),
        input_output_aliases={i: 2 + i for i in range(2 * n)},
        compiler_params=pltpu.CompilerParams(has_side_effects=pltpu.SideEffectType.DATAFLOW_SIDE_EFFECTING),
    )(*[pltpu.with_memory_space_constraint(s, pltpu.HBM) for s in srcs],
      *[pltpu.with_memory_space_constraint(l, pltpu.HBM) for l in lands], after)
    return (outs[0], outs[1], outs[2:2 + n], outs[2 + n:2 + 2 * n], gather), outs[-1]


def push_wait(name, handle, after):
    send_sems, recv_sems, srcs, lands, gather = handle
    n = len(srcs)

    def body(*refs):
        src_refs, land_refs = refs[:n], refs[n:2 * n]
        send_sem_ref, recv_sem_ref = refs[2 * n], refs[2 * n + 1]
        x, y, c = lax.axis_index("x"), lax.axis_index("y"), lax.axis_index("c")
        for a in range(n):
            for k, peer in enumerate(_peers_of(x, y, c)):
                cp = pltpu.make_async_remote_copy(
                    src_ref=src_refs[a] if gather else src_refs[a].at[_dev_index(peer)],
                    dst_ref=land_refs[a].at[_dev_index(peer)], send_sem=send_sem_ref.at[a * (N_DEV - 1) + k],
                    recv_sem=recv_sem_ref.at[a * (N_DEV - 1) + k],
                    device_id=peer, device_id_type=MESH_ID)
                cp.wait_send()
                cp.wait_recv()

    hbm = pl.BlockSpec(memory_space=pltpu.HBM)
    sem = pl.BlockSpec(memory_space=pltpu.SEMAPHORE)
    outs = pl.pallas_call(
        body, name=name,
        out_shape=(*[pltpu.HBM(s.shape, s.dtype) for s in srcs], *[pltpu.HBM(l.shape, l.dtype) for l in lands]),
        in_specs=[hbm] * (2 * n) + [sem, sem, pl.BlockSpec(memory_space=pl.ANY)],
        out_specs=tuple([hbm] * (2 * n)),
        input_output_aliases={i: i for i in range(2 * n)},
        compiler_params=pltpu.CompilerParams(has_side_effects=pltpu.SideEffectType.DATAFLOW_SIDE_EFFECTING),
    )(*srcs, *lands, send_sems, recv_sems, after)
    return outs[n:]


def _split_call(name, body, ins, sem_ins, after, n_sem_out, with_token):
    hbm = pl.BlockSpec(memory_space=pltpu.HBM)
    sem = pl.BlockSpec(memory_space=pltpu.SEMAPHORE)
    n = len(ins)
    out_shape = ([pltpu.SemaphoreType.DMA((s,)) for s in n_sem_out] + [pltpu.HBM(a.shape, a.dtype) for a in ins]
                 + ([jax.ShapeDtypeStruct((8, LANES), F32)] if with_token else []))
    out_specs = [sem] * len(n_sem_out) + [hbm] * n + ([pl.BlockSpec(memory_space=pltpu.VMEM)] if with_token else [])
    outs = pl.pallas_call(
        body, name=name, out_shape=tuple(out_shape), out_specs=tuple(out_specs),
        in_specs=[hbm] * n + [sem] * len(sem_ins) + [pl.BlockSpec(memory_space=pl.ANY)],
        input_output_aliases={i: len(n_sem_out) + i for i in range(n)},
        compiler_params=pltpu.CompilerParams(has_side_effects=pltpu.SideEffectType.DATAFLOW_SIDE_EFFECTING),
    )(*ins, *sem_ins, after)
    k = len(n_sem_out)
    return outs[:k], outs[k:k + n], (outs[-1] if with_token else None)


def _chips_of(x, y):
    return [(1 - x, y), (x, 1 - y), (1 - x, 1 - y)]


def gather_start(name, shards, lands, after):
    n = len(shards)

    def body(*refs):
        src, land = refs[:n], refs[n:2 * n]
        send, recv = refs[2 * n + 1], refs[2 * n + 2]
        token = refs[-1]
        x, y, c = lax.axis_index("x"), lax.axis_index("y"), lax.axis_index("c")
        to = [(x, y, 1 - c)] + [(*chip, c) for chip in _chips_of(x, y)]
        for a in range(n):
            for k, dev in enumerate(to):
                pltpu.make_async_remote_copy(src_ref=src[a], dst_ref=land[a].at[_dev_index((x, y, c))],
                                             send_sem=send.at[4 * a + k], recv_sem=recv.at[4 * a + k],
                                             device_id=dev, device_id_type=MESH_ID).start()
        token[...] = jnp.zeros_like(token)

    ins = [pltpu.with_memory_space_constraint(a, pltpu.HBM) for a in list(shards) + list(lands)]
    sems, thru, token = _split_call(name, body, ins, [], after, [4 * n, 4 * n], True)
    return dict(n=n, send=sems[0], recv=sems[1], shards=thru[:n], lands=thru[n:]), token


def gather_pass(name, h, after):
    n = h['n']

    def body(*refs):
        land, recv = refs[:n], refs[n]
        send2, recv2 = refs[n + 2], refs[n + 3]
        token = refs[-1]
        x, y, c = lax.axis_index("x"), lax.axis_index("y"), lax.axis_index("c")
        for a in range(n):
            for j, chip in enumerate(_chips_of(x, y)):
                blk = land[a].at[_dev_index((*chip, c))]
                pltpu.make_async_remote_copy(src_ref=blk, dst_ref=blk, send_sem=send2.at[3 * a + j], recv_sem=recv.at[4 * a + 1 + j],
                                             device_id=(*chip, c), device_id_type=MESH_ID).wait_recv()
                pltpu.make_async_remote_copy(src_ref=blk, dst_ref=blk, send_sem=send2.at[3 * a + j], recv_sem=recv2.at[3 * a + j],
                                             device_id=(x, y, 1 - c), device_id_type=MESH_ID).start()
        token[...] = jnp.zeros_like(token)

    sems, lands, token = _split_call(name, body, list(h['lands']), [h['recv']], after, [3 * n, 3 * n], True)
    return dict(h, lands=lands, send2=sems[0], recv2=sems[1]), token


def gather_wait(name, h, after):
    n = h['n']

    def body(*refs):
        src, land = refs[:n], refs[n:2 * n]
        send, recv, send2, recv2 = refs[2 * n:2 * n + 4]
        x, y, c = lax.axis_index("x"), lax.axis_index("y"), lax.axis_index("c")
        sib = (x, y, 1 - c)
        for a in range(n):
            mine = land[a].at[_dev_index((x, y, c))]
            for k in range(4):
                pltpu.make_async_remote_copy(src_ref=src[a], dst_ref=mine, send_sem=send.at[4 * a + k], recv_sem=recv.at[4 * a + k],
                                             device_id=sib, device_id_type=MESH_ID).wait_send()
            theirs = land[a].at[_dev_index(sib)]
            pltpu.make_async_remote_copy(src_ref=src[a], dst_ref=theirs, send_sem=send.at[4 * a], recv_sem=recv.at[4 * a],
                                         device_id=sib, device_id_type=MESH_ID).wait_recv()
            for j, chip in enumerate(_chips_of(x, y)):
                sent, got = land[a].at[_dev_index((*chip, c))], land[a].at[_dev_index((*chip, 1 - c))]
                cp = pltpu.make_async_remote_copy(src_ref=sent, dst_ref=got, send_sem=send2.at[3 * a + j], recv_sem=recv2.at[3 * a + j],
                                                  device_id=sib, device_id_type=MESH_ID)
                cp.wait_send()
                cp.wait_recv()

    _, thru, _ = _split_call(name, body, list(h['shards']) + list(h['lands']), [h['send'], h['recv'], h['send2'], h['recv2']],
                             after, [], False)
    return thru[n:]


def _own_block(block, me):
    land = lax.empty((N_DEV,) + block.shape, block.dtype)
    return lax.dynamic_update_slice(land, block[None], (me,) + (0,) * block.ndim)


def sum_blocks(name, parts):
    _, r, c = parts.shape
    tr = _pick(r, ROW_TILES)

    def body(p_ref, o_ref):
        acc = p_ref[0].astype(F32)
        for b in range(1, N_DEV):
            acc = acc + p_ref[b].astype(F32)
        o_ref[...] = acc

    return pl.pallas_call(
        body, grid=(r // tr,), in_specs=[pl.BlockSpec((N_DEV, tr, c), lambda i: (0, i, 0))],
        out_specs=pl.BlockSpec((tr, c), lambda i: (i, 0)), out_shape=jax.ShapeDtypeStruct((r, c), F32),
        compiler_params=pltpu.CompilerParams(dimension_semantics=("arbitrary",), vmem_limit_bytes=VMEM_LIMIT),
        name=name)(parts)


def adamw(name, w, g, m, v):
    r, c = w.shape
    tr = _pick(r, [t for t in ROW_TILES if t * c * 4 * 7 * 2 <= VMEM_LIMIT // 2])

    def body(w_ref, g_ref, m_ref, v_ref, d_ref, nm_ref, nv_ref):
        gv = g_ref[...]
        nm = ADAM_B1 * m_ref[...] + (1.0 - ADAM_B1) * gv
        nv = ADAM_B2 * v_ref[...] + (1.0 - ADAM_B2) * jnp.square(gv)
        m_hat = nm / (1.0 - ADAM_B1 ** ADAM_STEP)
        v_hat = nv / (1.0 - ADAM_B2 ** ADAM_STEP)
        d_ref[...] = -ADAM_LR * (m_hat / (jnp.sqrt(v_hat) + ADAM_EPS) + ADAM_WD * w_ref[...])
        nm_ref[...] = nm
        nv_ref[...] = nv

    spec = pl.BlockSpec((tr, c), lambda i: (i, 0))
    return pl.pallas_call(
        body, grid=(r // tr,), in_specs=[spec] * 4, out_specs=[spec] * 3,
        out_shape=[jax.ShapeDtypeStruct((r, c), F32)] * 3,
        compiler_params=pltpu.CompilerParams(dimension_semantics=("arbitrary",), vmem_limit_bytes=VMEM_LIMIT),
        name=name)(w, g, m, v)


def _pack(arrays):
    flat = jnp.concatenate([a.reshape(-1) for a in arrays])
    n = _round_up(flat.shape[0], 8 * LANES)
    return jnp.pad(flat, (0, n - flat.shape[0])).reshape(-1, LANES)


def _unpack(packed, shapes):
    flat = packed.reshape(-1)
    out, o = [], 0
    for sh in shapes:
        sz = 1
        for d in sh:
            sz *= d
        out.append(flat[o:o + sz].reshape(sh))
        o += sz
    return out


def _gather_axis(g, ax):
    g = jnp.moveaxis(g, 0, ax)
    return g.reshape(g.shape[:ax] + (g.shape[ax] * g.shape[ax + 1],) + g.shape[ax + 2:])


def _comm_rows(a):
    r = a.shape[0]
    rp = r if r % 16 == 0 else _round_up(r, ROW_PAD)
    return jnp.pad(_bf(a), ((0, rp - r), (0, 0)))


def _natural(g, r, pad_to=None):
    full = g[:, :r].reshape(N_DEV * r, g.shape[2])
    if pad_to is not None and full.shape[0] % pad_to:
        full = jnp.pad(full, ((0, _round_up(full.shape[0], pad_to) - full.shape[0]), (0, 0)))
    return full


def _blocked(full, r):
    blocks = full[:N_DEV * r].reshape(N_DEV, r, full.shape[1])
    rp = r if r % 16 == 0 else _round_up(r, ROW_PAD)
    return jnp.pad(_bf(blocks), ((0, 0), (0, rp - r), (0, 0)))


def kernel(x, *rest):
    nw_ = len(WEIGHTS)
    W = dict(zip(WEIGHTS, rest[:nw_]))
    loss_target = rest[nw_]
    M = dict(zip(WEIGHTS, rest[nw_ + 1:2 * nw_ + 1]))
    V = dict(zip(WEIGHTS, rest[2 * nw_ + 1:3 * nw_ + 1]))

    T0, D = x.shape[1], x.shape[2]
    T = T0 + N_META
    TP = _round_up(T, ROW_TILE)
    me = 4 * lax.axis_index("x") + 2 * lax.axis_index("y") + lax.axis_index("c")

    small_sharded = [k for k in WEIGHTS if k in SMALL and SMALL[k] is not None]
    (sg,) = all_gather("ag_small", [_pack([W[k] for k in small_sharded])])
    per_dev = [_unpack(sg[b], [W[k].shape for k in small_sharded]) for b in range(N_DEV)]
    full = {k: _gather_axis(jnp.stack([per_dev[b][i] for b in range(N_DEV)]), SMALL[k]) for i, k in enumerate(small_sharded)}
    for k in SMALL:
        if SMALL[k] is None:
            full[k] = W[k]

    r_ff = W['ffn_w_gate'].shape[3]
    r_mi, r_mo = W['mlstm_w_in'].shape[2], W['mlstm_w_out'].shape[1]
    r_gi, r_go = W['gdn_w_in'].shape[2], W['gdn_w_out'].shape[1]
    r_si, r_so = W['swa_w_qkv'].shape[2], W['swa_w_out'].shape[1]
    pw = W['pool_w'][0]
    stages = [(i, j) for i in range(DEPTH) for j in range(3)]

    def stage_shards(i, j):
        if j != 1:
            s = j // 2
            return [_comm_rows(W['ffn_w_gate'][i, s].T), _comm_rows(W['ffn_w_up'][i, s].T), _comm_rows(W['ffn_w_down'][i, s])]
        if i % 4 == 0:
            return [_comm_rows(W['mlstm_w_in'][0].T), _comm_rows(W['mlstm_w_out'][0])]
        if i % 4 == 1:
            return [_bf(pw.reshape(-1, pw.shape[2]))]
        if i % 4 == 2:
            return [_comm_rows(W['gdn_w_in'][0].T), _comm_rows(W['gdn_w_out'][0])]
        return [_comm_rows(W['swa_w_qkv'][0].T), _comm_rows(W['swa_w_out'][0])]

    def stage_weights(i, j, g):
        if j != 1:
            return tuple(_natural(a, r_ff) for a in g)
        if i % 4 == 0:
            return (_natural(g[0], r_mi, IN_PAD), _natural(g[1], r_mo))
        if i % 4 == 1:
            return (g[0].reshape(N_DEV, N_POOL, pw.shape[1], pw.shape[2]).transpose(1, 0, 2, 3)
                    .reshape(N_POOL, pw.shape[2], pw.shape[2]).astype(F32),)
        if i % 4 == 2:
            return (_natural(g[0], r_gi, IN_PAD), _natural(g[1], r_go))
        return (_natural(g[0], r_si), _natural(g[1], r_so))

    def start_gather(k, after):
        sh = stage_shards(*stages[k])
        return gather_start(f"ag_start_{k}", sh, [_own_block(s, me) for s in sh], after)

    first = all_gather("ag_stage_0", stage_shards(*stages[0]))
    wts = {0: stage_weights(*stages[0], first)}
    pending, passed, zero = {}, set(), jnp.zeros((), F32)
    ahead = 3
    tok, after = zero, first[0]
    for k in range(1, ahead + 1):
        pending[k], after = start_gather(k, after)
        tok = tok + after[0, 0]

    pos = jnp.arange(TP, dtype=jnp.int32)[:, None]
    inv = ROPE_THETA ** (-jnp.arange(0, SWA_DH, 2, dtype=F32) / SWA_DH)
    ang = jnp.arange(TP, dtype=F32)[:, None] * inv[None, :]
    ang = jnp.concatenate([ang, ang, ang, ang], axis=1)
    cos, sin = jnp.cos(ang), jnp.sin(ang)
    row_mask = ((pos >= N_META) & (pos < T)).astype(F32)
    tgt = jnp.pad(loss_target[0], ((N_META, TP - T), (0, 0)))
    nrm = lambda i, j: full['norm_w'][i, j][None, :]

    h = jnp.concatenate([full['meta_tokens'], x[0], jnp.zeros((TP - T, D), F32)], axis=0)
    res = {}
    for k, (i, j) in enumerate(stages):
        if k >= 1:
            tok = zero
            if k not in passed:
                pending[k], _ = gather_pass(f"ag_pass_{k}", pending[k], h)
            wts[k] = stage_weights(i, j, gather_wait(f"ag_wait_{k}", pending.pop(k), h))
            if k + 1 < len(stages):
                pending[k + 1], t = gather_pass(f"ag_pass_{k + 1}", pending[k + 1], h)
                passed.add(k + 1)
                tok = tok + t[0, 0]
            if k + ahead < len(stages):
                pending[k + ahead], t = start_gather(k + ahead, h)
                tok = tok + t[0, 0]
        nw = nrm(i, j) + tok
        w = wts[k]
        if j != 1:
            h, res[k] = ffn_fwd(f"ffn_{i}_{j // 2}", h, nw, *w)
        elif i % 4 == 0:
            h, res[k] = mlstm_fwd(h, nw, w[0], w[1], full['mlstm_b_if'], full['mlstm_norm_w'])
        elif i % 4 == 1:
            h, res[k] = pool_fwd(h, nw, w[0], full['pool_scale'], pos)
        elif i % 4 == 2:
            h, res[k] = gdn_fwd(h, nw, w[0], w[1], full['gdn_conv_w'][0], full['gdn_a_log'], full['gdn_dt_bias'],
                                full['gdn_norm_w'])
        else:
            h, res[k] = swa_fwd(h, nw, w[0], full['swa_b_qkv'], w[1], full['swa_b_out'], full['swa_sinks'], cos, sin, pos)

    loss_local, dh_f32, d_final, dh_bf = loss_call(h, full['final_norm_w'][None, :], tgt, row_mask)
    dh = (dh_f32, dh_bf)
    loss = lax.psum(loss_local[0, 0], ("x", "y", "c"))

    gs = {'final_norm_w': d_final[0]}
    d_norm = [[None] * 3 for _ in range(DEPTH)]
    sent = []
    dep = jnp.zeros((8, LANES), F32)

    def exchange(k, part, blocks, after):
        lands = [_own_block(lax.dynamic_index_in_dim(b, me, 0, keepdims=False), me) for b in blocks]
        handle, token = push_start(f"rs_start_{k}_{part}", blocks, lands, after, False)
        sent.append((k, part, handle))
        return token

    for k in reversed(range(len(stages))):
        i, j = stages[k]
        nw, w = nrm(i, j), wts[k]
        if j != 1:
            send = lambda part, gs_, k=k: exchange(k, part, [_blocked(g, r_ff) for g in gs_], gs_[0])
            dh, d_norm[i][j], dep = ffn_bwd(f"ffn_{i}_{j // 2}", dh, res[k], nw, *w, dep, send)
            continue
        else:
            if i % 4 == 0:
                dh, gm = mlstm_bwd(dh, res[k], nw, w[0], w[1], full['mlstm_norm_w'], dep)
                blocks = [_blocked(gm['w_in_t'], r_mi), _blocked(gm['w_out'], r_mo)]
                gs.update(mlstm_b_if=gm['b_if'], mlstm_norm_w=gm['norm_w'])
            elif i % 4 == 1:
                dh, gm = pool_bwd(dh, res[k], nw + dep[0, 0], w[0], full['pool_scale'], pos)
                dw = gm['w'].reshape(N_POOL, N_DEV, pw.shape[1], pw.shape[2]).transpose(1, 0, 2, 3)
                blocks = [_bf(dw.reshape(N_DEV, N_POOL * pw.shape[1], pw.shape[2]))]
                gs.update(pool_scale=gm['scale'])
            elif i % 4 == 2:
                dh, gm = gdn_bwd(dh, res[k], nw, w[0], w[1], full['gdn_conv_w'][0], full['gdn_norm_w'], dep)
                blocks = [_blocked(gm['w_in_t'], r_gi), _blocked(gm['w_out'], r_go)]
                gs.update(gdn_conv_w=gm['conv_w'][None], gdn_a_log=gm['a_log'], gdn_dt_bias=gm['dt_bias'],
                          gdn_norm_w=gm['norm_w'])
            else:
                dh, gm = swa_bwd(dh, res[k], nw, w[0], w[1], full['swa_sinks'], cos, sin, pos, dep)
                blocks = [_blocked(gm['w_qkv_t'], r_si), _blocked(gm['w_out'], r_so)]
                gs.update(swa_b_qkv=gm['b_qkv'], swa_b_out=gm['b_out'], swa_sinks=gm['sinks'])
            d_norm[i][1] = gm['nw']
        dep = exchange(k, 'm', blocks, dh[0])
    gs['norm_w'] = jnp.stack([jnp.concatenate(d_norm[i], axis=0) for i in range(DEPTH)])
    gs['meta_tokens'] = dh[0][:N_META]
    grad_x = dh[0][N_META:T][None]

    grads = {}
    small_names = [k for k in WEIGHTS if k in SMALL]
    (parts,) = all_gather("ag_small_grads", [_pack([gs[k].reshape(full[k].shape) for k in small_names])])
    tot = _unpack(sum_blocks("sum_small_grads", parts), [full[k].shape for k in small_names])
    for k, g in zip(small_names, tot):
        ax = SMALL[k]
        grads[k] = g if ax is None else lax.dynamic_slice_in_dim(g, me * W[k].shape[ax], W[k].shape[ax], axis=ax)

    summed, after = {}, dh[0]
    for k, part, handle in sent:
        parts = push_wait(f"rs_wait_{k}_{part}", handle, after)
        summed[k, part] = [sum_blocks(f"sum_{k}_{part}_{n}", p) for n, p in enumerate(parts)]
        after = summed[k, part][0]

    gg, gu, gd = [], [], []
    for i in range(DEPTH):
        for j in (0, 2):
            a, b = summed[3 * i + j, 'gu']
            gg.append(a[:r_ff].T)
            gu.append(b[:r_ff].T)
            gd.append(summed[3 * i + j, 'd'][0][:r_ff])
    shape4 = lambda lst, ref: jnp.stack(lst).reshape(ref.shape)
    grads['ffn_w_gate'] = shape4(gg, W['ffn_w_gate'])
    grads['ffn_w_up'] = shape4(gu, W['ffn_w_up'])
    grads['ffn_w_down'] = shape4(gd, W['ffn_w_down'])
    for i in range(DEPTH):
        g = summed[3 * i + 1, 'm']
        if i % 4 == 0:
            grads['mlstm_w_in'], grads['mlstm_w_out'] = g[0][:r_mi].T[None], g[1][:r_mo][None]
        elif i % 4 == 1:
            grads['pool_w'] = g[0].reshape(W['pool_w'].shape)
        elif i % 4 == 2:
            grads['gdn_w_in'], grads['gdn_w_out'] = g[0][:r_gi].T[None], g[1][:r_go][None]
        else:
            grads['swa_w_qkv'], grads['swa_w_out'] = g[0][:r_si].T[None], g[1][:r_so][None]

    delta, new_m, new_v = {}, {}, {}
    shapes = [W[k].shape for k in small_names]
    d, nm, nv = adamw("adamw_small", _pack([W[k] for k in small_names]), _pack([grads[k] for k in small_names]),
                      _pack([M[k] for k in small_names]), _pack([V[k] for k in small_names]))
    for k, a, b, c_ in zip(small_names, _unpack(d, shapes), _unpack(nm, shapes), _unpack(nv, shapes)):
        delta[k], new_m[k], new_v[k] = a, b, c_
    for k in WEIGHTS:
        if k not in SMALL:
            two = lambda a: a.reshape(-1, a.shape[-1])
            d, nm, nv = adamw("adamw_" + k, two(W[k]), two(grads[k]), two(M[k]), two(V[k]))
            delta[k], new_m[k], new_v[k] = d.reshape(W[k].shape), nm.reshape(W[k].shape), nv.reshape(W[k].shape)

    return (loss, grad_x, *[grads[k].reshape(W[k].shape) for k in WEIGHTS], *[delta[k] for k in WEIGHTS],
            *[new_m[k] for k in WEIGHTS], *[new_v[k] for k in WEIGHTS])
```

```python
import functools

import jax
import jax.numpy as jnp
from jax import lax
from jax.experimental import pallas as pl
from jax.experimental.pallas import tpu as pltpu

F32 = jnp.float32
BF16 = jnp.bfloat16

N_DEV = 8
N_META = 16
EPS = 1e-6
DEPTH = 4
MLSTM_HEADS = 8
MLSTM_CHUNK = 64
N_POOL = 4
GDN_DK = 128
GDN_CHUNK = 64
GDN_CONV = 4
GDN_HEADS_PER_STEP = 2
SWA_DH = 64
SWA_GROUP = 8
SWA_WINDOW = 128
ROPE_THETA = 10000.0
ADAM_LR = 0.001
ADAM_B1 = 0.9
ADAM_B2 = 0.999
ADAM_EPS = 1e-08
ADAM_WD = 0.01
ADAM_STEP = 10

LANES = 128
ROW_TILE = 128
ROW_PAD = 112
IN_PAD = 896
VMEM_LIMIT = 56 * 1024 * 1024
TOKEN_TILES = (1056, 768, 512, 384, 256, 128)
FEATURE_TILES = (512, 896, 768, 640, 384, 256, 128)
ROW_TILES = (512, 352, 256, 176, 160, 128, 112, 64, 48, 32, 16, 8)
HIGHEST = lax.Precision.HIGHEST
HIGH = lax.Precision.HIGH
MESH_ID = pl.DeviceIdType.MESH

WEIGHTS = ('meta_tokens', 'norm_w', 'ffn_w_gate', 'ffn_w_up', 'ffn_w_down', 'mlstm_w_in', 'mlstm_b_if',
           'mlstm_norm_w', 'mlstm_w_out', 'pool_w', 'pool_scale', 'gdn_w_in', 'gdn_conv_w', 'gdn_a_log',
           'gdn_dt_bias', 'gdn_norm_w', 'gdn_w_out', 'swa_w_qkv', 'swa_b_qkv', 'swa_sinks', 'swa_w_out',
           'swa_b_out', 'final_norm_w')
SMALL = {'meta_tokens': 1, 'norm_w': 2, 'mlstm_b_if': None, 'mlstm_norm_w': None, 'pool_scale': 1,
         'gdn_conv_w': 2, 'gdn_a_log': None, 'gdn_dt_bias': None, 'gdn_norm_w': None, 'swa_b_qkv': 1,
         'swa_sinks': None, 'swa_b_out': 1, 'final_norm_w': None}


def _pick(n, cands):
    for c in cands:
        if n % c == 0:
            return c
    return n


def _round_up(n, m):
    return -(-n // m) * m


def _bf(x):
    return x.astype(BF16)


def _dot(a, b, dims, precision=None):
    return lax.dot_general(a, b, (dims, ((), ())), preferred_element_type=F32, precision=precision)


def _mm(a, b):
    return _dot(_bf(a), _bf(b), ((1,), (0,)))


def _mm_nt(a, b):
    return _dot(_bf(a), _bf(b), ((1,), (1,)))


def _mm_tn(a, b):
    return _dot(_bf(a), _bf(b), ((0,), (0,)))


def _mm32(a, b):
    return _dot(a, b, ((1,), (0,)), precision=HIGHEST)


def _mm3(a, b):
    return _dot(a, b, ((1,), (0,)), precision=HIGH)


def _iota(shape, axis):
    return lax.broadcasted_iota(jnp.int32, shape, axis)


def _row2col(row, eye):
    return jnp.sum(eye * row, axis=1, keepdims=True)


def _lane_roll(shift):
    @jax.custom_vjp
    def f(x):
        return pltpu.roll(x, shift, 1)

    def fwd(x):
        return f(x), None

    def bwd(_, g):
        return (pltpu.roll(g, g.shape[1] - shift, 1),)

    f.defvjp(fwd, bwd)
    return f


def mm_call(name, mode, pairs, n_acc, epilogue, out_dtypes, extras=(), tm=None, tn=None, tk=None):
    a0, b0 = pairs[0][0], pairs[0][1]
    if mode == 'nn':
        (M, K), N = a0.shape, b0.shape[1]
    elif mode == 'nt':
        (M, K), N = a0.shape, b0.shape[0]
    else:
        (K, M), N = a0.shape, b0.shape[1]
    if mode == 'tn':
        tm = tm or _pick(M, FEATURE_TILES)
        tn = tn or (N if N <= 2048 else _pick(N, FEATURE_TILES))
        tk = tk or _pick(K, [t for t in (2112,) + TOKEN_TILES
                             if 2 * t * (tm * a0.dtype.itemsize + tn * b0.dtype.itemsize) <= VMEM_LIMIT // 2])
        dims = ((0,), (0,))
        a_spec = pl.BlockSpec((tk, tm), lambda i, j, k: (k, i))
        b_spec = pl.BlockSpec((tk, tn), lambda i, j, k: (k, j))
    else:
        tm = tm or _pick(M, TOKEN_TILES)
        tn = tn or _pick(N, FEATURE_TILES)
        fixed = tm * tn * 4 * (n_acc + 2 * len(out_dtypes) + 2 * sum(kind == 'mn' for _, kind in extras))
        fits = lambda t: 2 * len(pairs) * t * (tm * a0.dtype.itemsize + tn * b0.dtype.itemsize) + fixed <= VMEM_LIMIT * 7 // 8
        tk = tk or _pick(K, [t for t in (K, 2816, 2048) + FEATURE_TILES if fits(t)])
        a_spec = pl.BlockSpec((tm, tk), lambda i, j, k: (i, k))
        if mode == 'nn':
            dims = ((1,), (0,))
            b_spec = pl.BlockSpec((tk, tn), lambda i, j, k: (k, j))
        else:
            dims = ((1,), (1,))
            b_spec = pl.BlockSpec((tn, tk), lambda i, j, k: (j, k))
    n_pairs, n_ex, n_out = len(pairs), len(extras), len(out_dtypes)
    nk = K // tk

    def body(*refs):
        ab = refs[:2 * n_pairs]
        ex = refs[2 * n_pairs:2 * n_pairs + n_ex]
        outs = refs[2 * n_pairs + n_ex:2 * n_pairs + n_ex + n_out]
        accs = refs[2 * n_pairs + n_ex + n_out:]
        k = pl.program_id(2)

        @pl.when(k == 0)
        def _():
            for acc in accs:
                acc[...] = jnp.zeros_like(acc)

        for p, (_, _, ai) in enumerate(pairs):
            accs[ai][...] += _dot(_bf(ab[2 * p][...]), _bf(ab[2 * p + 1][...]), dims)

        @pl.when(k == nk - 1)
        def _():
            res = epilogue([acc[...] for acc in accs], [e[...] for e in ex])
            for o, v in zip(outs, res):
                o[...] = v.astype(o.dtype)

    ex_specs = [pl.BlockSpec((tm, tn), lambda i, j, k: (i, j)) if kind == 'mn'
                else pl.BlockSpec((1, tn), lambda i, j, k: (0, j)) if kind == 'n'
                else pl.BlockSpec(e.shape, lambda i, j, k: (0, 0)) for e, kind in extras]
    outs = pl.pallas_call(
        body, grid=(M // tm, N // tn, nk),
        in_specs=[a_spec, b_spec] * n_pairs + ex_specs,
        out_specs=[pl.BlockSpec((tm, tn), lambda i, j, k: (i, j)) for _ in out_dtypes],
        out_shape=[jax.ShapeDtypeStruct((M, N), dt) for dt in out_dtypes],
        scratch_shapes=[pltpu.VMEM((tm, tn), F32) for _ in range(n_acc)],
        compiler_params=pltpu.CompilerParams(dimension_semantics=("arbitrary",) * 3, vmem_limit_bytes=VMEM_LIMIT),
        name=name)(*[t for a, b, _ in pairs for t in (a, b)], *[e for e, _ in extras])
    return outs


def mm_plain(name, mode, a, b, dtype=F32, scale=None, dep=None):
    ep = (lambda accs, ex: [accs[0]]) if scale is None else (lambda accs, ex: [accs[0] * scale])
    return mm_call(name, mode, [(a, b, 0)], 1, ep, [dtype], extras=[] if dep is None else [(dep, 'dep')])[0]


def scan_fwd(name, step_fn, n_outer, n_steps, params, consts, xs, states, ys):
    n_p, n_c, n_x, n_s, n_y = len(params), len(consts), len(xs), len(states), len(ys)

    def body(*refs):
        p_refs = refs[:n_p]
        c_refs = refs[n_p:n_p + n_c]
        x_refs = refs[n_p + n_c:n_p + n_c + n_x]
        o = n_p + n_c + n_x
        y_refs = refs[o:o + n_y]
        sv_refs = refs[o + n_y:o + n_y + n_s]
        st_refs = refs[o + n_y + n_s:]
        s = pl.program_id(1)

        @pl.when(s == 0)
        def _():
            for r in st_refs:
                r[...] = jnp.zeros_like(r)

        st = tuple(r[...] for r in st_refs)
        for sv, v in zip(sv_refs, st):
            sv[...] = v
        new_st, y = step_fn(tuple(r[...] for r in p_refs), st, tuple(r[...] for r in x_refs),
                            tuple(r[...] for r in c_refs))
        for r, v in zip(y_refs, y):
            r[...] = v.astype(r.dtype)
        for r, v in zip(st_refs, new_st):
            r[...] = v

    in_specs = ([pl.BlockSpec(p[1], (lambda o, s, f=p[2]: f(o))) for p in params]
                + [pl.BlockSpec(c[1], c[2]) for c in consts]
                + [pl.BlockSpec(x[1], x[2]) for x in xs])
    out_specs = ([pl.BlockSpec(b, f) for _, _, b, f in ys]
                 + [pl.BlockSpec((None, None) + tuple(sh), (lambda o, s, n=len(sh): (o, s) + (0,) * n)) for sh, _ in states])
    out_shape = ([jax.ShapeDtypeStruct(sh, dt) for sh, dt, _, _ in ys]
                 + [jax.ShapeDtypeStruct((n_outer, n_steps) + tuple(sh), dt) for sh, dt in states])
    outs = pl.pallas_call(
        body, grid=(n_outer, n_steps), in_specs=in_specs, out_specs=out_specs, out_shape=out_shape,
        scratch_shapes=[pltpu.VMEM(tuple(sh), dt) for sh, dt in states],
        compiler_params=pltpu.CompilerParams(dimension_semantics=("arbitrary", "arbitrary"), vmem_limit_bytes=VMEM_LIMIT),
        name=name)(*[p[0] for p in params], *[c[0] for c in consts], *[x[0] for x in xs])
    return tuple(outs[:n_y]), tuple(outs[n_y:])


def scan_bwd(name, step_fn, n_outer, n_steps, params, consts, xs, states, saved, dys, glob, bf16_copies=()):
    n_p, n_c, n_x, n_s, n_y = len(params), len(consts), len(xs), len(states), len(dys)
    n_cp = len(bf16_copies)
    rev = lambda f: (lambda o, s: f(o, n_steps - 1 - s))

    def body(*refs):
        p_refs = refs[:n_p]
        c_refs = refs[n_p:n_p + n_c]
        x_refs = refs[n_p + n_c:n_p + n_c + n_x]
        o = n_p + n_c + n_x
        sv_refs = refs[o:o + n_s]
        dy_refs = refs[o + n_s:o + n_s + n_y]
        o = o + n_s + n_y
        dx_refs = refs[o:o + n_x]
        dp_refs = refs[o + n_x:o + n_x + n_p]
        cp_refs = refs[o + n_x + n_p:o + n_x + n_p + n_cp]
        dst_refs = refs[o + n_x + n_p + n_cp:]
        oi, s = pl.program_id(0), pl.program_id(1)

        @pl.when(s == 0)
        def _():
            for r in dst_refs:
                r[...] = jnp.zeros_like(r)

        for r, g in zip(dp_refs, glob):
            @pl.when(((s == 0) & (oi == 0)) if g else (s == 0))
            def _(r=r):
                r[...] = jnp.zeros_like(r)

        c_vals = tuple(r[...] for r in c_refs)
        f = lambda p, st, x: step_fn(p, st, x, c_vals)
        _, vjp = jax.vjp(f, tuple(r[...] for r in p_refs), tuple(r[...] for r in sv_refs), tuple(r[...] for r in x_refs))
        dp, dst, dx = vjp((tuple(r[...] for r in dst_refs), tuple(r[...] for r in dy_refs)))
        for r, v in zip(dx_refs, dx):
            r[...] = v.astype(r.dtype)
        for r, i in zip(cp_refs, bf16_copies):
            r[...] = dx[i].astype(r.dtype)
        for r, v in zip(dst_refs, dst):
            r[...] = v
        for r, v in zip(dp_refs, dp):
            r[...] += v

    gshape = lambda t: t[3] if len(t) > 3 else t[0].shape
    gidx = lambda t: t[4] if len(t) > 3 else t[2]
    in_specs = ([pl.BlockSpec(p[1], (lambda o, s, f=p[2]: f(o))) for p in params]
                + [pl.BlockSpec(c[1], rev(c[2])) for c in consts]
                + [pl.BlockSpec(x[1], rev(x[2])) for x in xs]
                + [pl.BlockSpec((None, None) + tuple(sh), (lambda o, s, n=len(sh): (o, n_steps - 1 - s) + (0,) * n)) for sh, _ in states]
                + [pl.BlockSpec(b, rev(f)) for _, b, f in dys])
    out_specs = ([pl.BlockSpec(x[1], rev(gidx(x))) for x in xs]
                 + [pl.BlockSpec(p[1], (lambda o, s, f=gidx(p): f(o))) for p in params]
                 + [pl.BlockSpec(xs[i][1], rev(gidx(xs[i]))) for i in bf16_copies])
    out_shape = ([jax.ShapeDtypeStruct(gshape(x), F32) for x in xs]
                 + [jax.ShapeDtypeStruct(gshape(p), F32) for p in params]
                 + [jax.ShapeDtypeStruct(gshape(xs[i]), BF16) for i in bf16_copies])
    outs = pl.pallas_call(
        body, grid=(n_outer, n_steps), in_specs=in_specs, out_specs=out_specs, out_shape=out_shape,
        scratch_shapes=[pltpu.VMEM(tuple(sh), dt) for sh, dt in states],
        compiler_params=pltpu.CompilerParams(dimension_semantics=("arbitrary", "arbitrary"), vmem_limit_bytes=VMEM_LIMIT),
        name=name)(*[p[0] for p in params], *[c[0] for c in consts], *[x[0] for x in xs], *saved,
                   *[d[0] for d in dys])
    if n_cp:
        return tuple(outs[:n_x]), tuple(outs[n_x:n_x + n_p]), tuple(outs[n_x + n_p:])
    return tuple(outs[:n_x]), tuple(outs[n_x:])


def _rows(a, rt):
    return (a, (rt, a.shape[1]), lambda o, s: (s, 0))


def _whole(a):
    return (a, a.shape, lambda o: (0,) * a.ndim)


def _rms(h, w):
    return h * lax.rsqrt(jnp.mean(h * h, axis=1, keepdims=True) + EPS) * w


def rms_fwd(name, h, w, dtype):
    TP, D = h.shape
    rt = _pick(TP, (384, 128))
    step = lambda p, st, x, c: ((), (_rms(x[0], p[0]),))
    (y,), _ = scan_fwd(name, step, 1, TP // rt, [_whole(w)], [], [_rows(h, rt)], [],
                       [((TP, D), dtype, (rt, D), lambda o, s: (s, 0))])
    return y


def rms_bwd(name, h, w, dxn, dres):
    TP, D = h.shape
    rt = _pick(TP, (384, 128))
    step = lambda p, st, x, c: ((), (_rms(x[0], p[0]), x[0]))
    (dh,), (dw,), (dh_bf,) = scan_bwd(name, step, 1, TP // rt, [_whole(w)], [], [_rows(h, rt)], [], [],
                                      [_rows(dxn, rt), _rows(dres, rt)], [True], bf16_copies=(0,))
    return (dh, dh_bf), dw


def loss_call(h, w, tgt, mask):
    TP, D = h.shape
    rt = _pick(TP, (384, 128))

    def body(h_ref, w_ref, t_ref, m_ref, loss_ref, dh_ref, dw_ref, dhb_ref):
        s = pl.program_id(0)

        @pl.when(s == 0)
        def _():
            loss_ref[...] = jnp.zeros_like(loss_ref)
            dw_ref[...] = jnp.zeros_like(dw_ref)

        tg, mk = t_ref[...], m_ref[...]

        def f(wv, hv):
            err = jnp.square(_rms(hv, wv) - tg) * mk
            return 0.5 * jnp.sum(jnp.sum(err, axis=1, keepdims=True), axis=0, keepdims=True) / D

        l, vjp = jax.vjp(f, w_ref[...], h_ref[...])
        dw, dh = vjp(jnp.ones((1, 1), F32))
        loss_ref[...] += l
        dw_ref[...] += dw
        dh_ref[...] = dh
        dhb_ref[...] = dh.astype(dhb_ref.dtype)

    row = lambda wd: pl.BlockSpec((rt, wd), lambda s: (s, 0))
    const = lambda shape: pl.BlockSpec(shape, lambda s: (0, 0))
    return pl.pallas_call(
        body, grid=(TP // rt,), in_specs=[row(D), const((1, D)), row(D), row(1)],
        out_specs=[const((1, 1)), row(D), const((1, D)), row(D)],
        out_shape=[jax.ShapeDtypeStruct((1, 1), F32), jax.ShapeDtypeStruct((TP, D), F32), jax.ShapeDtypeStruct((1, D), F32),
                   jax.ShapeDtypeStruct((TP, D), BF16)],
        compiler_params=pltpu.CompilerParams(dimension_semantics=("arbitrary",), vmem_limit_bytes=VMEM_LIMIT),
        name="loss_head")(h, w, tgt, mask)


def colsum(name, a):
    TP, N = a.shape
    rt = _pick(TP, (384, 128))

    def body(a_ref, o_ref):
        @pl.when(pl.program_id(0) == 0)
        def _():
            o_ref[...] = jnp.zeros_like(o_ref)
        o_ref[...] += jnp.sum(a_ref[...], axis=0, keepdims=True)

    return pl.pallas_call(
        body, grid=(TP // rt,), in_specs=[pl.BlockSpec((rt, N), lambda s: (s, 0))],
        out_specs=pl.BlockSpec((1, N), lambda s: (0, 0)), out_shape=jax.ShapeDtypeStruct((1, N), F32),
        compiler_params=pltpu.CompilerParams(dimension_semantics=("arbitrary",)), name=name)(a)


def ffn_fwd(tag, h, nw, wg_t, wu_t, wd):
    xn = rms_fwd(tag + "_norm", h, nw, BF16)

    def ep_up(accs, ex):
        g, u = accs
        sg = jax.nn.sigmoid(g)
        silu = g * sg
        return [u * (sg * (1.0 + g * (1.0 - sg))), silu, silu * u]

    g, u, a = mm_call(tag + "_up", 'nt', [(xn, wg_t, 0), (xn, wu_t, 1)], 2, ep_up, [BF16, BF16, BF16])
    h2 = mm_call(tag + "_down", 'nn', [(a, wd, 0)], 1, lambda accs, ex: [ex[0] + 0.5 * accs[0]], [F32],
                 extras=[(h, 'mn')])[0]
    return h2, (h, xn, g, u, a)


def ffn_bwd(tag, dh2, res, nw, wg_t, wu_t, wd, dep, send):
    h, xn, g, u, a = res
    dh2, dhb = dh2

    def ep_act(accs, ex):
        da = 0.5 * accs[0]
        return [da * ex[0].astype(F32), da * ex[1].astype(F32)]

    dg, du = mm_call(tag + "_dact", 'nt', [(dhb, wd, 0)], 1, ep_act, [BF16, BF16],
                     extras=[(g, 'mn'), (u, 'mn'), (dep, 'dep')])
    dwd = mm_plain(tag + "_dwd", 'tn', a, dhb, BF16, scale=0.5)
    tok = send('d', [dwd])
    dwg_t = mm_plain(tag + "_dwg", 'tn', dg, xn, BF16, dep=tok)
    dwu_t = mm_plain(tag + "_dwu", 'tn', du, xn, BF16)
    tok = send('gu', [dwg_t, dwu_t])
    dxn = mm_call(tag + "_dxn", 'nn', [(dg, wg_t, 0), (du, wu_t, 0)], 1, lambda accs, ex: [accs[0]], [F32],
                  extras=[(tok, 'dep')])[0]
    dh, dnw = rms_bwd(tag + "_dnorm", h, nw, dxn, dh2)
    return dh, dnw, tok


def _mlstm_step(params, state, xs, consts, *, dk):
    bif, nw = params
    c_st, n_st, m_st = state
    q, k, v, og, gr, grc = xs
    C = MLSTM_CHUNK
    R = q.shape[0]
    dv = v.shape[1]
    n = R // C
    ri, ci = _iota((R, R), 0), _iota((R, R), 1)
    sh = C.bit_length() - 1
    same = (ri >> sh) == (ci >> sh)
    causal = same & (ci <= ri)
    q = q * (dk ** -0.5)
    li_row = gr[0:1, :] + bif[0:1, 0:1]
    li_col = grc[:, 0:1] + bif[0:1, 0:1]
    b_row = _mm32(jax.nn.log_sigmoid(gr[1:2, :] + bif[1:2, 0:1]), (same & (ri <= ci)).astype(F32))
    b_col = _mm32(causal.astype(F32), jax.nn.log_sigmoid(grc[:, 1:2] + bif[1:2, 0:1]))
    m_in, m_out, b_end, a_init = [], [], [], []
    for j in range(n):
        sl = slice(j * C, (j + 1) * C)
        b_last = b_row[:, (j + 1) * C - 1:(j + 1) * C]
        log_end = b_last - b_row[:, sl] + li_row[:, sl]
        m_new = lax.stop_gradient(jnp.maximum(b_last + m_st, jnp.max(log_end, axis=1, keepdims=True)))
        m_in.append(m_st)
        m_out.append(m_new)
        b_end.append(b_last)
        a_init.append(jnp.exp(b_last + m_st - m_new))
        m_st = m_new
    rows = lambda vals: jnp.concatenate([jnp.broadcast_to(x, (C, x.shape[1])) for x in vals], axis=0)
    log_w = jnp.where(causal, b_col - b_row + li_row, -jnp.inf)
    log_init = b_col + rows(m_in)
    m_t = lax.stop_gradient(jnp.maximum(log_init, jnp.max(log_w, axis=1, keepdims=True)))
    w_init = jnp.exp(log_init - m_t)
    qk = _mm_nt(q, k) * jnp.exp(log_w - m_t)
    ka = k * jnp.exp(rows(b_end) - b_col + li_col - rows(m_out))
    row_chunk = _iota((R, 1), 0) >> sh
    kv = _mm_tn(jnp.concatenate([jnp.where(row_chunk == j, ka, 0.0) for j in range(n)], axis=1), v)
    cs, ns = [c_st], [n_st]
    for j in range(n):
        cs.append(a_init[j] * cs[j] + kv[j * dk:(j + 1) * dk])
        ns.append(a_init[j] * ns[j] + jnp.sum(ka[j * C:(j + 1) * C], axis=0, keepdims=True))
    qc = _mm(q, jnp.concatenate(cs[:n], axis=1))
    qc = jnp.concatenate([qc[j * C:(j + 1) * C, j * dv:(j + 1) * dv] for j in range(n)], axis=0)
    num = w_init * qc + _mm(qk, v)
    den = w_init * jnp.sum(q * rows(ns[:n]), axis=1, keepdims=True) + jnp.sum(qk, axis=1, keepdims=True)
    h = num / jnp.maximum(jnp.abs(den), jnp.exp(-m_t))
    hn = h * lax.rsqrt(jnp.mean(h * h, axis=1, keepdims=True) + EPS)
    return (cs[n], ns[n], m_st), (hn * nw * jax.nn.sigmoid(og),)


def _mlstm_ops(p, gr, grc, bif, nw):
    H = MLSTM_HEADS
    TP = p.shape[0]
    dv = nw.shape[1] // H
    dk = dv // 2
    R = ROW_TILE
    step = functools.partial(_mlstm_step, dk=dk)
    params = [(bif, (None, 2, LANES), lambda o: (o, 0, 0)), (nw, (1, dv), lambda o: (0, o))]
    col = lambda w, off: (p, (R, w), (lambda o, s: (s, off + o)), (TP, H * w), (lambda o, s: (s, o)))
    xs = [col(dk, 0), col(dk, H), col(dv, H), col(dv, 2 * H), (gr, (None, 2, R), lambda o, s: (o, 0, s)),
          (grc, (None, R, 2), lambda o, s: (o, s, 0))]
    states = [((dk, dv), F32), ((1, dk), F32), ((1, 1), F32)]
    ys = [((TP, H * dv), F32, (R, dv), lambda o, s: (s, o))]
    return step, H, TP // R, params, xs, states, ys


def mlstm_fwd(h, nw1, w_in_t, w_out, b_if, norm_w):
    H = MLSTM_HEADS
    D = h.shape[1]
    u = rms_fwd("mlstm_norm", h, nw1, BF16)
    p = mm_plain("mlstm_in", 'nt', u, w_in_t)
    gates = p[:, 3 * D:3 * D + 2 * H]
    gr = gates.T.reshape(2, H, -1).transpose(1, 0, 2)
    grc = gates.reshape(-1, 2, H).transpose(2, 0, 1)
    bif = jnp.broadcast_to(b_if.reshape(2, H).T[:, :, None], (H, 2, LANES))
    step, _, n, params, xs, states, ys = _mlstm_ops(p, gr, grc, bif, norm_w)
    (act,), saved = scan_fwd("mlstm_core", step, H, n, params, [], xs, states, ys)
    h2 = mm_call("mlstm_out", 'nn', [(act, w_out, 0)], 1, lambda accs, ex: [ex[0] + accs[0]], [F32], extras=[(h, 'mn')])[0]
    return h2, (h, u, p, gr, grc, bif, saved, act)


def mlstm_bwd(dh2, res, nw1, w_in_t, w_out, norm_w, dep):
    h, u, p, gr, grc, bif, saved, act = res
    H = MLSTM_HEADS
    TP = h.shape[0]
    dh2, dhb = dh2
    dact = mm_plain("mlstm_dact", 'nt', dhb, w_out, dep=dep)
    dw_out = mm_plain("mlstm_dwout", 'tn', act, dhb, BF16)
    step, _, n, params, xs, states, ys = _mlstm_ops(p, gr, grc, bif, norm_w)
    (dq, dk, dv, dog, dgr, dgrc), (dbif, dnorm) = scan_bwd("mlstm_core_bwd", step, H, n, params, [], xs, states, saved,
                                                           [(dact, ys[0][2], ys[0][3])], [False, False])
    dgates = dgr.transpose(1, 0, 2).reshape(2 * H, TP).T + dgrc.transpose(1, 2, 0).reshape(TP, 2 * H)
    pad = p.shape[1] - (dq.shape[1] + dk.shape[1] + dv.shape[1] + dog.shape[1] + 2 * H)
    dp = jnp.concatenate([dq, dk, dv, dog, dgates, jnp.zeros((TP, pad), F32)], axis=1)
    du = mm_plain("mlstm_du", 'nn', dp, w_in_t)
    dw_in_t = mm_plain("mlstm_dwin", 'tn', dp, u, BF16)
    dh, dnw1 = rms_bwd("mlstm_dnorm", h, nw1, du, dh2)
    db_if = dbif[:, :, 0].T.reshape(1, 2 * H)
    return dh, dict(nw=dnw1, w_in_t=dw_in_t, w_out=dw_out, b_if=db_if, norm_w=dnorm)


def _pool_step(params, state, xs, consts):
    w, scale = params
    (prev,) = state
    u, h = xs
    pos, win = consts
    R = u.shape[0]
    wn = win[0:1, 0:1]
    ext = jnp.concatenate([prev, u], axis=0)
    lag = _iota((R, 2 * R), 0) + R - _iota((R, 2 * R), 1)
    band = ((lag >= 0) & (lag < wn)).astype(F32)
    wsum = _mm32(band, ext)
    cnt = jnp.minimum(pos + 1, wn).astype(F32)
    pooled = wsum / cnt - u
    return (u,), (h + _mm(pooled, w) * scale,)


def _pool_ops(u, h, w, scale, pos):
    TP, D = u.shape
    G = D // N_POOL
    R = ROW_TILE
    win = jnp.broadcast_to(jnp.array([2 << g for g in range(N_POOL)], jnp.int32)[:, None, None], (N_POOL, 1, LANES))
    params = [(w, (None, G, G), lambda o: (o, 0, 0)), (scale, (1, G), lambda o: (0, o))]
    consts = [(pos, (R, 1), lambda o, s: (s, 0)), (win, (None, 1, LANES), lambda o, s: (o, 0, 0))]
    grp = lambda a: (a, (R, G), lambda o, s: (s, o))
    return N_POOL, TP // R, params, consts, [grp(u), grp(h)], [((R, G), F32)], [((TP, D), F32, (R, G), lambda o, s: (s, o))]


def pool_fwd(h, nw1, w, scale, pos):
    u = rms_fwd("pool_norm", h, nw1, F32)
    no, n, params, consts, xs, states, ys = _pool_ops(u, h, w, scale, pos)
    (h2,), saved = scan_fwd("pool_core", _pool_step, no, n, params, consts, xs, states, ys)
    return h2, (h, u, saved)


def pool_bwd(dh2, res, nw1, w, scale, pos):
    h, u, saved = res
    dh2 = dh2[0]
    no, n, params, consts, xs, states, ys = _pool_ops(u, h, w, scale, pos)
    (du, dres), (dw, dscale) = scan_bwd("pool_core_bwd", _pool_step, no, n, params, consts, xs, states, saved,
                                        [(dh2, ys[0][2], ys[0][3])], [False, False])
    dh, dnw1 = rms_bwd("pool_dnorm", h, nw1, du, dres)
    return dh, dict(nw=dnw1, w=dw, scale=dscale)


def _unit_lower_inverse(low, width):
    n = low.shape[0]
    ri, ci = _iota((n, n), 0), _iota((n, n), 1)
    inv = (ri == ci).astype(F32)
    b = 1
    while b < width:
        blk = 2 * b
        sh = blk.bit_length() - 1
        off = jnp.where(((ri >> sh) == (ci >> sh)) & ((ri & (blk - 1)) >= b) & ((ci & (blk - 1)) < b), low, 0.0)
        inv = inv - (off if b == 1 else _mm(_mm(inv, off), inv))
        b = blk
    return inv


def _unit_lower_solve(width):
    @jax.custom_vjp
    def solve(low, rhs):
        return _mm3(_unit_lower_inverse(low, width), rhs)

    def fwd(low, rhs):
        inv = _unit_lower_inverse(low, width)
        sol = _mm3(inv, rhs)
        return sol, (inv, sol)

    def bwd(saved, g):
        inv, sol = saved
        d_rhs = _dot(inv, g, ((0,), (0,)), precision=HIGH)
        return -_dot(d_rhs, sol, ((1,), (1,)), precision=HIGH), d_rhs

    solve.defvjp(fwd, bwd)
    return solve


def _conv_silu(prev, x, w):
    R = x.shape[0]
    ext = jnp.concatenate([prev, x], axis=0)
    y = sum(w[j:j + 1, :] * ext[8 - (GDN_CONV - 1) + j:8 - (GDN_CONV - 1) + j + R] for j in range(GDN_CONV))
    return jax.nn.silu(y)


def _gdn_step(params, state, xs, consts):
    cwq, cwk, cwv, ad, gnw = params
    s_cat, pq, pk, pv = state
    q, k, v, z, gb, gbc = xs
    n = gb.shape[0]
    dk = q.shape[1] // n
    one = lambda a, i, w: a[:, i * w:(i + 1) * w]
    states, outs = [], []
    for i in range(n):
        st, (y,) = _gdn_head_step(
            (one(cwq, i, dk), one(cwk, i, dk), one(cwv, i, 2 * dk), ad[i], gnw),
            (one(s_cat, i, 2 * dk), one(pq, i, dk), one(pk, i, dk), one(pv, i, 2 * dk)),
            (one(q, i, dk), one(k, i, dk), one(v, i, 2 * dk), one(z, i, 2 * dk), gb[i], gbc[i]), consts)
        states.append(st)
        outs.append(y)
    cat = lambda j: jnp.concatenate([st[j] for st in states], axis=1)
    return (cat(0), cat(1), cat(2), cat(3)), (jnp.concatenate(outs, axis=1),)


def _gdn_head_step(params, state, xs, consts):
    cwq, cwk, cwv, ad, gnw = params
    s_cat, pq, pk, pv = state
    q, k, v, z, gb, gbc = xs
    C = GDN_CHUNK
    R, dk = q.shape
    dv = v.shape[1] // 2
    ri, ci = _iota((R, R), 0), _iota((R, R), 1)
    sh = C.bit_length() - 1
    same = (ri >> sh) == (ci >> sh)
    causal, strict = same & (ci <= ri), same & (ci < ri)
    qc, kc, vc = _conv_silu(pq, q, cwq), _conv_silu(pk, k, cwk), _conv_silu(pv, v, cwv)
    qn = qc * lax.rsqrt(jnp.sum(qc * qc, axis=1, keepdims=True) + EPS) * (dk ** -0.5)
    kn = kc * lax.rsqrt(jnp.sum(kc * kc, axis=1, keepdims=True) + EPS)
    kk, qk = _mm_nt(kn, kn), _mm_nt(qn, kn)
    g_rows = -jnp.exp(ad[0:2, 0:1]) * jax.nn.softplus(gb[2:4, :] + ad[2:4, 0:1])
    gc_rows = _mm3(g_rows, (same & (ri <= ci)).astype(F32))
    g_cols = jnp.concatenate([-jnp.exp(ad[e:e + 1, 0:1]) * jax.nn.softplus(gbc[:, 2 + e:3 + e] + ad[2 + e:3 + e, 0:1])
                              for e in range(2)], axis=1)
    gc_cols = _mm3(causal.astype(F32), g_cols)
    beta_cols = jax.nn.sigmoid(gbc[:, 0:2])
    lows, rhss, attns, qgs = [], [], [], []
    for e in range(2):
        gcc, bc = gc_cols[:, e:e + 1], beta_cols[:, e:e + 1]
        decay = jnp.exp(jnp.where(causal, gcc - gc_rows[e:e + 1, :], -jnp.inf))
        lows.append(jnp.where(strict, kk * bc * decay, 0.0))
        attns.append(qk * decay)
        eg = jnp.exp(gcc)
        rhss.append(jnp.concatenate([vc[:, e * dv:(e + 1) * dv] * bc, kn * (bc * eg)], axis=1))
        qgs.append(qn * eg)
    zero = jnp.zeros((R, R), F32)
    big = jnp.concatenate([jnp.concatenate([lows[0], zero], axis=1), jnp.concatenate([zero, lows[1]], axis=1)], axis=0)
    sol = _unit_lower_solve(C)(big, jnp.concatenate(rhss, axis=0))
    outs = [[], []]
    n_chunks = R // C
    for j in range(n_chunks):
        sl = slice(j * C, (j + 1) * C)
        lhs = jnp.concatenate([sol[sl, dv:], sol[R + j * C:R + (j + 1) * C, dv:], qgs[0][sl], qgs[1][sl]], axis=0)
        pr = _mm(lhs, s_cat)
        v_new = [sol[e * R + j * C:e * R + (j + 1) * C, :dv] - pr[e * C:(e + 1) * C, e * dv:(e + 1) * dv] for e in range(2)]
        v_lanes = jnp.concatenate(v_new, axis=1)
        pad = jnp.zeros((C, 2 * dv), F32)
        v_rows = jnp.concatenate([v_lanes if i == j else pad for i in range(n_chunks)], axis=0)
        av = _mm(jnp.concatenate([attns[0][sl], attns[1][sl]], axis=0), v_rows)
        g_last = [gc_rows[e:e + 1, (j + 1) * C - 1:(j + 1) * C] for e in range(2)]
        kg = jnp.concatenate([kn[sl] * jnp.exp(g_last[e] - gc_cols[sl, e:e + 1]) for e in range(2)], axis=1)
        kv = _mm_tn(kg, v_lanes)
        s_cat = jnp.concatenate([jnp.exp(g_last[e]) * s_cat[:, e * dv:(e + 1) * dv]
                                 + kv[e * dk:(e + 1) * dk, e * dv:(e + 1) * dv] for e in range(2)], axis=1)
        for e in range(2):
            o = pr[(2 + e) * C:(3 + e) * C, e * dv:(e + 1) * dv] + av[e * C:(e + 1) * C, e * dv:(e + 1) * dv]
            on = o * lax.rsqrt(jnp.mean(o * o, axis=1, keepdims=True) + EPS) * gnw
            outs[e].append(on * jax.nn.silu(z[sl, e * dv:(e + 1) * dv]))
    out = jnp.concatenate([jnp.concatenate(outs[0], axis=0), jnp.concatenate(outs[1], axis=0)], axis=1)
    return (s_cat, q[R - 8:], k[R - 8:], v[R - 8:]), (out,)


def _gdn_ops(p, gb, gbc, conv_w, ad, gnw):
    TP = p.shape[0]
    dk = GDN_DK
    nqk = gb.shape[0]
    hp = GDN_HEADS_PER_STEP
    no = nqk // hp
    R = ROW_TILE
    cw = lambda w, off: (conv_w, (GDN_CONV, hp * w), (lambda o: (0, off + o)), (GDN_CONV, nqk * w), (lambda o: (0, o)))
    params = [cw(dk, 0), cw(dk, no), cw(2 * dk, no), (ad, (hp, 4, LANES), lambda o: (o, 0, 0)), _whole(gnw)]
    col = lambda w, off: (p, (R, hp * w), (lambda o, s: (s, off + o)), (TP, nqk * w), (lambda o, s: (s, o)))
    xs = [col(dk, 0), col(dk, no), col(2 * dk, no), col(2 * dk, 2 * no), (gb, (hp, 4, R), lambda o, s: (o, 0, s)),
          (gbc, (hp, R, 4), lambda o, s: (o, s, 0))]
    states = [((dk, hp * 2 * dk), F32), ((8, hp * dk), F32), ((8, hp * dk), F32), ((8, hp * 2 * dk), F32)]
    ys = [((TP, 2 * nqk * dk), F32, (R, hp * 2 * dk), lambda o, s: (s, o))]
    return no, TP // R, params, xs, states, ys


def gdn_fwd(h, nw1, w_in_t, w_out, conv_w, a_log, dt_bias, gnw):
    D = h.shape[1]
    nqk = D // GDN_DK
    u = rms_fwd("gdn_norm", h, nw1, BF16)
    p = mm_plain("gdn_in", 'nt', u, w_in_t)
    gates = p[:, 6 * D:6 * D + 4 * nqk]
    gb = gates.T.reshape(2, nqk, 2, -1).transpose(1, 0, 2, 3).reshape(nqk, 4, -1)
    gbc = gates.reshape(-1, 2, nqk, 2).transpose(2, 0, 1, 3).reshape(nqk, -1, 4)
    ad = jnp.concatenate([a_log.reshape(nqk, 2), dt_bias.reshape(nqk, 2)], axis=1)
    ad = jnp.broadcast_to(ad[:, :, None], (nqk, 4, LANES))
    no, n, params, xs, states, ys = _gdn_ops(p, gb, gbc, conv_w, ad, gnw)
    (act,), saved = scan_fwd("gdn_core", _gdn_step, no, n, params, [], xs, states, ys)
    h2 = mm_call("gdn_out", 'nn', [(act, w_out, 0)], 1, lambda accs, ex: [ex[0] + accs[0]], [F32], extras=[(h, 'mn')])[0]
    return h2, (h, u, p, gb, gbc, ad, saved, act)


def gdn_bwd(dh2, res, nw1, w_in_t, w_out, conv_w, gnw, dep):
    h, u, p, gb, gbc, ad, saved, act = res
    TP, D = h.shape
    nqk = D // GDN_DK
    dh2, dhb = dh2
    dact = mm_plain("gdn_dact", 'nt', dhb, w_out, dep=dep)
    dw_out = mm_plain("gdn_dwout", 'tn', act, dhb, BF16)
    no, n, params, xs, states, ys = _gdn_ops(p, gb, gbc, conv_w, ad, gnw)
    (dq, dk, dv, dz, dgb, dgbc), (dcq, dck, dcv, dad, dgnw) = scan_bwd(
        "gdn_core_bwd", _gdn_step, no, n, params, [], xs, states, saved, [(dact, ys[0][2], ys[0][3])],
        [False, False, False, False, True])
    dgates = (dgb.reshape(nqk, 2, 2, TP).transpose(1, 0, 2, 3).reshape(4 * nqk, TP).T
              + dgbc.reshape(nqk, TP, 2, 2).transpose(1, 2, 0, 3).reshape(TP, 4 * nqk))
    pad = p.shape[1] - (6 * D + 4 * nqk)
    dp = jnp.concatenate([dq, dk, dv, dz, dgates, jnp.zeros((TP, pad), F32)], axis=1)
    du = mm_plain("gdn_du", 'nn', dp, w_in_t)
    dw_in_t = mm_plain("gdn_dwin", 'tn', dp, u, BF16)
    dh, dnw1 = rms_bwd("gdn_dnorm", h, nw1, du, dh2)
    dad = dad[:, :, 0]
    return dh, dict(nw=dnw1, w_in_t=dw_in_t, w_out=dw_out, conv_w=jnp.concatenate([dcq, dck, dcv], axis=1),
                    a_log=dad[:, :2].reshape(1, 2 * nqk), dt_bias=dad[:, 2:].reshape(1, 2 * nqk), norm_w=dgnw)


def _rope(x, cos, sin):
    W = x.shape[1]
    first_half = (_iota(x.shape, 1) & (SWA_DH - 1)) < SWA_DH // 2
    rot = jnp.where(first_half, -_lane_roll(W - SWA_DH // 2)(x), _lane_roll(SWA_DH // 2)(x))
    return x * jnp.tile(cos, (1, W // LANES)) + rot * jnp.tile(sin, (1, W // LANES))


def _swa_step(params, state, xs, consts):
    (sinks,) = params
    kprev, vprev = state
    q, k, v = xs
    cos, sin, pos = consts
    R = q.shape[0]
    hkv = k.shape[1] // SWA_DH
    G = SWA_GROUP
    qr, kr = _rope(q, cos, sin), _rope(k, cos, sin)
    k2, v2 = jnp.concatenate([kprev, kr], axis=0), jnp.concatenate([vprev, v], axis=0)
    lane = _iota((R, LANES), 1)
    qpos = jnp.concatenate([pos] * G, axis=0)
    kpos = pos[0:1, 0:1] - R + _iota((1, 2 * R), 1)
    mask = (kpos <= qpos) & (qpos - kpos < SWA_WINDOW) & (kpos >= 0)
    sel_r, sel_c = _iota((hkv * SWA_DH, LANES), 0), _iota((hkv * SWA_DH, LANES), 1)
    out = []
    for hh in range(hkv):
        sel = _bf((sel_r == hh * SWA_DH + (sel_c & (SWA_DH - 1))).astype(F32))
        kd, vd = _mm(k2, sel), _mm(v2, sel)
        q8 = []
        for i in range(G // 2):
            qb = qr[:, (hh * G // 2 + i) * LANES:(hh * G // 2 + i + 1) * LANES]
            q8 += [jnp.where(lane < SWA_DH, qb, 0.0), jnp.where(lane >= SWA_DH, qb, 0.0)]
        s = _mm_nt(jnp.concatenate(q8, axis=0), kd) * (SWA_DH ** -0.5)
        s = jnp.where(mask, s, -jnp.inf)
        sink = jnp.concatenate([jnp.broadcast_to(sinks[0:1, hh * G + g:hh * G + g + 1], (R, 1)) for g in range(G)], axis=0)
        m = lax.stop_gradient(jnp.maximum(jnp.max(s, axis=1, keepdims=True), sink))
        e = jnp.exp(s - m)
        prob = e / (jnp.sum(e, axis=1, keepdims=True) + jnp.exp(sink - m))
        o8 = _mm(prob, vd)
        for i in range(G // 2):
            out.append(jnp.where(lane < SWA_DH, o8[2 * i * R:(2 * i + 1) * R], o8[(2 * i + 1) * R:(2 * i + 2) * R]))
    return (kr, v), (jnp.concatenate(out, axis=1),)


def _swa_ops(p, sinks, cos, sin, pos):
    TP = p.shape[0]
    R = ROW_TILE
    hq = sinks.shape[1]
    wq, wkv = hq * SWA_DH, hq // SWA_GROUP * SWA_DH
    nb = wq // wkv
    xs = [(p, (R, wq), (lambda o, s: (s, 0)), (TP, wq), (lambda o, s: (s, 0))),
          (p, (R, wkv), (lambda o, s: (s, nb)), (TP, wkv), (lambda o, s: (s, 0))),
          (p, (R, wkv), (lambda o, s: (s, nb + 1)), (TP, wkv), (lambda o, s: (s, 0)))]
    consts = [_rows(cos, R), _rows(sin, R), _rows(pos, R)]
    states = [((R, wkv), F32), ((R, wkv), F32)]
    ys = [((TP, wq), F32, (R, wq), lambda o, s: (s, 0))]
    return TP // R, [_whole(sinks)], consts, xs, states, ys


def swa_fwd(h, nw1, w_qkv_t, b_qkv, w_out, b_out, sinks, cos, sin, pos):
    u = rms_fwd("swa_norm", h, nw1, BF16)
    p = mm_call("swa_in", 'nt', [(u, w_qkv_t, 0)], 1, lambda accs, ex: [accs[0] + ex[0]], [F32], extras=[(b_qkv, 'n')])[0]
    n, params, consts, xs, states, ys = _swa_ops(p, sinks, cos, sin, pos)
    (act,), saved = scan_fwd("swa_core", _swa_step, 1, n, params, consts, xs, states, ys)
    h2 = mm_call("swa_out", 'nn', [(act, w_out, 0)], 1, lambda accs, ex: [ex[0] + accs[0] + ex[1]], [F32],
                 extras=[(h, 'mn'), (b_out, 'n')])[0]
    return h2, (h, u, p, saved, act)


def swa_bwd(dh2, res, nw1, w_qkv_t, w_out, sinks, cos, sin, pos, dep):
    h, u, p, saved, act = res
    dh2, dhb = dh2
    dact = mm_plain("swa_dact", 'nt', dhb, w_out, dep=dep)
    dw_out = mm_plain("swa_dwout", 'tn', act, dhb, BF16)
    db_out = colsum("swa_dbout", dh2)
    n, params, consts, xs, states, ys = _swa_ops(p, sinks, cos, sin, pos)
    (dq, dk, dv), (dsinks,) = scan_bwd("swa_core_bwd", _swa_step, 1, n, params, consts, xs, states, saved,
                                       [(dact, ys[0][2], ys[0][3])], [True])
    dp = jnp.concatenate([dq, dk, dv], axis=1)
    db_qkv = colsum("swa_dbqkv", dp)
    du = mm_plain("swa_du", 'nn', dp, w_qkv_t)
    dw_qkv_t = mm_plain("swa_dwqkv", 'tn', dp, u, BF16)
    dh, dnw1 = rms_bwd("swa_dnorm", h, nw1, du, dh2)
    return dh, dict(nw=dnw1, w_qkv_t=dw_qkv_t, w_out=dw_out, b_qkv=db_qkv, b_out=db_out, sinks=dsinks)


def _dev_index(dev):
    return 4 * dev[0] + 2 * dev[1] + dev[2]


def all_gather(name, shards):
    n = len(shards)

    def body(*refs):
        x_refs, out_refs = refs[:n], refs[n:2 * n]
        send_sems, recv_sems, local_sem = refs[2 * n:]
        x, y, c = lax.axis_index("x"), lax.axis_index("y"), lax.axis_index("c")
        me, sibling = (x, y, c), (x, y, 1 - c)
        chips = [(1 - x, y), (x, 1 - y), (1 - x, 1 - y)]

        def copy(a, k, block, to, src=None):
            dst = out_refs[a].at[_dev_index(block)]
            return pltpu.make_async_remote_copy(src_ref=dst if src is None else src, dst_ref=dst,
                                                send_sem=send_sems.at[a, k], recv_sem=recv_sems.at[a, k],
                                                device_id=to, device_id_type=MESH_ID)

        mine = [pltpu.make_async_copy(x_refs[a], out_refs[a].at[_dev_index(me)], local_sem.at[a]) for a in range(n)]
        first, passed = [], []
        for a in range(n):
            mine[a].start()
            first += [copy(a, 0, me, sibling, src=x_refs[a])]
            first += [copy(a, 1 + j, me, (*chip, c), src=x_refs[a]) for j, chip in enumerate(chips)]
        for cp in first:
            cp.start()
        for a in range(n):
            for j, chip in enumerate(chips):
                copy(a, 1 + j, (*chip, c), me).wait_recv()
                fwd = copy(a, 4 + j, (*chip, c), sibling)
                fwd.start()
                passed.append(fwd)
        for a in range(n):
            copy(a, 0, sibling, me).wait_recv()
            for j, chip in enumerate(chips):
                copy(a, 4 + j, (*chip, 1 - c), me).wait_recv()
        for cp in first + passed:
            cp.wait_send()
        for cp in mine:
            cp.wait()

    any_spec = pl.BlockSpec(memory_space=pl.ANY)
    return pl.pallas_call(
        body, in_specs=[any_spec] * n, out_specs=[any_spec] * n,
        out_shape=[jax.ShapeDtypeStruct((N_DEV,) + s.shape, s.dtype) for s in shards],
        scratch_shapes=[pltpu.SemaphoreType.DMA((n, 7)), pltpu.SemaphoreType.DMA((n, 7)), pltpu.SemaphoreType.DMA((n,))],
        name=name)(*shards)


def exchange_blocks(name, fulls):
    n = len(fulls)

    def body(*refs):
        g_refs, out_refs = refs[:n], refs[n:2 * n]
        send_sems, recv_sems, local_sem = refs[2 * n:]
        x, y, c = lax.axis_index("x"), lax.axis_index("y"), lax.axis_index("c")
        me = (x, y, c)
        peers = [(1 - x if r & 4 else x, 1 - y if r & 2 else y, 1 - c if r & 1 else c) for r in range(1, N_DEV)]

        def copy(a, k, peer):
            return pltpu.make_async_remote_copy(src_ref=g_refs[a].at[_dev_index(peer)], dst_ref=out_refs[a].at[_dev_index(me)],
                                                send_sem=send_sems.at[a, k], recv_sem=recv_sems.at[a, k],
                                                device_id=peer, device_id_type=MESH_ID)

        mine = [pltpu.make_async_copy(g_refs[a].at[_dev_index(me)], out_refs[a].at[_dev_index(me)], local_sem.at[a])
                for a in range(n)]
        sends = [copy(a, k, peer) for a in range(n) for k, peer in enumerate(peers)]
        for cp in mine + sends:
            cp.start()
        for a in range(n):
            for k, peer in enumerate(peers):
                pltpu.make_async_remote_copy(src_ref=g_refs[a].at[_dev_index(peer)], dst_ref=out_refs[a].at[_dev_index(peer)],
                                             send_sem=send_sems.at[a, k], recv_sem=recv_sems.at[a, k],
                                             device_id=peer, device_id_type=MESH_ID).wait_recv()
        for cp in sends:
            cp.wait_send()
        for cp in mine:
            cp.wait()

    any_spec = pl.BlockSpec(memory_space=pl.ANY)
    return pl.pallas_call(
        body, in_specs=[any_spec] * n, out_specs=[any_spec] * n,
        out_shape=[jax.ShapeDtypeStruct(g.shape, g.dtype) for g in fulls],
        scratch_shapes=[pltpu.SemaphoreType.DMA((n, 7)), pltpu.SemaphoreType.DMA((n, 7)), pltpu.SemaphoreType.DMA((n,))],
        name=name)(*fulls)


def _peers_of(x, y, c):
    return [(1 - x if r & 4 else x, 1 - y if r & 2 else y, 1 - c if r & 1 else c) for r in range(1, N_DEV)]


def push_start(name, srcs, lands, after, gather):
    n = len(srcs)

    def body(*refs):
        src_refs, land_refs = refs[:n], refs[n:2 * n]
        send_sems, recv_sems = refs[2 * n + 1], refs[2 * n + 2]
        token = refs[-1]
        x, y, c = lax.axis_index("x"), lax.axis_index("y"), lax.axis_index("c")
        me = (x, y, c)
        for a in range(n):
            for k, peer in enumerate(_peers_of(x, y, c)):
                pltpu.make_async_remote_copy(
                    src_ref=src_refs[a] if gather else src_refs[a].at[_dev_index(peer)],
                    dst_ref=land_refs[a].at[_dev_index(me)], send_sem=send_sems.at[a * (N_DEV - 1) + k],
                    recv_sem=recv_sems.at[a * (N_DEV - 1) + k],
                    device_id=peer, device_id_type=MESH_ID).start()
        token[...] = jnp.zeros_like(token)

    hbm = pl.BlockSpec(memory_space=pltpu.HBM)
    sem = pl.BlockSpec(memory_space=pltpu.SEMAPHORE)
    outs = pl.pallas_call(
        body, name=name,
        out_shape=(pltpu.SemaphoreType.DMA((n * (N_DEV - 1),)), pltpu.SemaphoreType.DMA((n * (N_DEV - 1),)),
                   *[pltpu.HBM(s.shape, s.dtype) for s in srcs], *[pltpu.HBM(l.shape, l.dtype) for l in lands],
                   jax.ShapeDtypeStruct((8, LANES), F32)),
        in_specs=[hbm] * (2 * n) + [pl.BlockSpec(memory_space=pl.ANY)],
        out_specs=(sem, sem, *[hbm] * (2 * n), pl.BlockSpec(memory_space=pltpu.VMEM)),
        input_output_aliases={i: 2 + i for i in range(2 * n)},
        compiler_params=pltpu.CompilerParams(has_side_effects=pltpu.SideEffectType.DATAFLOW_SIDE_EFFECTING),
    )(*[pltpu.with_memory_space_constraint(s, pltpu.HBM) for s in srcs],
      *[pltpu.with_memory_space_constraint(l, pltpu.HBM) for l in lands], after)
    return (outs[0], outs[1], outs[2:2 + n], outs[2 + n:2 + 2 * n], gather), outs[-1]


def push_wait(name, handle, after):
    send_sems, recv_sems, srcs, lands, gather = handle
    n = len(srcs)

    def body(*refs):
        src_refs, land_refs = refs[:n], refs[n:2 * n]
        send_sem_ref, recv_sem_ref = refs[2 * n], refs[2 * n + 1]
        x, y, c = lax.axis_index("x"), lax.axis_index("y"), lax.axis_index("c")
        for a in range(n):
            for k, peer in enumerate(_peers_of(x, y, c)):
                cp = pltpu.make_async_remote_copy(
                    src_ref=src_refs[a] if gather else src_refs[a].at[_dev_index(peer)],
                    dst_ref=land_refs[a].at[_dev_index(peer)], send_sem=send_sem_ref.at[a * (N_DEV - 1) + k],
                    recv_sem=recv_sem_ref.at[a * (N_DEV - 1) + k],
                    device_id=peer, device_id_type=MESH_ID)
                cp.wait_send()
                cp.wait_recv()

    hbm = pl.BlockSpec(memory_space=pltpu.HBM)
    sem = pl.BlockSpec(memory_space=pltpu.SEMAPHORE)
    outs = pl.pallas_call(
        body, name=name,
        out_shape=(*[pltpu.HBM(s.shape, s.dtype) for s in srcs], *[pltpu.HBM(l.shape, l.dtype) for l in lands]),
        in_specs=[hbm] * (2 * n) + [sem, sem, pl.BlockSpec(memory_space=pl.ANY)],
        out_specs=tuple([hbm] * (2 * n)),
        input_output_aliases={i: i for i in range(2 * n)},
        compiler_params=pltpu.CompilerParams(has_side_effects=pltpu.SideEffectType.DATAFLOW_SIDE_EFFECTING),
    )(*srcs, *lands, send_sems, recv_sems, after)
    return outs[n:]


def _split_call(name, body, ins, sem_ins, after, n_sem_out, with_token):
    hbm = pl.BlockSpec(memory_space=pltpu.HBM)
    sem = pl.BlockSpec(memory_space=pltpu.SEMAPHORE)
    n = len(ins)
    out_shape = ([pltpu.SemaphoreType.DMA((s,)) for s in n_sem_out] + [pltpu.HBM(a.shape, a.dtype) for a in ins]
                 + ([jax.ShapeDtypeStruct((8, LANES), F32)] if with_token else []))
    out_specs = [sem] * len(n_sem_out) + [hbm] * n + ([pl.BlockSpec(memory_space=pltpu.VMEM)] if with_token else [])
    outs = pl.pallas_call(
        body, name=name, out_shape=tuple(out_shape), out_specs=tuple(out_specs),
        in_specs=[hbm] * n + [sem] * len(sem_ins) + [pl.BlockSpec(memory_space=pl.ANY)],
        input_output_aliases={i: len(n_sem_out) + i for i in range(n)},
        compiler_params=pltpu.CompilerParams(has_side_effects=pltpu.SideEffectType.DATAFLOW_SIDE_EFFECTING),
    )(*ins, *sem_ins, after)
    k = len(n_sem_out)
    return outs[:k], outs[k:k + n], (outs[-1] if with_token else None)


def _chips_of(x, y):
    return [(1 - x, y), (x, 1 - y), (1 - x, 1 - y)]


def gather_start(name, shards, lands, after):
    n = len(shards)

    def body(*refs):
        src, land = refs[:n], refs[n:2 * n]
        send, recv = refs[2 * n + 1], refs[2 * n + 2]
        token = refs[-1]
        x, y, c = lax.axis_index("x"), lax.axis_index("y"), lax.axis_index("c")
        to = [(x, y, 1 - c)] + [(*chip, c) for chip in _chips_of(x, y)]
        for a in range(n):
            for k, dev in enumerate(to):
                pltpu.make_async_remote_copy(src_ref=src[a], dst_ref=land[a].at[_dev_index((x, y, c))],
                                             send_sem=send.at[4 * a + k], recv_sem=recv.at[4 * a + k],
                                             device_id=dev, device_id_type=MESH_ID).start()
        token[...] = jnp.zeros_like(token)

    ins = [pltpu.with_memory_space_constraint(a, pltpu.HBM) for a in list(shards) + list(lands)]
    sems, thru, token = _split_call(name, body, ins, [], after, [4 * n, 4 * n], True)
    return dict(n=n, send=sems[0], recv=sems[1], shards=thru[:n], lands=thru[n:]), token


def gather_pass(name, h, after):
    n = h['n']

    def body(*refs):
        land, recv = refs[:n], refs[n]
        send2, recv2 = refs[n + 2], refs[n + 3]
        token = refs[-1]
        x, y, c = lax.axis_index("x"), lax.axis_index("y"), lax.axis_index("c")
        for a in range(n):
            for j, chip in enumerate(_chips_of(x, y)):
                blk = land[a].at[_dev_index((*chip, c))]
                pltpu.make_async_remote_copy(src_ref=blk, dst_ref=blk, send_sem=send2.at[3 * a + j], recv_sem=recv.at[4 * a + 1 + j],
                                             device_id=(*chip, c), device_id_type=MESH_ID).wait_recv()
                pltpu.make_async_remote_copy(src_ref=blk, dst_ref=blk, send_sem=send2.at[3 * a + j], recv_sem=recv2.at[3 * a + j],
                                             device_id=(x, y, 1 - c), device_id_type=MESH_ID).start()
        token[...] = jnp.zeros_like(token)

    sems, lands, token = _split_call(name, body, list(h['lands']), [h['recv']], after, [3 * n, 3 * n], True)
    return dict(h, lands=lands, send2=sems[0], recv2=sems[1]), token


def gather_wait(name, h, after):
    n = h['n']

    def body(*refs):
        src, land = refs[:n], refs[n:2 * n]
        send, recv, send2, recv2 = refs[2 * n:2 * n + 4]
        x, y, c = lax.axis_index("x"), lax.axis_index("y"), lax.axis_index("c")
        sib = (x, y, 1 - c)
        for a in range(n):
            mine = land[a].at[_dev_index((x, y, c))]
            for k in range(4):
                pltpu.make_async_remote_copy(src_ref=src[a], dst_ref=mine, send_sem=send.at[4 * a + k], recv_sem=recv.at[4 * a + k],
                                             device_id=sib, device_id_type=MESH_ID).wait_send()
            theirs = land[a].at[_dev_index(sib)]
            pltpu.make_async_remote_copy(src_ref=src[a], dst_ref=theirs, send_sem=send.at[4 * a], recv_sem=recv.at[4 * a],
                                         device_id=sib, device_id_type=MESH_ID).wait_recv()
            for j, chip in enumerate(_chips_of(x, y)):
                sent, got = land[a].at[_dev_index((*chip, c))], land[a].at[_dev_index((*chip, 1 - c))]
                cp = pltpu.make_async_remote_copy(src_ref=sent, dst_ref=got, send_sem=send2.at[3 * a + j], recv_sem=recv2.at[3 * a + j],
                                                  device_id=sib, device_id_type=MESH_ID)
                cp.wait_send()
                cp.wait_recv()

    _, thru, _ = _split_call(name, body, list(h['shards']) + list(h['lands']), [h['send'], h['recv'], h['send2'], h['recv2']],
                             after, [], False)
    return thru[n:]


def _own_block(block, me):
    land = lax.empty((N_DEV,) + block.shape, block.dtype)
    return lax.dynamic_update_slice(land, block[None], (me,) + (0,) * block.ndim)


def sum_blocks(name, parts):
    _, r, c = parts.shape
    tr = _pick(r, ROW_TILES)

    def body(p_ref, o_ref):
        acc = p_ref[0].astype(F32)
        for b in range(1, N_DEV):
            acc = acc + p_ref[b].astype(F32)
        o_ref[...] = acc

    return pl.pallas_call(
        body, grid=(r // tr,), in_specs=[pl.BlockSpec((N_DEV, tr, c), lambda i: (0, i, 0))],
        out_specs=pl.BlockSpec((tr, c), lambda i: (i, 0)), out_shape=jax.ShapeDtypeStruct((r, c), F32),
        compiler_params=pltpu.CompilerParams(dimension_semantics=("arbitrary",), vmem_limit_bytes=VMEM_LIMIT),
        name=name)(parts)


def adamw(name, w, g, m, v):
    r, c = w.shape
    tr = _pick(r, [t for t in ROW_TILES if t * c * 4 * 7 * 2 <= VMEM_LIMIT // 2])

    def body(w_ref, g_ref, m_ref, v_ref, d_ref, nm_ref, nv_ref):
        gv = g_ref[...]
        nm = ADAM_B1 * m_ref[...] + (1.0 - ADAM_B1) * gv
        nv = ADAM_B2 * v_ref[...] + (1.0 - ADAM_B2) * jnp.square(gv)
        m_hat = nm / (1.0 - ADAM_B1 ** ADAM_STEP)
        v_hat = nv / (1.0 - ADAM_B2 ** ADAM_STEP)
        d_ref[...] = -ADAM_LR * (m_hat / (jnp.sqrt(v_hat) + ADAM_EPS) + ADAM_WD * w_ref[...])
        nm_ref[...] = nm
        nv_ref[...] = nv

    spec = pl.BlockSpec((tr, c), lambda i: (i, 0))
    return pl.pallas_call(
        body, grid=(r // tr,), in_specs=[spec] * 4, out_specs=[spec] * 3,
        out_shape=[jax.ShapeDtypeStruct((r, c), F32)] * 3,
        compiler_params=pltpu.CompilerParams(dimension_semantics=("arbitrary",), vmem_limit_bytes=VMEM_LIMIT),
        name=name)(w, g, m, v)


def _pack(arrays):
    flat = jnp.concatenate([a.reshape(-1) for a in arrays])
    n = _round_up(flat.shape[0], 8 * LANES)
    return jnp.pad(flat, (0, n - flat.shape[0])).reshape(-1, LANES)


def _unpack(packed, shapes):
    flat = packed.reshape(-1)
    out, o = [], 0
    for sh in shapes:
        sz = 1
        for d in sh:
            sz *= d
        out.append(flat[o:o + sz].reshape(sh))
        o += sz
    return out


def _gather_axis(g, ax):
    g = jnp.moveaxis(g, 0, ax)
    return g.reshape(g.shape[:ax] + (g.shape[ax] * g.shape[ax + 1],) + g.shape[ax + 2:])


def _comm_rows(a):
    r = a.shape[0]
    rp = r if r % 16 == 0 else _round_up(r, ROW_PAD)
    return jnp.pad(_bf(a), ((0, rp - r), (0, 0)))


def _natural(g, r, pad_to=None):
    full = g[:, :r].reshape(N_DEV * r, g.shape[2])
    if pad_to is not None and full.shape[0] % pad_to:
        full = jnp.pad(full, ((0, _round_up(full.shape[0], pad_to) - full.shape[0]), (0, 0)))
    return full


def _blocked(full, r):
    blocks = full[:N_DEV * r].reshape(N_DEV, r, full.shape[1])
    rp = r if r % 16 == 0 else _round_up(r, ROW_PAD)
    return jnp.pad(_bf(blocks), ((0, 0), (0, rp - r), (0, 0)))


def kernel(x, *rest):
    nw_ = len(WEIGHTS)
    W = dict(zip(WEIGHTS, rest[:nw_]))
    loss_target = rest[nw_]
    M = dict(zip(WEIGHTS, rest[nw_ + 1:2 * nw_ + 1]))
    V = dict(zip(WEIGHTS, rest[2 * nw_ + 1:3 * nw_ + 1]))

    T0, D = x.shape[1], x.shape[2]
    T = T0 + N_META
    TP = _round_up(T, ROW_TILE)
    me = 4 * lax.axis_index("x") + 2 * lax.axis_index("y") + lax.axis_index("c")

    small_sharded = [k for k in WEIGHTS if k in SMALL and SMALL[k] is not None]
    (sg,) = all_gather("ag_small", [_pack([W[k] for k in small_sharded])])
    per_dev = [_unpack(sg[b], [W[k].shape for k in small_sharded]) for b in range(N_DEV)]
    full = {k: _gather_axis(jnp.stack([per_dev[b][i] for b in range(N_DEV)]), SMALL[k]) for i, k in enumerate(small_sharded)}
    for k in SMALL:
        if SMALL[k] is None:
            full[k] = W[k]

    r_ff = W['ffn_w_gate'].shape[3]
    r_mi, r_mo = W['mlstm_w_in'].shape[2], W['mlstm_w_out'].shape[1]
    r_gi, r_go = W['gdn_w_in'].shape[2], W['gdn_w_out'].shape[1]
    r_si, r_so = W['swa_w_qkv'].shape[2], W['swa_w_out'].shape[1]
    pw = W['pool_w'][0]
    stages = [(i, j) for i in range(DEPTH) for j in range(3)]

    def stage_shards(i, j):
        if j != 1:
            s = j // 2
            return [_comm_rows(W['ffn_w_gate'][i, s].T), _comm_rows(W['ffn_w_up'][i, s].T), _comm_rows(W['ffn_w_down'][i, s])]
        if i % 4 == 0:
            return [_comm_rows(W['mlstm_w_in'][0].T), _comm_rows(W['mlstm_w_out'][0])]
        if i % 4 == 1:
            return [_bf(pw.reshape(-1, pw.shape[2]))]
        if i % 4 == 2:
            return [_comm_rows(W['gdn_w_in'][0].T), _comm_rows(W['gdn_w_out'][0])]
        return [_comm_rows(W['swa_w_qkv'][0].T), _comm_rows(W['swa_w_out'][0])]

    def stage_weights(i, j, g):
        if j != 1:
            return tuple(_natural(a, r_ff) for a in g)
        if i % 4 == 0:
            return (_natural(g[0], r_mi, IN_PAD), _natural(g[1], r_mo))
        if i % 4 == 1:
            return (g[0].reshape(N_DEV, N_POOL, pw.shape[1], pw.shape[2]).transpose(1, 0, 2, 3)
                    .reshape(N_POOL, pw.shape[2], pw.shape[2]).astype(F32),)
        if i % 4 == 2:
            return (_natural(g[0], r_gi, IN_PAD), _natural(g[1], r_go))
        return (_natural(g[0], r_si), _natural(g[1], r_so))

    def start_gather(k, after):
        sh = stage_shards(*stages[k])
        return gather_start(f"ag_start_{k}", sh, [_own_block(s, me) for s in sh], after)

    first = all_gather("ag_stage_0", stage_shards(*stages[0]))
    wts = {0: stage_weights(*stages[0], first)}
    pending, passed, zero = {}, set(), jnp.zeros((), F32)
    ahead = 3
    tok, after = zero, first[0]
    for k in range(1, ahead + 1):
        pending[k], after = start_gather(k, after)
        tok = tok + after[0, 0]

    pos = jnp.arange(TP, dtype=jnp.int32)[:, None]
    inv = ROPE_THETA ** (-jnp.arange(0, SWA_DH, 2, dtype=F32) / SWA_DH)
    ang = jnp.arange(TP, dtype=F32)[:, None] * inv[None, :]
    ang = jnp.concatenate([ang, ang, ang, ang], axis=1)
    cos, sin = jnp.cos(ang), jnp.sin(ang)
    row_mask = ((pos >= N_META) & (pos < T)).astype(F32)
    tgt = jnp.pad(loss_target[0], ((N_META, TP - T), (0, 0)))
    nrm = lambda i, j: full['norm_w'][i, j][None, :]

    h = jnp.concatenate([full['meta_tokens'], x[0], jnp.zeros((TP - T, D), F32)], axis=0)
    res = {}
    for k, (i, j) in enumerate(stages):
        if k >= 1:
            tok = zero
            if k not in passed:
                pending[k], _ = gather_pass(f"ag_pass_{k}", pending[k], h)
            wts[k] = stage_weights(i, j, gather_wait(f"ag_wait_{k}", pending.pop(k), h))
            if k + 1 < len(stages):
                pending[k + 1], t = gather_pass(f"ag_pass_{k + 1}", pending[k + 1], h)
                passed.add(k + 1)
                tok = tok + t[0, 0]
            if k + ahead < len(stages):
                pending[k + ahead], t = start_gather(k + ahead, h)
                tok = tok + t[0, 0]
        nw = nrm(i, j) + tok
        w = wts[k]
        if j != 1:
            h, res[k] = ffn_fwd(f"ffn_{i}_{j // 2}", h, nw, *w)
        elif i % 4 == 0:
            h, res[k] = mlstm_fwd(h, nw, w[0], w[1], full['mlstm_b_if'], full['mlstm_norm_w'])
        elif i % 4 == 1:
            h, res[k] = pool_fwd(h, nw, w[0], full['pool_scale'], pos)
        elif i % 4 == 2:
            h, res[k] = gdn_fwd(h, nw, w[0], w[1], full['gdn_conv_w'][0], full['gdn_a_log'], full['gdn_dt_bias'],
                                full['gdn_norm_w'])
        else:
            h, res[k] = swa_fwd(h, nw, w[0], full['swa_b_qkv'], w[1], full['swa_b_out'], full['swa_sinks'], cos, sin, pos)

    loss_local, dh_f32, d_final, dh_bf = loss_call(h, full['final_norm_w'][None, :], tgt, row_mask)
    dh = (dh_f32, dh_bf)
    loss = lax.psum(loss_local[0, 0], ("x", "y", "c"))

    gs = {'final_norm_w': d_final[0]}
    d_norm = [[None] * 3 for _ in range(DEPTH)]
    sent = []
    dep = jnp.zeros((8, LANES), F32)

    def exchange(k, part, blocks, after):
        lands = [_own_block(lax.dynamic_index_in_dim(b, me, 0, keepdims=False), me) for b in blocks]
        handle, token = push_start(f"rs_start_{k}_{part}", blocks, lands, after, False)
        sent.append((k, part, handle))
        return token

    for k in reversed(range(len(stages))):
        i, j = stages[k]
        nw, w = nrm(i, j), wts[k]
        if j != 1:
            send = lambda part, gs_, k=k: exchange(k, part, [_blocked(g, r_ff) for g in gs_], gs_[0])
            dh, d_norm[i][j], dep = ffn_bwd(f"ffn_{i}_{j // 2}", dh, res[k], nw, *w, dep, send)
            continue
        else:
            if i % 4 == 0:
                dh, gm = mlstm_bwd(dh, res[k], nw, w[0], w[1], full['mlstm_norm_w'], dep)
                blocks = [_blocked(gm['w_in_t'], r_mi), _blocked(gm['w_out'], r_mo)]
                gs.update(mlstm_b_if=gm['b_if'], mlstm_norm_w=gm['norm_w'])
            elif i % 4 == 1:
                dh, gm = pool_bwd(dh, res[k], nw + dep[0, 0], w[0], full['pool_scale'], pos)
                dw = gm['w'].reshape(N_POOL, N_DEV, pw.shape[1], pw.shape[2]).transpose(1, 0, 2, 3)
                blocks = [_bf(dw.reshape(N_DEV, N_POOL * pw.shape[1], pw.shape[2]))]
                gs.update(pool_scale=gm['scale'])
            elif i % 4 == 2:
                dh, gm = gdn_bwd(dh, res[k], nw, w[0], w[1], full['gdn_conv_w'][0], full['gdn_norm_w'], dep)
                blocks = [_blocked(gm['w_in_t'], r_gi), _blocked(gm['w_out'], r_go)]
                gs.update(gdn_conv_w=gm['conv_w'][None], gdn_a_log=gm['a_log'], gdn_dt_bias=gm['dt_bias'],
                          gdn_norm_w=gm['norm_w'])
            else:
                dh, gm = swa_bwd(dh, res[k], nw, w[0], w[1], full['swa_sinks'], cos, sin, pos, dep)
                blocks = [_blocked(gm['w_qkv_t'], r_si), _blocked(gm['w_out'], r_so)]
                gs.update(swa_b_qkv=gm['b_qkv'], swa_b_out=gm['b_out'], swa_sinks=gm['sinks'])
            d_norm[i][1] = gm['nw']
        dep = exchange(k, 'm', blocks, dh[0])
    gs['norm_w'] = jnp.stack([jnp.concatenate(d_norm[i], axis=0) for i in range(DEPTH)])
    gs['meta_tokens'] = dh[0][:N_META]
    grad_x = dh[0][N_META:T][None]

    grads = {}
    small_names = [k for k in WEIGHTS if k in SMALL]
    (parts,) = all_gather("ag_small_grads", [_pack([gs[k].reshape(full[k].shape) for k in small_names])])
    tot = _unpack(sum_blocks("sum_small_grads", parts), [full[k].shape for k in small_names])
    for k, g in zip(small_names, tot):
        ax = SMALL[k]
        grads[k] = g if ax is None else lax.dynamic_slice_in_dim(g, me * W[k].shape[ax], W[k].shape[ax], axis=ax)

    summed, after = {}, dh[0]
    for k, part, handle in sent:
        parts = push_wait(f"rs_wait_{k}_{part}", handle, after)
        summed[k, part] = [sum_blocks(f"sum_{k}_{part}_{n}", p) for n, p in enumerate(parts)]
        after = summed[k, part][0]

    gg, gu, gd = [], [], []
    for i in range(DEPTH):
        for j in (0, 2):
            a, b = summed[3 * i + j, 'gu']
            gg.append(a[:r_ff].T)
            gu.append(b[:r_ff].T)
            gd.append(summed[3 * i + j, 'd'][0][:r_ff])
    shape4 = lambda lst, ref: jnp.stack(lst).reshape(ref.shape)
    grads['ffn_w_gate'] = shape4(gg, W['ffn_w_gate'])
    grads['ffn_w_up'] = shape4(gu, W['ffn_w_up'])
    grads['ffn_w_down'] = shape4(gd, W['ffn_w_down'])
    for i in range(DEPTH):
        g = summed[3 * i + 1, 'm']
        if i % 4 == 0:
            grads['mlstm_w_in'], grads['mlstm_w_out'] = g[0][:r_mi].T[None], g[1][:r_mo][None]
        elif i % 4 == 1:
            grads['pool_w'] = g[0].reshape(W['pool_w'].shape)
        elif i % 4 == 2:
            grads['gdn_w_in'], grads['gdn_w_out'] = g[0][:r_gi].T[None], g[1][:r_go][None]
        else:
            grads['swa_w_qkv'], grads['swa_w_out'] = g[0][:r_si].T[None], g[1][:r_so][None]

    delta, new_m, new_v = {}, {}, {}
    shapes = [W[k].shape for k in small_names]
    d, nm, nv = adamw("adamw_small", _pack([W[k] for k in small_names]), _pack([grads[k] for k in small_names]),
                      _pack([M[k] for k in small_names]), _pack([V[k] for k in small_names]))
    for k, a, b, c_ in zip(small_names, _unpack(d, shapes), _unpack(nm, shapes), _unpack(nv, shapes)):
        delta[k], new_m[k], new_v[k] = a, b, c_
    for k in WEIGHTS:
        if k not in SMALL:
            two = lambda a: a.reshape(-1, a.shape[-1])
            d, nm, nv = adamw("adamw_" + k, two(W[k]), two(grads[k]), two(M[k]), two(V[k]))
            delta[k], new_m[k], new_v[k] = d.reshape(W[k].shape), nm.reshape(W[k].shape), nv.reshape(W[k].shape)

    return (loss, grad_x, *[grads[k].reshape(W[k].shape) for k in WEIGHTS], *[delta[k] for k in WEIGHTS],
            *[new_m[k] for k in WEIGHTS], *[new_v[k] for k in WEIGHTS])
```

```python
import functools

import jax
import jax.numpy as jnp
from jax import lax
from jax.experimental import pallas as pl
from jax.experimental.pallas import tpu as pltpu

F32 = jnp.float32
BF16 = jnp.bfloat16

N_DEV = 8
N_META = 16
EPS = 1e-6
DEPTH = 4
MLSTM_HEADS = 8
MLSTM_CHUNK = 64
N_POOL = 4
GDN_DK = 128
GDN_CHUNK = 64
GDN_CONV = 4
GDN_HEADS_PER_STEP = 2
SWA_DH = 64
SWA_GROUP = 8
SWA_WINDOW = 128
ROPE_THETA = 10000.0
ADAM_LR = 0.001
ADAM_B1 = 0.9
ADAM_B2 = 0.999
ADAM_EPS = 1e-08
ADAM_WD = 0.01
ADAM_STEP = 10

LANES = 128
ROW_TILE = 128
ROW_PAD = 112
IN_PAD = 896
VMEM_LIMIT = 56 * 1024 * 1024
TOKEN_TILES = (1056, 768, 512, 384, 256, 128)
FEATURE_TILES = (512, 896, 768, 640, 384, 256, 128)
ROW_TILES = (512, 352, 256, 176, 160, 128, 112, 64, 48, 32, 16, 8)
HIGHEST = lax.Precision.HIGHEST
HIGH = lax.Precision.HIGH
MESH_ID = pl.DeviceIdType.MESH

WEIGHTS = ('meta_tokens', 'norm_w', 'ffn_w_gate', 'ffn_w_up', 'ffn_w_down', 'mlstm_w_in', 'mlstm_b_if',
           'mlstm_norm_w', 'mlstm_w_out', 'pool_w', 'pool_scale', 'gdn_w_in', 'gdn_conv_w', 'gdn_a_log',
           'gdn_dt_bias', 'gdn_norm_w', 'gdn_w_out', 'swa_w_qkv', 'swa_b_qkv', 'swa_sinks', 'swa_w_out',
           'swa_b_out', 'final_norm_w')
SMALL = {'meta_tokens': 1, 'norm_w': 2, 'mlstm_b_if': None, 'mlstm_norm_w': None, 'pool_scale': 1,
         'gdn_conv_w': 2, 'gdn_a_log': None, 'gdn_dt_bias': None, 'gdn_norm_w': None, 'swa_b_qkv': 1,
         'swa_sinks': None, 'swa_b_out': 1, 'final_norm_w': None}


def _pick(n, cands):
    for c in cands:
        if n % c == 0:
            return c
    return n


def _round_up(n, m):
    return -(-n // m) * m


def _bf(x):
    return x.astype(BF16)


def _dot(a, b, dims, precision=None):
    return lax.dot_general(a, b, (dims, ((), ())), preferred_element_type=F32, precision=precision)


def _mm(a, b):
    return _dot(_bf(a), _bf(b), ((1,), (0,)))


def _mm_nt(a, b):
    return _dot(_bf(a), _bf(b), ((1,), (1,)))


def _mm_tn(a, b):
    return _dot(_bf(a), _bf(b), ((0,), (0,)))


def _mm32(a, b):
    return _dot(a, b, ((1,), (0,)), precision=HIGHEST)


def _mm3(a, b):
    return _dot(a, b, ((1,), (0,)), precision=HIGH)


def _iota(shape, axis):
    return lax.broadcasted_iota(jnp.int32, shape, axis)


def _row2col(row, eye):
    return jnp.sum(eye * row, axis=1, keepdims=True)


def _lane_roll(shift):
    @jax.custom_vjp
    def f(x):
        return pltpu.roll(x, shift, 1)

    def fwd(x):
        return f(x), None

    def bwd(_, g):
        return (pltpu.roll(g, g.shape[1] - shift, 1),)

    f.defvjp(fwd, bwd)
    return f


def mm_call(name, mode, pairs, n_acc, epilogue, out_dtypes, extras=(), tm=None, tn=None, tk=None):
    a0, b0 = pairs[0][0], pairs[0][1]
    if mode == 'nn':
        (M, K), N = a0.shape, b0.shape[1]
    elif mode == 'nt':
        (M, K), N = a0.shape, b0.shape[0]
    else:
        (K, M), N = a0.shape, b0.shape[1]
    if mode == 'tn':
        tm = tm or _pick(M, FEATURE_TILES)
        tn = tn or (N if N <= 2048 else _pick(N, FEATURE_TILES))
        tk = tk or _pick(K, [t for t in (2112,) + TOKEN_TILES
                             if 2 * t * (tm * a0.dtype.itemsize + tn * b0.dtype.itemsize) <= VMEM_LIMIT // 2])
        dims = ((0,), (0,))
        a_spec = pl.BlockSpec((tk, tm), lambda i, j, k: (k, i))
        b_spec = pl.BlockSpec((tk, tn), lambda i, j, k: (k, j))
    else:
        tm = tm or _pick(M, TOKEN_TILES)
        tn = tn or _pick(N, FEATURE_TILES)
        fixed = tm * tn * 4 * (n_acc + 2 * len(out_dtypes) + 2 * sum(kind == 'mn' for _, kind in extras))
        fits = lambda t: 2 * len(pairs) * t * (tm * a0.dtype.itemsize + tn * b0.dtype.itemsize) + fixed <= VMEM_LIMIT * 7 // 8
        tk = tk or _pick(K, [t for t in (K, 2816, 2048) + FEATURE_TILES if fits(t)])
        a_spec = pl.BlockSpec((tm, tk), lambda i, j, k: (i, k))
        if mode == 'nn':
            dims = ((1,), (0,))
            b_spec = pl.BlockSpec((tk, tn), lambda i, j, k: (k, j))
        else:
            dims = ((1,), (1,))
            b_spec = pl.BlockSpec((tn, tk), lambda i, j, k: (j, k))
    n_pairs, n_ex, n_out = len(pairs), len(extras), len(out_dtypes)
    nk = K // tk

    def body(*refs):
        ab = refs[:2 * n_pairs]
        ex = refs[2 * n_pairs:2 * n_pairs + n_ex]
        outs = refs[2 * n_pairs + n_ex:2 * n_pairs + n_ex + n_out]
        accs = refs[2 * n_pairs + n_ex + n_out:]
        k = pl.program_id(2)

        @pl.when(k == 0)
        def _():
            for acc in accs:
                acc[...] = jnp.zeros_like(acc)

        for p, (_, _, ai) in enumerate(pairs):
            accs[ai][...] += _dot(_bf(ab[2 * p][...]), _bf(ab[2 * p + 1][...]), dims)

        @pl.when(k == nk - 1)
        def _():
            res = epilogue([acc[...] for acc in accs], [e[...] for e in ex])
            for o, v in zip(outs, res):
                o[...] = v.astype(o.dtype)

    ex_specs = [pl.BlockSpec((tm, tn), lambda i, j, k: (i, j)) if kind == 'mn'
                else pl.BlockSpec((1, tn), lambda i, j, k: (0, j)) if kind == 'n'
                else pl.BlockSpec(e.shape, lambda i, j, k: (0, 0)) for e, kind in extras]
    outs = pl.pallas_call(
        body, grid=(M // tm, N // tn, nk),
        in_specs=[a_spec, b_spec] * n_pairs + ex_specs,
        out_specs=[pl.BlockSpec((tm, tn), lambda i, j, k: (i, j)) for _ in out_dtypes],
        out_shape=[jax.ShapeDtypeStruct((M, N), dt) for dt in out_dtypes],
        scratch_shapes=[pltpu.VMEM((tm, tn), F32) for _ in range(n_acc)],
        compiler_params=pltpu.CompilerParams(dimension_semantics=("arbitrary",) * 3, vmem_limit_bytes=VMEM_LIMIT),
        name=name)(*[t for a, b, _ in pairs for t in (a, b)], *[e for e, _ in extras])
    return outs


def mm_plain(name, mode, a, b, dtype=F32, scale=None, dep=None):
    ep = (lambda accs, ex: [accs[0]]) if scale is None else (lambda accs, ex: [accs[0] * scale])
    return mm_call(name, mode, [(a, b, 0)], 1, ep, [dtype], extras=[] if dep is None else [(dep, 'dep')])[0]


def scan_fwd(name, step_fn, n_outer, n_steps, params, consts, xs, states, ys):
    n_p, n_c, n_x, n_s, n_y = len(params), len(consts), len(xs), len(states), len(ys)

    def body(*refs):
        p_refs = refs[:n_p]
        c_refs = refs[n_p:n_p + n_c]
        x_refs = refs[n_p + n_c:n_p + n_c + n_x]
        o = n_p + n_c + n_x
        y_refs = refs[o:o + n_y]
        sv_refs = refs[o + n_y:o + n_y + n_s]
        st_refs = refs[o + n_y + n_s:]
        s = pl.program_id(1)

        @pl.when(s == 0)
        def _():
            for r in st_refs:
                r[...] = jnp.zeros_like(r)

        st = tuple(r[...] for r in st_refs)
        for sv, v in zip(sv_refs, st):
            sv[...] = v
        new_st, y = step_fn(tuple(r[...] for r in p_refs), st, tuple(r[...] for r in x_refs),
                            tuple(r[...] for r in c_refs))
        for r, v in zip(y_refs, y):
            r[...] = v.astype(r.dtype)
        for r, v in zip(st_refs, new_st):
            r[...] = v

    in_specs = ([pl.BlockSpec(p[1], (lambda o, s, f=p[2]: f(o))) for p in params]
                + [pl.BlockSpec(c[1], c[2]) for c in consts]
                + [pl.BlockSpec(x[1], x[2]) for x in xs])
    out_specs = ([pl.BlockSpec(b, f) for _, _, b, f in ys]
                 + [pl.BlockSpec((None, None) + tuple(sh), (lambda o, s, n=len(sh): (o, s) + (0,) * n)) for sh, _ in states])
    out_shape = ([jax.ShapeDtypeStruct(sh, dt) for sh, dt, _, _ in ys]
                 + [jax.ShapeDtypeStruct((n_outer, n_steps) + tuple(sh), dt) for sh, dt in states])
    outs = pl.pallas_call(
        body, grid=(n_outer, n_steps), in_specs=in_specs, out_specs=out_specs, out_shape=out_shape,
        scratch_shapes=[pltpu.VMEM(tuple(sh), dt) for sh, dt in states],
        compiler_params=pltpu.CompilerParams(dimension_semantics=("arbitrary", "arbitrary"), vmem_limit_bytes=VMEM_LIMIT),
        name=name)(*[p[0] for p in params], *[c[0] for c in consts], *[x[0] for x in xs])
    return tuple(outs[:n_y]), tuple(outs[n_y:])


def scan_bwd(name, step_fn, n_outer, n_steps, params, consts, xs, states, saved, dys, glob, bf16_copies=(), bf16_dxs=()):
    n_p, n_c, n_x, n_s, n_y = len(params), len(consts), len(xs), len(states), len(dys)
    n_cp = len(bf16_copies)
    rev = lambda f: (lambda o, s: f(o, n_steps - 1 - s))

    def body(*refs):
        p_refs = refs[:n_p]
        c_refs = refs[n_p:n_p + n_c]
        x_refs = refs[n_p + n_c:n_p + n_c + n_x]
        o = n_p + n_c + n_x
        sv_refs = refs[o:o + n_s]
        dy_refs = refs[o + n_s:o + n_s + n_y]
        o = o + n_s + n_y
        dx_refs = refs[o:o + n_x]
        dp_refs = refs[o + n_x:o + n_x + n_p]
        cp_refs = refs[o + n_x + n_p:o + n_x + n_p + n_cp]
        dst_refs = refs[o + n_x + n_p + n_cp:]
        oi, s = pl.program_id(0), pl.program_id(1)

        @pl.when(s == 0)
        def _():
            for r in dst_refs:
                r[...] = jnp.zeros_like(r)

        for r, g in zip(dp_refs, glob):
            @pl.when(((s == 0) & (oi == 0)) if g else (s == 0))
            def _(r=r):
                r[...] = jnp.zeros_like(r)

        c_vals = tuple(r[...] for r in c_refs)
        f = lambda p, st, x: step_fn(p, st, x, c_vals)
        _, vjp = jax.vjp(f, tuple(r[...] for r in p_refs), tuple(r[...] for r in sv_refs), tuple(r[...] for r in x_refs))
        dp, dst, dx = vjp((tuple(r[...] for r in dst_refs), tuple(r[...] for r in dy_refs)))
        for r, v in zip(dx_refs, dx):
            r[...] = v.astype(r.dtype)
        for r, i in zip(cp_refs, bf16_copies):
            r[...] = dx[i].astype(r.dtype)
        for r, v in zip(dst_refs, dst):
            r[...] = v
        for r, v in zip(dp_refs, dp):
            r[...] += v

    gshape = lambda t: t[3] if len(t) > 3 else t[0].shape
    gidx = lambda t: t[4] if len(t) > 3 else t[2]
    in_specs = ([pl.BlockSpec(p[1], (lambda o, s, f=p[2]: f(o))) for p in params]
                + [pl.BlockSpec(c[1], rev(c[2])) for c in consts]
                + [pl.BlockSpec(x[1], rev(x[2])) for x in xs]
                + [pl.BlockSpec((None, None) + tuple(sh), (lambda o, s, n=len(sh): (o, n_steps - 1 - s) + (0,) * n)) for sh, _ in states]
                + [pl.BlockSpec(b, rev(f)) for _, b, f in dys])
    out_specs = ([pl.BlockSpec(x[1], rev(gidx(x))) for x in xs]
                 + [pl.BlockSpec(p[1], (lambda o, s, f=gidx(p): f(o))) for p in params]
                 + [pl.BlockSpec(xs[i][1], rev(gidx(xs[i]))) for i in bf16_copies])
    out_shape = ([jax.ShapeDtypeStruct(gshape(x), BF16 if i in bf16_dxs else F32) for i, x in enumerate(xs)]
                 + [jax.ShapeDtypeStruct(gshape(p), F32) for p in params]
                 + [jax.ShapeDtypeStruct(gshape(xs[i]), BF16) for i in bf16_copies])
    outs = pl.pallas_call(
        body, grid=(n_outer, n_steps), in_specs=in_specs, out_specs=out_specs, out_shape=out_shape,
        scratch_shapes=[pltpu.VMEM(tuple(sh), dt) for sh, dt in states],
        compiler_params=pltpu.CompilerParams(dimension_semantics=("arbitrary", "arbitrary"), vmem_limit_bytes=VMEM_LIMIT),
        name=name)(*[p[0] for p in params], *[c[0] for c in consts], *[x[0] for x in xs], *saved,
                   *[d[0] for d in dys])
    if n_cp:
        return tuple(outs[:n_x]), tuple(outs[n_x:n_x + n_p]), tuple(outs[n_x + n_p:])
    return tuple(outs[:n_x]), tuple(outs[n_x:])


def _rows(a, rt):
    return (a, (rt, a.shape[1]), lambda o, s: (s, 0))


def _whole(a):
    return (a, a.shape, lambda o: (0,) * a.ndim)


def _rms(h, w):
    return h * lax.rsqrt(jnp.mean(h * h, axis=1, keepdims=True) + EPS) * w


def rms_fwd(name, h, w, dtype):
    TP, D = h.shape
    rt = _pick(TP, (384, 128))
    step = lambda p, st, x, c: ((), (_rms(x[0], p[0]),))
    (y,), _ = scan_fwd(name, step, 1, TP // rt, [_whole(w)], [], [_rows(h, rt)], [],
                       [((TP, D), dtype, (rt, D), lambda o, s: (s, 0))])
    return y


def rms_bwd(name, h, w, dxn, dres):
    TP, D = h.shape
    rt = _pick(TP, (384, 128))
    step = lambda p, st, x, c: ((), (_rms(x[0], p[0]), x[0]))
    (dh,), (dw,), (dh_bf,) = scan_bwd(name, step, 1, TP // rt, [_whole(w)], [], [_rows(h, rt)], [], [],
                                      [_rows(dxn, rt), _rows(dres, rt)], [True], bf16_copies=(0,))
    return (dh, dh_bf), dw


def loss_call(h, w, tgt, mask):
    TP, D = h.shape
    rt = _pick(TP, (384, 128))

    def body(h_ref, w_ref, t_ref, m_ref, loss_ref, dh_ref, dw_ref, dhb_ref):
        s = pl.program_id(0)

        @pl.when(s == 0)
        def _():
            loss_ref[...] = jnp.zeros_like(loss_ref)
            dw_ref[...] = jnp.zeros_like(dw_ref)

        tg, mk = t_ref[...], m_ref[...]

        def f(wv, hv):
            err = jnp.square(_rms(hv, wv) - tg) * mk
            return 0.5 * jnp.sum(jnp.sum(err, axis=1, keepdims=True), axis=0, keepdims=True) / D

        l, vjp = jax.vjp(f, w_ref[...], h_ref[...])
        dw, dh = vjp(jnp.ones((1, 1), F32))
        loss_ref[...] += l
        dw_ref[...] += dw
        dh_ref[...] = dh
        dhb_ref[...] = dh.astype(dhb_ref.dtype)

    row = lambda wd: pl.BlockSpec((rt, wd), lambda s: (s, 0))
    const = lambda shape: pl.BlockSpec(shape, lambda s: (0, 0))
    return pl.pallas_call(
        body, grid=(TP // rt,), in_specs=[row(D), const((1, D)), row(D), row(1)],
        out_specs=[const((1, 1)), row(D), const((1, D)), row(D)],
        out_shape=[jax.ShapeDtypeStruct((1, 1), F32), jax.ShapeDtypeStruct((TP, D), F32), jax.ShapeDtypeStruct((1, D), F32),
                   jax.ShapeDtypeStruct((TP, D), BF16)],
        compiler_params=pltpu.CompilerParams(dimension_semantics=("arbitrary",), vmem_limit_bytes=VMEM_LIMIT),
        name="loss_head")(h, w, tgt, mask)


def colsum(name, a):
    TP, N = a.shape
    rt = _pick(TP, (384, 128))

    def body(a_ref, o_ref):
        @pl.when(pl.program_id(0) == 0)
        def _():
            o_ref[...] = jnp.zeros_like(o_ref)
        o_ref[...] += jnp.sum(a_ref[...], axis=0, keepdims=True)

    return pl.pallas_call(
        body, grid=(TP // rt,), in_specs=[pl.BlockSpec((rt, N), lambda s: (s, 0))],
        out_specs=pl.BlockSpec((1, N), lambda s: (0, 0)), out_shape=jax.ShapeDtypeStruct((1, N), F32),
        compiler_params=pltpu.CompilerParams(dimension_semantics=("arbitrary",)), name=name)(a)


def ffn_fwd(tag, h, nw, wg_t, wu_t, wd):
    xn = rms_fwd(tag + "_norm", h, nw, BF16)

    def ep_up(accs, ex):
        g, u = accs
        sg = jax.nn.sigmoid(g)
        silu = g * sg
        return [u * (sg * (1.0 + g * (1.0 - sg))), silu, silu * u]

    g, u, a = mm_call(tag + "_up", 'nt', [(xn, wg_t, 0), (xn, wu_t, 1)], 2, ep_up, [BF16, BF16, BF16])
    h2 = mm_call(tag + "_down", 'nn', [(a, wd, 0)], 1, lambda accs, ex: [ex[0] + 0.5 * accs[0]], [F32],
                 extras=[(h, 'mn')])[0]
    return h2, (h, xn, g, u, a)


def ffn_bwd(tag, dh2, res, nw, wg_t, wu_t, wd, dep, send):
    h, xn, g, u, a = res
    dh2, dhb = dh2

    def ep_act(accs, ex):
        da = 0.5 * accs[0]
        return [da * ex[0].astype(F32), da * ex[1].astype(F32)]

    dg, du = mm_call(tag + "_dact", 'nt', [(dhb, wd, 0)], 1, ep_act, [BF16, BF16],
                     extras=[(g, 'mn'), (u, 'mn'), (dep, 'dep')])
    dwd = mm_plain(tag + "_dwd", 'tn', a, dhb, BF16, scale=0.5)
    tok = send('d', [dwd])
    dwg_t = mm_plain(tag + "_dwg", 'tn', dg, xn, BF16, dep=tok)
    dwu_t = mm_plain(tag + "_dwu", 'tn', du, xn, BF16)
    tok = send('gu', [dwg_t, dwu_t])
    dxn = mm_call(tag + "_dxn", 'nn', [(dg, wg_t, 0), (du, wu_t, 0)], 1, lambda accs, ex: [accs[0]], [F32],
                  extras=[(tok, 'dep')])[0]
    dh, dnw = rms_bwd(tag + "_dnorm", h, nw, dxn, dh2)
    return dh, dnw, tok


def _mlstm_step(params, state, xs, consts, *, dk):
    bif, nw = params
    c_st, n_st, m_st = state
    q, k, v, og, gr, grc = xs
    C = MLSTM_CHUNK
    R = q.shape[0]
    dv = v.shape[1]
    n = R // C
    ri, ci = _iota((R, R), 0), _iota((R, R), 1)
    sh = C.bit_length() - 1
    same = (ri >> sh) == (ci >> sh)
    causal = same & (ci <= ri)
    q = q * (dk ** -0.5)
    li_row = gr[0:1, :] + bif[0:1, 0:1]
    li_col = grc[:, 0:1] + bif[0:1, 0:1]
    b_row = _mm32(jax.nn.log_sigmoid(gr[1:2, :] + bif[1:2, 0:1]), (same & (ri <= ci)).astype(F32))
    b_col = _mm32(causal.astype(F32), jax.nn.log_sigmoid(grc[:, 1:2] + bif[1:2, 0:1]))
    m_in, m_out, b_end, a_init = [], [], [], []
    for j in range(n):
        sl = slice(j * C, (j + 1) * C)
        b_last = b_row[:, (j + 1) * C - 1:(j + 1) * C]
        log_end = b_last - b_row[:, sl] + li_row[:, sl]
        m_new = lax.stop_gradient(jnp.maximum(b_last + m_st, jnp.max(log_end, axis=1, keepdims=True)))
        m_in.append(m_st)
        m_out.append(m_new)
        b_end.append(b_last)
        a_init.append(jnp.exp(b_last + m_st - m_new))
        m_st = m_new
    rows = lambda vals: jnp.concatenate([jnp.broadcast_to(x, (C, x.shape[1])) for x in vals], axis=0)
    log_w = jnp.where(causal, b_col - b_row + li_row, -jnp.inf)
    log_init = b_col + rows(m_in)
    m_t = lax.stop_gradient(jnp.maximum(log_init, jnp.max(log_w, axis=1, keepdims=True)))
    w_init = jnp.exp(log_init - m_t)
    qk = _mm_nt(q, k) * jnp.exp(log_w - m_t)
    ka = k * jnp.exp(rows(b_end) - b_col + li_col - rows(m_out))
    row_chunk = _iota((R, 1), 0) >> sh
    kv = _mm_tn(jnp.concatenate([jnp.where(row_chunk == j, ka, 0.0) for j in range(n)], axis=1), v)
    cs, ns = [c_st], [n_st]
    for j in range(n):
        cs.append(a_init[j] * cs[j] + kv[j * dk:(j + 1) * dk])
        ns.append(a_init[j] * ns[j] + jnp.sum(ka[j * C:(j + 1) * C], axis=0, keepdims=True))
    qc = _mm(q, jnp.concatenate(cs[:n], axis=1))
    qc = jnp.concatenate([qc[j * C:(j + 1) * C, j * dv:(j + 1) * dv] for j in range(n)], axis=0)
    num = w_init * qc + _mm(qk, v)
    den = w_init * jnp.sum(q * rows(ns[:n]), axis=1, keepdims=True) + jnp.sum(qk, axis=1, keepdims=True)
    h = num / jnp.maximum(jnp.abs(den), jnp.exp(-m_t))
    hn = h * lax.rsqrt(jnp.mean(h * h, axis=1, keepdims=True) + EPS)
    return (cs[n], ns[n], m_st), (hn * nw * jax.nn.sigmoid(og),)


def _mlstm_ops(p, gr, grc, bif, nw):
    H = MLSTM_HEADS
    TP = p.shape[0]
    dv = nw.shape[1] // H
    dk = dv // 2
    R = ROW_TILE
    step = functools.partial(_mlstm_step, dk=dk)
    params = [(bif, (None, 2, LANES), lambda o: (o, 0, 0)), (nw, (1, dv), lambda o: (0, o))]
    col = lambda w, off: (p, (R, w), (lambda o, s: (s, off + o)), (TP, H * w), (lambda o, s: (s, o)))
    xs = [col(dk, 0), col(dk, H), col(dv, H), col(dv, 2 * H), (gr, (None, 2, R), lambda o, s: (o, 0, s)),
          (grc, (None, R, 2), lambda o, s: (o, s, 0))]
    states = [((dk, dv), F32), ((1, dk), F32), ((1, 1), F32)]
    ys = [((TP, H * dv), F32, (R, dv), lambda o, s: (s, o))]
    return step, H, TP // R, params, xs, states, ys


def mlstm_fwd(h, nw1, w_in_t, w_out, b_if, norm_w):
    H = MLSTM_HEADS
    D = h.shape[1]
    u = rms_fwd("mlstm_norm", h, nw1, BF16)
    p = mm_plain("mlstm_in", 'nt', u, w_in_t)
    gates = p[:, 3 * D:3 * D + 2 * H]
    gr = gates.T.reshape(2, H, -1).transpose(1, 0, 2)
    grc = gates.reshape(-1, 2, H).transpose(2, 0, 1)
    bif = jnp.broadcast_to(b_if.reshape(2, H).T[:, :, None], (H, 2, LANES))
    step, _, n, params, xs, states, ys = _mlstm_ops(p, gr, grc, bif, norm_w)
    (act,), saved = scan_fwd("mlstm_core", step, H, n, params, [], xs, states, ys)
    h2 = mm_call("mlstm_out", 'nn', [(act, w_out, 0)], 1, lambda accs, ex: [ex[0] + accs[0]], [F32], extras=[(h, 'mn')])[0]
    return h2, (h, u, p, gr, grc, bif, saved, act)


def mlstm_bwd(dh2, res, nw1, w_in_t, w_out, norm_w, dep):
    h, u, p, gr, grc, bif, saved, act = res
    H = MLSTM_HEADS
    TP = h.shape[0]
    dh2, dhb = dh2
    dact = mm_plain("mlstm_dact", 'nt', dhb, w_out, dep=dep)
    dw_out = mm_plain("mlstm_dwout", 'tn', act, dhb, BF16)
    step, _, n, params, xs, states, ys = _mlstm_ops(p, gr, grc, bif, norm_w)
    (dq, dk, dv, dog, dgr, dgrc), (dbif, dnorm) = scan_bwd("mlstm_core_bwd", step, H, n, params, [], xs, states, saved,
                                                           [(dact, ys[0][2], ys[0][3])], [False, False],
                                                           bf16_dxs=(0, 1, 2, 3))
    dgates = dgr.transpose(1, 0, 2).reshape(2 * H, TP).T + dgrc.transpose(1, 2, 0).reshape(TP, 2 * H)
    pad = p.shape[1] - (dq.shape[1] + dk.shape[1] + dv.shape[1] + dog.shape[1] + 2 * H)
    dp = jnp.concatenate([dq, dk, dv, dog, _bf(dgates), jnp.zeros((TP, pad), BF16)], axis=1)
    du = mm_plain("mlstm_du", 'nn', dp, w_in_t)
    dw_in_t = mm_plain("mlstm_dwin", 'tn', dp, u, BF16)
    dh, dnw1 = rms_bwd("mlstm_dnorm", h, nw1, du, dh2)
    db_if = dbif[:, :, 0].T.reshape(1, 2 * H)
    return dh, dict(nw=dnw1, w_in_t=dw_in_t, w_out=dw_out, b_if=db_if, norm_w=dnorm)


def _pool_step(params, state, xs, consts):
    w, scale = params
    (prev,) = state
    u, h = xs
    pos, win = consts
    R = u.shape[0]
    wn = win[0:1, 0:1]
    ext = jnp.concatenate([prev, u], axis=0)
    lag = _iota((R, 2 * R), 0) + R - _iota((R, 2 * R), 1)
    band = ((lag >= 0) & (lag < wn)).astype(F32)
    wsum = _mm32(band, ext)
    cnt = jnp.minimum(pos + 1, wn).astype(F32)
    pooled = wsum / cnt - u
    return (u,), (h + _mm(pooled, w) * scale,)


def _pool_ops(u, h, w, scale, pos):
    TP, D = u.shape
    G = D // N_POOL
    R = ROW_TILE
    win = jnp.broadcast_to(jnp.array([2 << g for g in range(N_POOL)], jnp.int32)[:, None, None], (N_POOL, 1, LANES))
    params = [(w, (None, G, G), lambda o: (o, 0, 0)), (scale, (1, G), lambda o: (0, o))]
    consts = [(pos, (R, 1), lambda o, s: (s, 0)), (win, (None, 1, LANES), lambda o, s: (o, 0, 0))]
    grp = lambda a: (a, (R, G), lambda o, s: (s, o))
    return N_POOL, TP // R, params, consts, [grp(u), grp(h)], [((R, G), F32)], [((TP, D), F32, (R, G), lambda o, s: (s, o))]


def pool_fwd(h, nw1, w, scale, pos):
    u = rms_fwd("pool_norm", h, nw1, F32)
    no, n, params, consts, xs, states, ys = _pool_ops(u, h, w, scale, pos)
    (h2,), saved = scan_fwd("pool_core", _pool_step, no, n, params, consts, xs, states, ys)
    return h2, (h, u, saved)


def pool_bwd(dh2, res, nw1, w, scale, pos):
    h, u, saved = res
    dh2 = dh2[0]
    no, n, params, consts, xs, states, ys = _pool_ops(u, h, w, scale, pos)
    (du, dres), (dw, dscale) = scan_bwd("pool_core_bwd", _pool_step, no, n, params, consts, xs, states, saved,
                                        [(dh2, ys[0][2], ys[0][3])], [False, False])
    dh, dnw1 = rms_bwd("pool_dnorm", h, nw1, du, dres)
    return dh, dict(nw=dnw1, w=dw, scale=dscale)


def _unit_lower_inverse(low, width):
    n = low.shape[0]
    ri, ci = _iota((n, n), 0), _iota((n, n), 1)
    inv = (ri == ci).astype(F32)
    b = 1
    while b < width:
        blk = 2 * b
        sh = blk.bit_length() - 1
        off = jnp.where(((ri >> sh) == (ci >> sh)) & ((ri & (blk - 1)) >= b) & ((ci & (blk - 1)) < b), low, 0.0)
        inv = inv - (off if b == 1 else _mm(_mm(inv, off), inv))
        b = blk
    return inv


def _unit_lower_solve(width):
    @jax.custom_vjp
    def solve(low, rhs):
        return _mm3(_unit_lower_inverse(low, width), rhs)

    def fwd(low, rhs):
        inv = _unit_lower_inverse(low, width)
        sol = _mm3(inv, rhs)
        return sol, (inv, sol)

    def bwd(saved, g):
        inv, sol = saved
        d_rhs = _dot(inv, g, ((0,), (0,)), precision=HIGH)
        return -_dot(d_rhs, sol, ((1,), (1,)), precision=HIGH), d_rhs

    solve.defvjp(fwd, bwd)
    return solve


def _conv_silu(prev, x, w):
    R = x.shape[0]
    ext = jnp.concatenate([prev, x], axis=0)
    y = sum(w[j:j + 1, :] * ext[8 - (GDN_CONV - 1) + j:8 - (GDN_CONV - 1) + j + R] for j in range(GDN_CONV))
    return jax.nn.silu(y)


def _gdn_step(params, state, xs, consts):
    cwq, cwk, cwv, ad, gnw = params
    s_cat, pq, pk, pv = state
    q, k, v, z, gb, gbc = xs
    n = gb.shape[0]
    dk = q.shape[1] // n
    one = lambda a, i, w: a[:, i * w:(i + 1) * w]
    states, outs = [], []
    for i in range(n):
        st, (y,) = _gdn_head_step(
            (one(cwq, i, dk), one(cwk, i, dk), one(cwv, i, 2 * dk), ad[i], gnw),
            (one(s_cat, i, 2 * dk), one(pq, i, dk), one(pk, i, dk), one(pv, i, 2 * dk)),
            (one(q, i, dk), one(k, i, dk), one(v, i, 2 * dk), one(z, i, 2 * dk), gb[i], gbc[i]), consts)
        states.append(st)
        outs.append(y)
    cat = lambda j: jnp.concatenate([st[j] for st in states], axis=1)
    return (cat(0), cat(1), cat(2), cat(3)), (jnp.concatenate(outs, axis=1),)


def _gdn_head_step(params, state, xs, consts):
    cwq, cwk, cwv, ad, gnw = params
    s_cat, pq, pk, pv = state
    q, k, v, z, gb, gbc = xs
    C = GDN_CHUNK
    R, dk = q.shape
    dv = v.shape[1] // 2
    ri, ci = _iota((R, R), 0), _iota((R, R), 1)
    sh = C.bit_length() - 1
    same = (ri >> sh) == (ci >> sh)
    causal, strict = same & (ci <= ri), same & (ci < ri)
    qc, kc, vc = _conv_silu(pq, q, cwq), _conv_silu(pk, k, cwk), _conv_silu(pv, v, cwv)
    qn = qc * lax.rsqrt(jnp.sum(qc * qc, axis=1, keepdims=True) + EPS) * (dk ** -0.5)
    kn = kc * lax.rsqrt(jnp.sum(kc * kc, axis=1, keepdims=True) + EPS)
    kk, qk = _mm_nt(kn, kn), _mm_nt(qn, kn)
    g_rows = -jnp.exp(ad[0:2, 0:1]) * jax.nn.softplus(gb[2:4, :] + ad[2:4, 0:1])
    gc_rows = _mm3(g_rows, (same & (ri <= ci)).astype(F32))
    g_cols = jnp.concatenate([-jnp.exp(ad[e:e + 1, 0:1]) * jax.nn.softplus(gbc[:, 2 + e:3 + e] + ad[2 + e:3 + e, 0:1])
                              for e in range(2)], axis=1)
    gc_cols = _mm3(causal.astype(F32), g_cols)
    beta_cols = jax.nn.sigmoid(gbc[:, 0:2])
    lows, rhss, attns, qgs = [], [], [], []
    for e in range(2):
        gcc, bc = gc_cols[:, e:e + 1], beta_cols[:, e:e + 1]
        decay = jnp.exp(jnp.where(causal, gcc - gc_rows[e:e + 1, :], -jnp.inf))
        lows.append(jnp.where(strict, kk * bc * decay, 0.0))
        attns.append(qk * decay)
        eg = jnp.exp(gcc)
        rhss.append(jnp.concatenate([vc[:, e * dv:(e + 1) * dv] * bc, kn * (bc * eg)], axis=1))
        qgs.append(qn * eg)
    zero = jnp.zeros((R, R), F32)
    big = jnp.concatenate([jnp.concatenate([lows[0], zero], axis=1), jnp.concatenate([zero, lows[1]], axis=1)], axis=0)
    sol = _unit_lower_solve(C)(big, jnp.concatenate(rhss, axis=0))
    outs = [[], []]
    n_chunks = R // C
    for j in range(n_chunks):
        sl = slice(j * C, (j + 1) * C)
        lhs = jnp.concatenate([sol[sl, dv:], sol[R + j * C:R + (j + 1) * C, dv:], qgs[0][sl], qgs[1][sl]], axis=0)
        pr = _mm(lhs, s_cat)
        v_new = [sol[e * R + j * C:e * R + (j + 1) * C, :dv] - pr[e * C:(e + 1) * C, e * dv:(e + 1) * dv] for e in range(2)]
        v_lanes = jnp.concatenate(v_new, axis=1)
        pad = jnp.zeros((C, 2 * dv), F32)
        v_rows = jnp.concatenate([v_lanes if i == j else pad for i in range(n_chunks)], axis=0)
        av = _mm(jnp.concatenate([attns[0][sl], attns[1][sl]], axis=0), v_rows)
        g_last = [gc_rows[e:e + 1, (j + 1) * C - 1:(j + 1) * C] for e in range(2)]
        kg = jnp.concatenate([kn[sl] * jnp.exp(g_last[e] - gc_cols[sl, e:e + 1]) for e in range(2)], axis=1)
        kv = _mm_tn(kg, v_lanes)
        s_cat = jnp.concatenate([jnp.exp(g_last[e]) * s_cat[:, e * dv:(e + 1) * dv]
                                 + kv[e * dk:(e + 1) * dk, e * dv:(e + 1) * dv] for e in range(2)], axis=1)
        for e in range(2):
            o = pr[(2 + e) * C:(3 + e) * C, e * dv:(e + 1) * dv] + av[e * C:(e + 1) * C, e * dv:(e + 1) * dv]
            on = o * lax.rsqrt(jnp.mean(o * o, axis=1, keepdims=True) + EPS) * gnw
            outs[e].append(on * jax.nn.silu(z[sl, e * dv:(e + 1) * dv]))
    out = jnp.concatenate([jnp.concatenate(outs[0], axis=0), jnp.concatenate(outs[1], axis=0)], axis=1)
    return (s_cat, q[R - 8:], k[R - 8:], v[R - 8:]), (out,)


def _gdn_ops(p, gb, gbc, conv_w, ad, gnw):
    TP = p.shape[0]
    dk = GDN_DK
    nqk = gb.shape[0]
    hp = GDN_HEADS_PER_STEP
    no = nqk // hp
    R = ROW_TILE
    cw = lambda w, off: (conv_w, (GDN_CONV, hp * w), (lambda o: (0, off + o)), (GDN_CONV, nqk * w), (lambda o: (0, o)))
    params = [cw(dk, 0), cw(dk, no), cw(2 * dk, no), (ad, (hp, 4, LANES), lambda o: (o, 0, 0)), _whole(gnw)]
    col = lambda w, off: (p, (R, hp * w), (lambda o, s: (s, off + o)), (TP, nqk * w), (lambda o, s: (s, o)))
    xs = [col(dk, 0), col(dk, no), col(2 * dk, no), col(2 * dk, 2 * no), (gb, (hp, 4, R), lambda o, s: (o, 0, s)),
          (gbc, (hp, R, 4), lambda o, s: (o, s, 0))]
    states = [((dk, hp * 2 * dk), F32), ((8, hp * dk), F32), ((8, hp * dk), F32), ((8, hp * 2 * dk), F32)]
    ys = [((TP, 2 * nqk * dk), F32, (R, hp * 2 * dk), lambda o, s: (s, o))]
    return no, TP // R, params, xs, states, ys


def gdn_fwd(h, nw1, w_in_t, w_out, conv_w, a_log, dt_bias, gnw):
    D = h.shape[1]
    nqk = D // GDN_DK
    u = rms_fwd("gdn_norm", h, nw1, BF16)
    p = mm_plain("gdn_in", 'nt', u, w_in_t)
    gates = p[:, 6 * D:6 * D + 4 * nqk]
    gb = gates.T.reshape(2, nqk, 2, -1).transpose(1, 0, 2, 3).reshape(nqk, 4, -1)
    gbc = gates.reshape(-1, 2, nqk, 2).transpose(2, 0, 1, 3).reshape(nqk, -1, 4)
    ad = jnp.concatenate([a_log.reshape(nqk, 2), dt_bias.reshape(nqk, 2)], axis=1)
    ad = jnp.broadcast_to(ad[:, :, None], (nqk, 4, LANES))
    no, n, params, xs, states, ys = _gdn_ops(p, gb, gbc, conv_w, ad, gnw)
    (act,), saved = scan_fwd("gdn_core", _gdn_step, no, n, params, [], xs, states, ys)
    h2 = mm_call("gdn_out", 'nn', [(act, w_out, 0)], 1, lambda accs, ex: [ex[0] + accs[0]], [F32], extras=[(h, 'mn')])[0]
    return h2, (h, u, p, gb, gbc, ad, saved, act)


def gdn_bwd(dh2, res, nw1, w_in_t, w_out, conv_w, gnw, dep):
    h, u, p, gb, gbc, ad, saved, act = res
    TP, D = h.shape
    nqk = D // GDN_DK
    dh2, dhb = dh2
    dact = mm_plain("gdn_dact", 'nt', dhb, w_out, dep=dep)
    dw_out = mm_plain("gdn_dwout", 'tn', act, dhb, BF16)
    no, n, params, xs, states, ys = _gdn_ops(p, gb, gbc, conv_w, ad, gnw)
    (dq, dk, dv, dz, dgb, dgbc), (dcq, dck, dcv, dad, dgnw) = scan_bwd(
        "gdn_core_bwd", _gdn_step, no, n, params, [], xs, states, saved, [(dact, ys[0][2], ys[0][3])],
        [False, False, False, False, True], bf16_dxs=(0, 1, 2, 3))
    dgates = (dgb.reshape(nqk, 2, 2, TP).transpose(1, 0, 2, 3).reshape(4 * nqk, TP).T
              + dgbc.reshape(nqk, TP, 2, 2).transpose(1, 2, 0, 3).reshape(TP, 4 * nqk))
    pad = p.shape[1] - (6 * D + 4 * nqk)
    dp = jnp.concatenate([dq, dk, dv, dz, _bf(dgates), jnp.zeros((TP, pad), BF16)], axis=1)
    du = mm_plain("gdn_du", 'nn', dp, w_in_t)
    dw_in_t = mm_plain("gdn_dwin", 'tn', dp, u, BF16)
    dh, dnw1 = rms_bwd("gdn_dnorm", h, nw1, du, dh2)
    dad = dad[:, :, 0]
    return dh, dict(nw=dnw1, w_in_t=dw_in_t, w_out=dw_out, conv_w=jnp.concatenate([dcq, dck, dcv], axis=1),
                    a_log=dad[:, :2].reshape(1, 2 * nqk), dt_bias=dad[:, 2:].reshape(1, 2 * nqk), norm_w=dgnw)


def _rope(x, cos, sin):
    W = x.shape[1]
    first_half = (_iota(x.shape, 1) & (SWA_DH - 1)) < SWA_DH // 2
    rot = jnp.where(first_half, -_lane_roll(W - SWA_DH // 2)(x), _lane_roll(SWA_DH // 2)(x))
    return x * jnp.tile(cos, (1, W // LANES)) + rot * jnp.tile(sin, (1, W // LANES))


def _swa_step(params, state, xs, consts):
    (sinks,) = params
    kprev, vprev = state
    q, k, v = xs
    cos, sin, pos = consts
    R = q.shape[0]
    hkv = k.shape[1] // SWA_DH
    G = SWA_GROUP
    qr, kr = _rope(q, cos, sin), _rope(k, cos, sin)
    k2, v2 = jnp.concatenate([kprev, kr], axis=0), jnp.concatenate([vprev, v], axis=0)
    lane = _iota((R, LANES), 1)
    qpos = jnp.concatenate([pos] * G, axis=0)
    kpos = pos[0:1, 0:1] - R + _iota((1, 2 * R), 1)
    mask = (kpos <= qpos) & (qpos - kpos < SWA_WINDOW) & (kpos >= 0)
    sel_r, sel_c = _iota((hkv * SWA_DH, LANES), 0), _iota((hkv * SWA_DH, LANES), 1)
    out = []
    for hh in range(hkv):
        sel = _bf((sel_r == hh * SWA_DH + (sel_c & (SWA_DH - 1))).astype(F32))
        kd, vd = _mm(k2, sel), _mm(v2, sel)
        q8 = []
        for i in range(G // 2):
            qb = qr[:, (hh * G // 2 + i) * LANES:(hh * G // 2 + i + 1) * LANES]
            q8 += [jnp.where(lane < SWA_DH, qb, 0.0), jnp.where(lane >= SWA_DH, qb, 0.0)]
        s = _mm_nt(jnp.concatenate(q8, axis=0), kd) * (SWA_DH ** -0.5)
        s = jnp.where(mask, s, -jnp.inf)
        sink = jnp.concatenate([jnp.broadcast_to(sinks[0:1, hh * G + g:hh * G + g + 1], (R, 1)) for g in range(G)], axis=0)
        m = lax.stop_gradient(jnp.maximum(jnp.max(s, axis=1, keepdims=True), sink))
        e = jnp.exp(s - m)
        prob = e / (jnp.sum(e, axis=1, keepdims=True) + jnp.exp(sink - m))
        o8 = _mm(prob, vd)
        for i in range(G // 2):
            out.append(jnp.where(lane < SWA_DH, o8[2 * i * R:(2 * i + 1) * R], o8[(2 * i + 1) * R:(2 * i + 2) * R]))
    return (kr, v), (jnp.concatenate(out, axis=1),)


def _swa_ops(p, sinks, cos, sin, pos):
    TP = p.shape[0]
    R = ROW_TILE
    hq = sinks.shape[1]
    wq, wkv = hq * SWA_DH, hq // SWA_GROUP * SWA_DH
    nb = wq // wkv
    xs = [(p, (R, wq), (lambda o, s: (s, 0)), (TP, wq), (lambda o, s: (s, 0))),
          (p, (R, wkv), (lambda o, s: (s, nb)), (TP, wkv), (lambda o, s: (s, 0))),
          (p, (R, wkv), (lambda o, s: (s, nb + 1)), (TP, wkv), (lambda o, s: (s, 0)))]
    consts = [_rows(cos, R), _rows(sin, R), _rows(pos, R)]
    states = [((R, wkv), F32), ((R, wkv), F32)]
    ys = [((TP, wq), F32, (R, wq), lambda o, s: (s, 0))]
    return TP // R, [_whole(sinks)], consts, xs, states, ys


def swa_fwd(h, nw1, w_qkv_t, b_qkv, w_out, b_out, sinks, cos, sin, pos):
    u = rms_fwd("swa_norm", h, nw1, BF16)
    p = mm_call("swa_in", 'nt', [(u, w_qkv_t, 0)], 1, lambda accs, ex: [accs[0] + ex[0]], [F32], extras=[(b_qkv, 'n')])[0]
    n, params, consts, xs, states, ys = _swa_ops(p, sinks, cos, sin, pos)
    (act,), saved = scan_fwd("swa_core", _swa_step, 1, n, params, consts, xs, states, ys)
    h2 = mm_call("swa_out", 'nn', [(act, w_out, 0)], 1, lambda accs, ex: [ex[0] + accs[0] + ex[1]], [F32],
                 extras=[(h, 'mn'), (b_out, 'n')])[0]
    return h2, (h, u, p, saved, act)


def swa_bwd(dh2, res, nw1, w_qkv_t, w_out, sinks, cos, sin, pos, dep):
    h, u, p, saved, act = res
    dh2, dhb = dh2
    dact = mm_plain("swa_dact", 'nt', dhb, w_out, dep=dep)
    dw_out = mm_plain("swa_dwout", 'tn', act, dhb, BF16)
    db_out = colsum("swa_dbout", dh2)
    n, params, consts, xs, states, ys = _swa_ops(p, sinks, cos, sin, pos)
    (dq, dk, dv), (dsinks,) = scan_bwd("swa_core_bwd", _swa_step, 1, n, params, consts, xs, states, saved,
                                       [(dact, ys[0][2], ys[0][3])], [True])
    dp = jnp.concatenate([dq, dk, dv], axis=1)
    db_qkv = colsum("swa_dbqkv", dp)
    du = mm_plain("swa_du", 'nn', dp, w_qkv_t)
    dw_qkv_t = mm_plain("swa_dwqkv", 'tn', dp, u, BF16)
    dh, dnw1 = rms_bwd("swa_dnorm", h, nw1, du, dh2)
    return dh, dict(nw=dnw1, w_qkv_t=dw_qkv_t, w_out=dw_out, b_qkv=db_qkv, b_out=db_out, sinks=dsinks)


def _dev_index(dev):
    return 4 * dev[0] + 2 * dev[1] + dev[2]


def all_gather(name, shards):
    n = len(shards)

    def body(*refs):
        x_refs, out_refs = refs[:n], refs[n:2 * n]
        send_sems, recv_sems, local_sem = refs[2 * n:]
        x, y, c = lax.axis_index("x"), lax.axis_index("y"), lax.axis_index("c")
        me, sibling = (x, y, c), (x, y, 1 - c)
        chips = [(1 - x, y), (x, 1 - y), (1 - x, 1 - y)]

        def copy(a, k, block, to, src=None):
            dst = out_refs[a].at[_dev_index(block)]
            return pltpu.make_async_remote_copy(src_ref=dst if src is None else src, dst_ref=dst,
                                                send_sem=send_sems.at[a, k], recv_sem=recv_sems.at[a, k],
                                                device_id=to, device_id_type=MESH_ID)

        mine = [pltpu.make_async_copy(x_refs[a], out_refs[a].at[_dev_index(me)], local_sem.at[a]) for a in range(n)]
        first, passed = [], []
        for a in range(n):
            mine[a].start()
            first += [copy(a, 0, me, sibling, src=x_refs[a])]
            first += [copy(a, 1 + j, me, (*chip, c), src=x_refs[a]) for j, chip in enumerate(chips)]
        for cp in first:
            cp.start()
        for a in range(n):
            for j, chip in enumerate(chips):
                copy(a, 1 + j, (*chip, c), me).wait_recv()
                fwd = copy(a, 4 + j, (*chip, c), sibling)
                fwd.start()
                passed.append(fwd)
        for a in range(n):
            copy(a, 0, sibling, me).wait_recv()
            for j, chip in enumerate(chips):
                copy(a, 4 + j, (*chip, 1 - c), me).wait_recv()
        for cp in first + passed:
            cp.wait_send()
        for cp in mine:
            cp.wait()

    any_spec = pl.BlockSpec(memory_space=pl.ANY)
    return pl.pallas_call(
        body, in_specs=[any_spec] * n, out_specs=[any_spec] * n,
        out_shape=[jax.ShapeDtypeStruct((N_DEV,) + s.shape, s.dtype) for s in shards],
        scratch_shapes=[pltpu.SemaphoreType.DMA((n, 7)), pltpu.SemaphoreType.DMA((n, 7)), pltpu.SemaphoreType.DMA((n,))],
        name=name)(*shards)


def exchange_blocks(name, fulls):
    n = len(fulls)

    def body(*refs):
        g_refs, out_refs = refs[:n], refs[n:2 * n]
        send_sems, recv_sems, local_sem = refs[2 * n:]
        x, y, c = lax.axis_index("x"), lax.axis_index("y"), lax.axis_index("c")
        me = (x, y, c)
        peers = [(1 - x if r & 4 else x, 1 - y if r & 2 else y, 1 - c if r & 1 else c) for r in range(1, N_DEV)]

        def copy(a, k, peer):
            return pltpu.make_async_remote_copy(src_ref=g_refs[a].at[_dev_index(peer)], dst_ref=out_refs[a].at[_dev_index(me)],
                                                send_sem=send_sems.at[a, k], recv_sem=recv_sems.at[a, k],
                                                device_id=peer, device_id_type=MESH_ID)

        mine = [pltpu.make_async_copy(g_refs[a].at[_dev_index(me)], out_refs[a].at[_dev_index(me)], local_sem.at[a])
                for a in range(n)]
        sends = [copy(a, k, peer) for a in range(n) for k, peer in enumerate(peers)]
        for cp in mine + sends:
            cp.start()
        for a in range(n):
            for k, peer in enumerate(peers):
                pltpu.make_async_remote_copy(src_ref=g_refs[a].at[_dev_index(peer)], dst_ref=out_refs[a].at[_dev_index(peer)],
                                             send_sem=send_sems.at[a, k], recv_sem=recv_sems.at[a, k],
                                             device_id=peer, device_id_type=MESH_ID).wait_recv()
        for cp in sends:
            cp.wait_send()
        for cp in mine:
            cp.wait()

    any_spec = pl.BlockSpec(memory_space=pl.ANY)
    return pl.pallas_call(
        body, in_specs=[any_spec] * n, out_specs=[any_spec] * n,
        out_shape=[jax.ShapeDtypeStruct(g.shape, g.dtype) for g in fulls],
        scratch_shapes=[pltpu.SemaphoreType.DMA((n, 7)), pltpu.SemaphoreType.DMA((n, 7)), pltpu.SemaphoreType.DMA((n,))],
        name=name)(*fulls)


def _peers_of(x, y, c):
    return [(1 - x if r & 4 else x, 1 - y if r & 2 else y, 1 - c if r & 1 else c) for r in range(1, N_DEV)]


def push_start(name, srcs, lands, after, gather):
    n = len(srcs)

    def body(*refs):
        src_refs, land_refs = refs[:n], refs[n:2 * n]
        send_sems, recv_sems = refs[2 * n + 1], refs[2 * n + 2]
        token = refs[-1]
        x, y, c = lax.axis_index("x"), lax.axis_index("y"), lax.axis_index("c")
        me = (x, y, c)
        for a in range(n):
            for k, peer in enumerate(_peers_of(x, y, c)):
                pltpu.make_async_remote_copy(
                    src_ref=src_refs[a] if gather else src_refs[a].at[_dev_index(peer)],
                    dst_ref=land_refs[a].at[_dev_index(me)], send_sem=send_sems.at[a * (N_DEV - 1) + k],
                    recv_sem=recv_sems.at[a * (N_DEV - 1) + k],
                    device_id=peer, device_id_type=MESH_ID).start()
        token[...] = jnp.zeros_like(token)

    hbm = pl.BlockSpec(memory_space=pltpu.HBM)
    sem = pl.BlockSpec(memory_space=pltpu.SEMAPHORE)
    outs = pl.pallas_call(
        body, name=name,
        out_shape=(pltpu.SemaphoreType.DMA((n * (N_DEV - 1),)), pltpu.SemaphoreType.DMA((n * (N_DEV - 1),)),
                   *[pltpu.HBM(s.shape, s.dtype) for s in srcs], *[pltpu.HBM(l.shape, l.dtype) for l in lands],
                   jax.ShapeDtypeStruct((8, LANES), F32)),
        in_specs=[hbm] * (2 * n) + [pl.BlockSpec(memory_space=pl.ANY)],
        out_specs=(sem, sem, *[hbm] * (2 * n), pl.BlockSpec(memory_space=pltpu.VMEM)),
        input_output_aliases={i: 2 + i for i in range(2 * n)},
        compiler_params=pltpu.CompilerParams(has_side_effects=pltpu.SideEffectType.DATAFLOW_SIDE_EFFECTING),
    )(*[pltpu.with_memory_space_constraint(s, pltpu.HBM) for s in srcs],
      *[pltpu.with_memory_space_constraint(l, pltpu.HBM) for l in lands], after)
    return (outs[0], outs[1], outs[2:2 + n], outs[2 + n:2 + 2 * n], gather), outs[-1]


def push_wait(name, handle, after):
    send_sems, recv_sems, srcs, lands, gather = handle
    n = len(srcs)

    def body(*refs):
        src_refs, land_refs = refs[:n], refs[n:2 * n]
        send_sem_ref, recv_sem_ref = refs[2 * n], refs[2 * n + 1]
        x, y, c = lax.axis_index("x"), lax.axis_index("y"), lax.axis_index("c")
        for a in range(n):
            for k, peer in enumerate(_peers_of(x, y, c)):
                cp = pltpu.make_async_remote_copy(
                    src_ref=src_refs[a] if gather else src_refs[a].at[_dev_index(peer)],
                    dst_ref=land_refs[a].at[_dev_index(peer)], send_sem=send_sem_ref.at[a * (N_DEV - 1) + k],
                    recv_sem=recv_sem_ref.at[a * (N_DEV - 1) + k],
                    device_id=peer, device_id_type=MESH_ID)
                cp.wait_send()
                cp.wait_recv()

    hbm = pl.BlockSpec(memory_space=pltpu.HBM)
    sem = pl.BlockSpec(memory_space=pltpu.SEMAPHORE)
    outs = pl.pallas_call(
        body, name=name,
        out_shape=(*[pltpu.HBM(s.shape, s.dtype) for s in srcs], *[pltpu.HBM(l.shape, l.dtype) for l in lands]),
        in_specs=[hbm] * (2 * n) + [sem, sem, pl.BlockSpec(memory_space=pl.ANY)],
        out_specs=tuple([hbm] * (2 * n)),
        input_output_aliases={i: i for i in range(2 * n)},
        compiler_params=pltpu.CompilerParams(has_side_effects=pltpu.SideEffectType.DATAFLOW_SIDE_EFFECTING),
    )(*srcs, *lands, send_sems, recv_sems, after)
    return outs[n:]


def _split_call(name, body, ins, sem_ins, after, n_sem_out, with_token):
    hbm = pl.BlockSpec(memory_space=pltpu.HBM)
    sem = pl.BlockSpec(memory_space=pltpu.SEMAPHORE)
    n = len(ins)
    out_shape = ([pltpu.SemaphoreType.DMA((s,)) for s in n_sem_out] + [pltpu.HBM(a.shape, a.dtype) for a in ins]
                 + ([jax.ShapeDtypeStruct((8, LANES), F32)] if with_token else []))
    out_specs = [sem] * len(n_sem_out) + [hbm] * n + ([pl.BlockSpec(memory_space=pltpu.VMEM)] if with_token else [])
    outs = pl.pallas_call(
        body, name=name, out_shape=tuple(out_shape), out_specs=tuple(out_specs),
        in_specs=[hbm] * n + [sem] * len(sem_ins) + [pl.BlockSpec(memory_space=pl.ANY)],
        input_output_aliases={i: len(n_sem_out) + i for i in range(n)},
        compiler_params=pltpu.CompilerParams(has_side_effects=pltpu.SideEffectType.DATAFLOW_SIDE_EFFECTING),
    )(*ins, *sem_ins, after)
    k = len(n_sem_out)
    return outs[:k], outs[k:k + n], (outs[-1] if with_token else None)


def _chips_of(x, y):
    return [(1 - x, y), (x, 1 - y), (1 - x, 1 - y)]


def gather_start(name, shards, lands, after):
    n = len(shards)

    def body(*refs):
        src, land = refs[:n], refs[n:2 * n]
        send, recv = refs[2 * n + 1], refs[2 * n + 2]
        token = refs[-1]
        x, y, c = lax.axis_index("x"), lax.axis_index("y"), lax.axis_index("c")
        to = [(x, y, 1 - c)] + [(*chip, c) for chip in _chips_of(x, y)]
        for a in range(n):
            for k, dev in enumerate(to):
                pltpu.make_async_remote_copy(src_ref=src[a], dst_ref=land[a].at[_dev_index((x, y, c))],
                                             send_sem=send.at[4 * a + k], recv_sem=recv.at[4 * a + k],
                                             device_id=dev, device_id_type=MESH_ID).start()
        token[...] = jnp.zeros_like(token)

    ins = [pltpu.with_memory_space_constraint(a, pltpu.HBM) for a in list(shards) + list(lands)]
    sems, thru, token = _split_call(name, body, ins, [], after, [4 * n, 4 * n], True)
    return dict(n=n, send=sems[0], recv=sems[1], shards=thru[:n], lands=thru[n:]), token


def gather_pass(name, h, after):
    n = h['n']

    def body(*refs):
        land, recv = refs[:n], refs[n]
        send2, recv2 = refs[n + 2], refs[n + 3]
        token = refs[-1]
        x, y, c = lax.axis_index("x"), lax.axis_index("y"), lax.axis_index("c")
        for a in range(n):
            for j, chip in enumerate(_chips_of(x, y)):
                blk = land[a].at[_dev_index((*chip, c))]
                pltpu.make_async_remote_copy(src_ref=blk, dst_ref=blk, send_sem=send2.at[3 * a + j], recv_sem=recv.at[4 * a + 1 + j],
                                             device_id=(*chip, c), device_id_type=MESH_ID).wait_recv()
                pltpu.make_async_remote_copy(src_ref=blk, dst_ref=blk, send_sem=send2.at[3 * a + j], recv_sem=recv2.at[3 * a + j],
                                             device_id=(x, y, 1 - c), device_id_type=MESH_ID).start()
        token[...] = jnp.zeros_like(token)

    sems, lands, token = _split_call(name, body, list(h['lands']), [h['recv']], after, [3 * n, 3 * n], True)
    return dict(h, lands=lands, send2=sems[0], recv2=sems[1]), token


def gather_wait(name, h, after):
    n = h['n']

    def body(*refs):
        src, land = refs[:n], refs[n:2 * n]
        send, recv, send2, recv2 = refs[2 * n:2 * n + 4]
        x, y, c = lax.axis_index("x"), lax.axis_index("y"), lax.axis_index("c")
        sib = (x, y, 1 - c)
        for a in range(n):
            mine = land[a].at[_dev_index((x, y, c))]
            for k in range(4):
                pltpu.make_async_remote_copy(src_ref=src[a], dst_ref=mine, send_sem=send.at[4 * a + k], recv_sem=recv.at[4 * a + k],
                                             device_id=sib, device_id_type=MESH_ID).wait_send()
            theirs = land[a].at[_dev_index(sib)]
            pltpu.make_async_remote_copy(src_ref=src[a], dst_ref=theirs, send_sem=send.at[4 * a], recv_sem=recv.at[4 * a],
                                         device_id=sib, device_id_type=MESH_ID).wait_recv()
            for j, chip in enumerate(_chips_of(x, y)):
                sent, got = land[a].at[_dev_index((*chip, c))], land[a].at[_dev_index((*chip, 1 - c))]
                cp = pltpu.make_async_remote_copy(src_ref=sent, dst_ref=got, send_sem=send2.at[3 * a + j], recv_sem=recv2.at[3 * a + j],
                                                  device_id=sib, device_id_type=MESH_ID)
                cp.wait_send()
                cp.wait_recv()

    _, thru, _ = _split_call(name, body, list(h['shards']) + list(h['lands']), [h['send'], h['recv'], h['send2'], h['recv2']],
                             after, [], False)
    return thru[n:]


def _own_block(block, me):
    land = lax.empty((N_DEV,) + block.shape, block.dtype)
    return lax.dynamic_update_slice(land, block[None], (me,) + (0,) * block.ndim)


def sum_blocks(name, parts):
    _, r, c = parts.shape
    tr = _pick(r, ROW_TILES)

    def body(p_ref, o_ref):
        acc = p_ref[0].astype(F32)
        for b in range(1, N_DEV):
            acc = acc + p_ref[b].astype(F32)
        o_ref[...] = acc

    return pl.pallas_call(
        body, grid=(r // tr,), in_specs=[pl.BlockSpec((N_DEV, tr, c), lambda i: (0, i, 0))],
        out_specs=pl.BlockSpec((tr, c), lambda i: (i, 0)), out_shape=jax.ShapeDtypeStruct((r, c), F32),
        compiler_params=pltpu.CompilerParams(dimension_semantics=("arbitrary",), vmem_limit_bytes=VMEM_LIMIT),
        name=name)(parts)


def adamw(name, w, g, m, v):
    lead, (r, c) = w.shape[:-2], w.shape[-2:]
    fits = lambda tr_, tc_: tr_ * tc_ * 4 * 7 * 2 <= VMEM_LIMIT // 2
    row_tiles = [t for t in ROW_TILES if t >= 64 and r % t == 0 and fits(t, c)]
    if row_tiles:
        tr, tc = row_tiles[0], c
    else:
        tr, tc = r, _pick(c, [t for t in (2048, 1024, 512, 256, 128) if fits(r, t)])

    def body(w_ref, g_ref, m_ref, v_ref, d_ref, nm_ref, nv_ref):
        gv = g_ref[...]
        nm = ADAM_B1 * m_ref[...] + (1.0 - ADAM_B1) * gv
        nv = ADAM_B2 * v_ref[...] + (1.0 - ADAM_B2) * jnp.square(gv)
        m_hat = nm / (1.0 - ADAM_B1 ** ADAM_STEP)
        v_hat = nv / (1.0 - ADAM_B2 ** ADAM_STEP)
        d_ref[...] = -ADAM_LR * (m_hat / (jnp.sqrt(v_hat) + ADAM_EPS) + ADAM_WD * w_ref[...])
        nm_ref[...] = nm
        nv_ref[...] = nv

    spec = pl.BlockSpec((None,) * len(lead) + (tr, tc), lambda *idx: idx)
    return pl.pallas_call(
        body, grid=(*lead, r // tr, c // tc), in_specs=[spec] * 4, out_specs=[spec] * 3,
        out_shape=[jax.ShapeDtypeStruct(w.shape, F32)] * 3,
        compiler_params=pltpu.CompilerParams(dimension_semantics=("arbitrary",) * (len(lead) + 2),
                                             vmem_limit_bytes=VMEM_LIMIT),
        name=name)(w, g, m, v)


def _pack(arrays):
    flat = jnp.concatenate([a.reshape(-1) for a in arrays])
    n = _round_up(flat.shape[0], 8 * LANES)
    return jnp.pad(flat, (0, n - flat.shape[0])).reshape(-1, LANES)


def _unpack(packed, shapes):
    flat = packed.reshape(-1)
    out, o = [], 0
    for sh in shapes:
        sz = 1
        for d in sh:
            sz *= d
        out.append(flat[o:o + sz].reshape(sh))
        o += sz
    return out


def _gather_axis(g, ax):
    g = jnp.moveaxis(g, 0, ax)
    return g.reshape(g.shape[:ax] + (g.shape[ax] * g.shape[ax + 1],) + g.shape[ax + 2:])


def _comm_rows(a):
    r = a.shape[0]
    rp = r if r % 16 == 0 else _round_up(r, ROW_PAD)
    return jnp.pad(_bf(a), ((0, rp - r), (0, 0)))


def _natural(g, r, pad_to=None):
    full = g[:, :r].reshape(N_DEV * r, g.shape[2])
    if pad_to is not None and full.shape[0] % pad_to:
        full = jnp.pad(full, ((0, _round_up(full.shape[0], pad_to) - full.shape[0]), (0, 0)))
    return full


def _blocked(full, r):
    blocks = full[:N_DEV * r].reshape(N_DEV, r, full.shape[1])
    rp = r if r % 16 == 0 else _round_up(r, ROW_PAD)
    return jnp.pad(_bf(blocks), ((0, 0), (0, rp - r), (0, 0)))


def kernel(x, *rest):
    nw_ = len(WEIGHTS)
    W = dict(zip(WEIGHTS, rest[:nw_]))
    loss_target = rest[nw_]
    M = dict(zip(WEIGHTS, rest[nw_ + 1:2 * nw_ + 1]))
    V = dict(zip(WEIGHTS, rest[2 * nw_ + 1:3 * nw_ + 1]))

    T0, D = x.shape[1], x.shape[2]
    T = T0 + N_META
    TP = _round_up(T, ROW_TILE)
    me = 4 * lax.axis_index("x") + 2 * lax.axis_index("y") + lax.axis_index("c")

    small_sharded = [k for k in WEIGHTS if k in SMALL and SMALL[k] is not None]
    (sg,) = all_gather("ag_small", [_pack([W[k] for k in small_sharded])])
    per_dev = [_unpack(sg[b], [W[k].shape for k in small_sharded]) for b in range(N_DEV)]
    full = {k: _gather_axis(jnp.stack([per_dev[b][i] for b in range(N_DEV)]), SMALL[k]) for i, k in enumerate(small_sharded)}
    for k in SMALL:
        if SMALL[k] is None:
            full[k] = W[k]

    r_ff = W['ffn_w_gate'].shape[3]
    r_mi, r_mo = W['mlstm_w_in'].shape[2], W['mlstm_w_out'].shape[1]
    r_gi, r_go = W['gdn_w_in'].shape[2], W['gdn_w_out'].shape[1]
    r_si, r_so = W['swa_w_qkv'].shape[2], W['swa_w_out'].shape[1]
    pw = W['pool_w'][0]
    stages = [(i, j) for i in range(DEPTH) for j in range(3)]

    def stage_shards(i, j):
        if j != 1:
            s = j // 2
            return [_comm_rows(W['ffn_w_gate'][i, s].T), _comm_rows(W['ffn_w_up'][i, s].T), _comm_rows(W['ffn_w_down'][i, s])]
        if i % 4 == 0:
            return [_comm_rows(W['mlstm_w_in'][0].T), _comm_rows(W['mlstm_w_out'][0])]
        if i % 4 == 1:
            return [_bf(pw.reshape(-1, pw.shape[2]))]
        if i % 4 == 2:
            return [_comm_rows(W['gdn_w_in'][0].T), _comm_rows(W['gdn_w_out'][0])]
        return [_comm_rows(W['swa_w_qkv'][0].T), _comm_rows(W['swa_w_out'][0])]

    def stage_weights(i, j, g):
        if j != 1:
            return tuple(_natural(a, r_ff) for a in g)
        if i % 4 == 0:
            return (_natural(g[0], r_mi, IN_PAD), _natural(g[1], r_mo))
        if i % 4 == 1:
            return (g[0].reshape(N_DEV, N_POOL, pw.shape[1], pw.shape[2]).transpose(1, 0, 2, 3)
                    .reshape(N_POOL, pw.shape[2], pw.shape[2]).astype(F32),)
        if i % 4 == 2:
            return (_natural(g[0], r_gi, IN_PAD), _natural(g[1], r_go))
        return (_natural(g[0], r_si), _natural(g[1], r_so))

    def start_gather(k, after):
        sh = stage_shards(*stages[k])
        return gather_start(f"ag_start_{k}", sh, [_own_block(s, me) for s in sh], after)

    first = all_gather("ag_stage_0", stage_shards(*stages[0]))
    wts = {0: stage_weights(*stages[0], first)}
    pending, passed, zero = {}, set(), jnp.zeros((), F32)
    ahead = 3
    tok, after = zero, first[0]
    for k in range(1, ahead + 1):
        pending[k], after = start_gather(k, after)
        tok = tok + after[0, 0]

    pos = jnp.arange(TP, dtype=jnp.int32)[:, None]
    inv = ROPE_THETA ** (-jnp.arange(0, SWA_DH, 2, dtype=F32) / SWA_DH)
    ang = jnp.arange(TP, dtype=F32)[:, None] * inv[None, :]
    ang = jnp.concatenate([ang, ang, ang, ang], axis=1)
    cos, sin = jnp.cos(ang), jnp.sin(ang)
    row_mask = ((pos >= N_META) & (pos < T)).astype(F32)
    tgt = jnp.pad(loss_target[0], ((N_META, TP - T), (0, 0)))
    nrm = lambda i, j: full['norm_w'][i, j][None, :]

    h = jnp.concatenate([full['meta_tokens'], x[0], jnp.zeros((TP - T, D), F32)], axis=0)
    res = {}
    for k, (i, j) in enumerate(stages):
        if k >= 1:
            tok = zero
            if k not in passed:
                pending[k], _ = gather_pass(f"ag_pass_{k}", pending[k], h)
            wts[k] = stage_weights(i, j, gather_wait(f"ag_wait_{k}", pending.pop(k), h))
            if k + 1 < len(stages):
                pending[k + 1], t = gather_pass(f"ag_pass_{k + 1}", pending[k + 1], h)
                passed.add(k + 1)
                tok = tok + t[0, 0]
            if k + ahead < len(stages):
                pending[k + ahead], t = start_gather(k + ahead, h)
                tok = tok + t[0, 0]
        nw = nrm(i, j) + tok
        w = wts[k]
        if j != 1:
            h, res[k] = ffn_fwd(f"ffn_{i}_{j // 2}", h, nw, *w)
        elif i % 4 == 0:
            h, res[k] = mlstm_fwd(h, nw, w[0], w[1], full['mlstm_b_if'], full['mlstm_norm_w'])
        elif i % 4 == 1:
            h, res[k] = pool_fwd(h, nw, w[0], full['pool_scale'], pos)
        elif i % 4 == 2:
            h, res[k] = gdn_fwd(h, nw, w[0], w[1], full['gdn_conv_w'][0], full['gdn_a_log'], full['gdn_dt_bias'],
                                full['gdn_norm_w'])
        else:
            h, res[k] = swa_fwd(h, nw, w[0], full['swa_b_qkv'], w[1], full['swa_b_out'], full['swa_sinks'], cos, sin, pos)

    loss_local, dh_f32, d_final, dh_bf = loss_call(h, full['final_norm_w'][None, :], tgt, row_mask)
    dh = (dh_f32, dh_bf)
    loss = lax.psum(loss_local[0, 0], ("x", "y", "c"))

    gs = {'final_norm_w': d_final[0]}
    d_norm = [[None] * 3 for _ in range(DEPTH)]
    sent = []
    dep = jnp.zeros((8, LANES), F32)

    def exchange(k, part, blocks, after):
        lands = [_own_block(lax.dynamic_index_in_dim(b, me, 0, keepdims=False), me) for b in blocks]
        handle, token = push_start(f"rs_start_{k}_{part}", blocks, lands, after, False)
        sent.append((k, part, handle))
        return token

    for k in reversed(range(len(stages))):
        i, j = stages[k]
        nw, w = nrm(i, j), wts[k]
        if j != 1:
            send = lambda part, gs_, k=k: exchange(k, part, [_blocked(g, r_ff) for g in gs_], gs_[0])
            dh, d_norm[i][j], dep = ffn_bwd(f"ffn_{i}_{j // 2}", dh, res[k], nw, *w, dep, send)
            continue
        else:
            if i % 4 == 0:
                dh, gm = mlstm_bwd(dh, res[k], nw, w[0], w[1], full['mlstm_norm_w'], dep)
                blocks = [_blocked(gm['w_in_t'], r_mi), _blocked(gm['w_out'], r_mo)]
                gs.update(mlstm_b_if=gm['b_if'], mlstm_norm_w=gm['norm_w'])
            elif i % 4 == 1:
                dh, gm = pool_bwd(dh, res[k], nw + dep[0, 0], w[0], full['pool_scale'], pos)
                dw = gm['w'].reshape(N_POOL, N_DEV, pw.shape[1], pw.shape[2]).transpose(1, 0, 2, 3)
                blocks = [_bf(dw.reshape(N_DEV, N_POOL * pw.shape[1], pw.shape[2]))]
                gs.update(pool_scale=gm['scale'])
            elif i % 4 == 2:
                dh, gm = gdn_bwd(dh, res[k], nw, w[0], w[1], full['gdn_conv_w'][0], full['gdn_norm_w'], dep)
                blocks = [_blocked(gm['w_in_t'], r_gi), _blocked(gm['w_out'], r_go)]
                gs.update(gdn_conv_w=gm['conv_w'][None], gdn_a_log=gm['a_log'], gdn_dt_bias=gm['dt_bias'],
                          gdn_norm_w=gm['norm_w'])
            else:
                dh, gm = swa_bwd(dh, res[k], nw, w[0], w[1], full['swa_sinks'], cos, sin, pos, dep)
                blocks = [_blocked(gm['w_qkv_t'], r_si), _blocked(gm['w_out'], r_so)]
                gs.update(swa_b_qkv=gm['b_qkv'], swa_b_out=gm['b_out'], swa_sinks=gm['sinks'])
            d_norm[i][1] = gm['nw']
        dep = exchange(k, 'm', blocks, dh[0])
    gs['norm_w'] = jnp.stack([jnp.concatenate(d_norm[i], axis=0) for i in range(DEPTH)])
    gs['meta_tokens'] = dh[0][:N_META]
    grad_x = dh[0][N_META:T][None]

    grads = {}
    small_names = [k for k in WEIGHTS if k in SMALL]
    (parts,) = all_gather("ag_small_grads", [_pack([gs[k].reshape(full[k].shape) for k in small_names])])
    tot = _unpack(sum_blocks("sum_small_grads", parts), [full[k].shape for k in small_names])
    for k, g in zip(small_names, tot):
        ax = SMALL[k]
        grads[k] = g if ax is None else lax.dynamic_slice_in_dim(g, me * W[k].shape[ax], W[k].shape[ax], axis=ax)

    summed, after = {}, dh[0]
    for k, part, handle in sent:
        parts = push_wait(f"rs_wait_{k}_{part}", handle, after)
        summed[k, part] = [sum_blocks(f"sum_{k}_{part}_{n}", p) for n, p in enumerate(parts)]
        after = summed[k, part][0]

    swap = lambda a: jnp.swapaxes(a, -1, -2)
    grads_t = {}
    gg, gu, gd = [], [], []
    for i in range(DEPTH):
        for j in (0, 2):
            a, b = summed[3 * i + j, 'gu']
            gg.append(a[:r_ff])
            gu.append(b[:r_ff])
            gd.append(summed[3 * i + j, 'd'][0][:r_ff])
    shape4 = lambda lst, shape: jnp.stack(lst).reshape(shape)
    grads_t['ffn_w_gate'] = shape4(gg, swap(W['ffn_w_gate']).shape)
    grads_t['ffn_w_up'] = shape4(gu, swap(W['ffn_w_up']).shape)
    grads['ffn_w_down'] = shape4(gd, W['ffn_w_down'].shape)
    for i in range(DEPTH):
        g = summed[3 * i + 1, 'm']
        if i % 4 == 0:
            grads_t['mlstm_w_in'], grads['mlstm_w_out'] = g[0][:r_mi][None], g[1][:r_mo][None]
        elif i % 4 == 1:
            grads['pool_w'] = g[0].reshape(W['pool_w'].shape)
        elif i % 4 == 2:
            grads_t['gdn_w_in'], grads['gdn_w_out'] = g[0][:r_gi][None], g[1][:r_go][None]
        else:
            grads_t['swa_w_qkv'], grads['swa_w_out'] = g[0][:r_si][None], g[1][:r_so][None]
    for k in grads_t:
        grads[k] = swap(grads_t[k])

    delta, new_m, new_v = {}, {}, {}
    shapes = [W[k].shape for k in small_names]
    d, nm, nv = adamw("adamw_small", _pack([W[k] for k in small_names]), _pack([grads[k] for k in small_names]),
                      _pack([M[k] for k in small_names]), _pack([V[k] for k in small_names]))
    for k, a, b, c_ in zip(small_names, _unpack(d, shapes), _unpack(nm, shapes), _unpack(nv, shapes)):
        delta[k], new_m[k], new_v[k] = a, b, c_
    for k in WEIGHTS:
        if k in grads_t:
            delta[k], new_m[k], new_v[k] = (swap(a) for a in adamw("adamw_" + k, swap(W[k]), grads_t[k], swap(M[k]), swap(V[k])))
        elif k not in SMALL:
            delta[k], new_m[k], new_v[k] = adamw("adamw_" + k, W[k], grads[k].reshape(W[k].shape), M[k], V[k])

    return (loss, grad_x, *[grads[k].reshape(W[k].shape) for k in WEIGHTS], *[delta[k] for k in WEIGHTS],
            *[new_m[k] for k in WEIGHTS], *[new_v[k] for k in WEIGHTS])
```
